```python
import math
import jax, jax.numpy as jnp
from jax import lax
import numpy as np

D_MODEL = 2048
BATCH = 16
SEQ = 2048
DEPTH = 4

MEM_LEN = 256
N_HEADS_TOTAL = 16
HEAD_DIM = D_MODEL // N_HEADS_TOTAL
N_DIFF = 4
N_SB = 6
N_FOX = N_HEADS_TOTAL - N_DIFF - N_SB
MIX_WIDTH = N_HEADS_TOTAL * HEAD_DIM
DIFF_QK_DIM = HEAD_DIM // 2
A_W = N_DIFF * HEAD_DIM
SB_W = N_SB * HEAD_DIM
FOX_W = N_FOX * HEAD_DIM
IN_COLS = 3 * A_W + 3 * SB_W + 3 * FOX_W + N_FOX
N_CROSS_HEADS = 4
CROSS_HEAD_DIM = D_MODEL // 16
CROSS_W = N_CROSS_HEADS * CROSS_HEAD_DIM
D_FF = -(-8 * D_MODEL // (3 * 256)) * 256
BLOCK_Q = 128
ROPE_THETA = 10000.0
EPS = 1e-6
NEG_INF = -1e30

kernel_name = "hymba_diff_stickbreak_fox_trunk"


def rms_norm(x, g):
    xf = x.astype(jnp.float32)
    y = xf * lax.rsqrt(jnp.mean(xf * xf, axis=-1, keepdims=True) + EPS)
    return (y * g.astype(jnp.float32)).astype(x.dtype)


def rope(x, positions):
    d = x.shape[-1]
    half = d // 2
    inv_freq = ROPE_THETA ** (-jnp.arange(half, dtype=jnp.float32) / half)
    ang = positions.astype(jnp.float32)[..., None] * inv_freq
    cos = jnp.cos(ang)[:, :, None, :]
    sin = jnp.sin(ang)[:, :, None, :]
    x1 = x[..., :half].astype(jnp.float32)
    x2 = x[..., half:].astype(jnp.float32)
    out = jnp.concatenate([x1 * cos - x2 * sin, x2 * cos + x1 * sin], axis=-1)
    return out.astype(x.dtype)


def split_heads(t, n, d):
    b, s, _ = t.shape
    return t.reshape(b, s, n, d).transpose(0, 2, 1, 3)


def merge_heads(t):
    b, h, s, d = t.shape
    return t.transpose(0, 2, 1, 3).reshape(b, s, h * d)


def block_size(s):
    return math.gcd(s, BLOCK_Q)


def to_blocks(t, bq):
    b, h, s = t.shape[:3]
    rest = t.shape[3:]
    t = t.reshape((b, h, s // bq, bq) + rest)
    return jnp.moveaxis(t, 2, 0)


def from_blocks(t):
    nb, b, h, bq, d = t.shape
    return t.transpose(1, 2, 0, 3, 4).reshape(b, h, nb * bq, d)


def query_block_index(s, bq):
    return jnp.arange(s, dtype=jnp.int32).reshape(s // bq, bq)


def diff_attention(q1, q2, k1, k2, v, lam):
    s = q1.shape[2]
    bq = block_size(s)
    scale = DIFF_QK_DIM ** -0.5
    k_idx = jnp.arange(s, dtype=jnp.int32)

    def one_block(args):
        qa, qb, qi = args
        causal = qi[:, None] >= k_idx[None, :]
        s1 = jnp.einsum('bhqd,bhkd->bhqk', qa, k1).astype(jnp.float32) * scale
        s2 = jnp.einsum('bhqd,bhkd->bhqk', qb, k2).astype(jnp.float32) * scale
        p1 = jax.nn.softmax(jnp.where(causal, s1, NEG_INF), axis=-1)
        p2 = jax.nn.softmax(jnp.where(causal, s2, NEG_INF), axis=-1)
        w = p1 - lam * p2
        return jnp.einsum('bhqk,bhkd->bhqd', w.astype(v.dtype), v)

    out = lax.map(one_block, (to_blocks(q1, bq), to_blocks(q2, bq), query_block_index(s, bq)))
    return from_blocks(out)


def stick_breaking_attention(q, k, v):
    s = q.shape[2]
    bq = block_size(s)
    scale = HEAD_DIM ** -0.5
    k_idx = jnp.arange(s, dtype=jnp.int32)

    def one_block(args):
        qb, qi = args
        z = jnp.einsum('bhqd,bhkd->bhqk', qb, k).astype(jnp.float32) * scale
        strict = qi[:, None] > k_idx[None, :]
        log_remain = jnp.where(strict, jax.nn.log_sigmoid(-z), 0.0)
        after = lax.cumsum(log_remain, axis=3, reverse=True) - log_remain
        w = jnp.where(strict, jnp.exp(jax.nn.log_sigmoid(z) + after), 0.0)
        return jnp.einsum('bhqk,bhkd->bhqd', w.astype(v.dtype), v)

    out = lax.map(one_block, (to_blocks(q, bq), query_block_index(s, bq)))
    return from_blocks(out)


def forgetting_attention(q, k, v, log_f):
    s = q.shape[2]
    bq = block_size(s)
    scale = HEAD_DIM ** -0.5
    k_idx = jnp.arange(s, dtype=jnp.int32)
    cum = lax.cumsum(log_f, axis=2)

    def one_block(args):
        qb, cq, qi = args
        logits = jnp.einsum('bhqd,bhkd->bhqk', qb, k).astype(jnp.float32) * scale
        logits = logits + (cq[:, :, :, None] - cum[:, :, None, :])
        causal = qi[:, None] >= k_idx[None, :]
        p = jax.nn.softmax(jnp.where(causal, logits, NEG_INF), axis=-1)
        return jnp.einsum('bhqk,bhkd->bhqd', p.astype(v.dtype), v)

    out = lax.map(one_block, (to_blocks(q, bq), to_blocks(cum, bq), query_block_index(s, bq)))
    return from_blocks(out)


def hybrid_mixer(h, positions, layer_idx, w_in, b_f, lam_q1, lam_k1, lam_q2, lam_k2,
                 g_diff_sub, g_sb_out, g_fox_out, w_out):
    b, s, _ = h.shape
    proj = h @ w_in
    sizes = [A_W, A_W, A_W, SB_W, SB_W, SB_W, FOX_W, FOX_W, FOX_W]
    points = [int(p) for p in np.cumsum(sizes)]
    qa, ka, va, qb, kb, vb, qc, kc, vc, fc = jnp.split(proj, points, axis=-1)

    qa = rope(qa.reshape(b, s, 2 * N_DIFF, DIFF_QK_DIM), positions).reshape(b, s, N_DIFF, 2, DIFF_QK_DIM)
    ka = rope(ka.reshape(b, s, 2 * N_DIFF, DIFF_QK_DIM), positions).reshape(b, s, N_DIFF, 2, DIFF_QK_DIM)
    q1 = qa[:, :, :, 0, :].transpose(0, 2, 1, 3)
    q2 = qa[:, :, :, 1, :].transpose(0, 2, 1, 3)
    k1 = ka[:, :, :, 0, :].transpose(0, 2, 1, 3)
    k2 = ka[:, :, :, 1, :].transpose(0, 2, 1, 3)
    lam_init = 0.8 - 0.6 * math.exp(-0.3 * layer_idx)
    lam = (jnp.exp(jnp.sum(lam_q1.astype(jnp.float32) * lam_k1.astype(jnp.float32)))
           - jnp.exp(jnp.sum(lam_q2.astype(jnp.float32) * lam_k2.astype(jnp.float32)))) + lam_init
    oa = diff_attention(q1, q2, k1, k2, split_heads(va, N_DIFF, HEAD_DIM), lam)
    oa = merge_heads(rms_norm(oa, g_diff_sub) * (1.0 - lam_init))

    ob = stick_breaking_attention(split_heads(qb, N_SB, HEAD_DIM), split_heads(kb, N_SB, HEAD_DIM),
                                  split_heads(vb, N_SB, HEAD_DIM))
    ob = rms_norm(merge_heads(ob), g_sb_out)

    log_f = jax.nn.log_sigmoid((fc + b_f).astype(jnp.float32)).transpose(0, 2, 1)
    oc = forgetting_attention(split_heads(qc, N_FOX, HEAD_DIM), split_heads(kc, N_FOX, HEAD_DIM),
                              split_heads(vc, N_FOX, HEAD_DIM), log_f)
    oc = rms_norm(merge_heads(oc), g_fox_out)

    merged = jnp.concatenate([oa, ob, oc], axis=-1)
    return merged @ w_out


def memory_cross_attention(h, mem_n, w_cq, w_ckv, w_co):
    b, s, _ = h.shape
    m = mem_n.shape[1]
    q = (h @ w_cq).reshape(b, s, N_CROSS_HEADS, CROSS_HEAD_DIM)
    kv = mem_n @ w_ckv
    k = kv[:, :, :CROSS_W].reshape(b, m, N_CROSS_HEADS, CROSS_HEAD_DIM)
    v = kv[:, :, CROSS_W:].reshape(b, m, N_CROSS_HEADS, CROSS_HEAD_DIM)
    logits = jnp.einsum('bqhd,bmhd->bhqm', q, k).astype(jnp.float32) * CROSS_HEAD_DIM ** -0.5
    p = jax.nn.softmax(logits, axis=-1)
    o = jnp.einsum('bhqm,bmhd->bqhd', p.astype(v.dtype), v).reshape(b, s, CROSS_W)
    return o @ w_co


def swiglu(h, w_gate, w_up, w_down):
    return (jax.nn.silu(h @ w_gate) * (h @ w_up)) @ w_down


def setup_inputs(seed: int = 0) -> dict:
    key = jax.random.key(seed)
    ks = jax.random.split(key, 32)
    f32 = jnp.float32

    def dense(k, shape, fan_in):
        return jax.random.normal(k, shape, f32) * fan_in ** -0.5

    def gain(k, shape):
        return 1.0 + 0.05 * jax.random.normal(k, shape, f32)

    x = jax.random.normal(ks[0], (BATCH, SEQ, D_MODEL), f32)
    mem = jax.random.normal(ks[1], (BATCH, MEM_LEN, D_MODEL), f32)
    offsets = jax.random.randint(ks[2], (BATCH, 1), 0, SEQ, dtype=jnp.int32)
    positions = offsets + jnp.arange(SEQ, dtype=jnp.int32)[None, :]
    return {
        "x": x,
        "mem": mem,
        "positions": positions,
        "g_mix_pre": gain(ks[3], (DEPTH, D_MODEL)),
        "g_mix_post": gain(ks[4], (DEPTH, D_MODEL)),
        "w_in": dense(ks[5], (DEPTH, D_MODEL, IN_COLS), D_MODEL),
        "b_f": 2.0 + 0.5 * jax.random.normal(ks[6], (DEPTH, N_FOX), f32),
        "lam_q1": 0.1 * jax.random.normal(ks[7], (DEPTH, DIFF_QK_DIM), f32),
        "lam_k1": 0.1 * jax.random.normal(ks[8], (DEPTH, DIFF_QK_DIM), f32),
        "lam_q2": 0.1 * jax.random.normal(ks[9], (DEPTH, DIFF_QK_DIM), f32),
        "lam_k2": 0.1 * jax.random.normal(ks[10], (DEPTH, DIFF_QK_DIM), f32),
        "g_diff_sub": gain(ks[11], (DEPTH, HEAD_DIM)),
        "g_sb_out": gain(ks[12], (DEPTH, SB_W)),
        "g_fox_out": gain(ks[13], (DEPTH, FOX_W)),
        "w_out": dense(ks[14], (DEPTH, MIX_WIDTH, D_MODEL), MIX_WIDTH),
        "g_x_pre": gain(ks[15], (DEPTH, D_MODEL)),
        "g_x_post": gain(ks[16], (DEPTH, D_MODEL)),
        "g_mem": gain(ks[17], (DEPTH, D_MODEL)),
        "w_cq": dense(ks[18], (DEPTH, D_MODEL, CROSS_W), D_MODEL),
        "w_ckv": dense(ks[19], (DEPTH, D_MODEL, 2 * CROSS_W), D_MODEL),
        "w_co": dense(ks[20], (DEPTH, CROSS_W, D_MODEL), CROSS_W),
        "g_ffn_pre": gain(ks[21], (DEPTH, D_MODEL)),
        "g_ffn_post": gain(ks[22], (DEPTH, D_MODEL)),
        "w_gate": dense(ks[23], (DEPTH, D_MODEL, D_FF), D_MODEL),
        "w_up": dense(ks[24], (DEPTH, D_MODEL, D_FF), D_MODEL),
        "w_down": dense(ks[25], (DEPTH, D_FF, D_MODEL), D_FF),
    }


def reference(x, mem, positions, g_mix_pre, g_mix_post, w_in, b_f, lam_q1, lam_k1, lam_q2, lam_k2,
              g_diff_sub, g_sb_out, g_fox_out, w_out, g_x_pre, g_x_post, g_mem, w_cq, w_ckv, w_co,
              g_ffn_pre, g_ffn_post, w_gate, w_up, w_down):
    for l in range(DEPTH):
        h = rms_norm(x, g_mix_pre[l])
        y = hybrid_mixer(h, positions, l, w_in[l], b_f[l], lam_q1[l], lam_k1[l], lam_q2[l], lam_k2[l],
                         g_diff_sub[l], g_sb_out[l], g_fox_out[l], w_out[l])
        x = x + rms_norm(y, g_mix_post[l])
        h = rms_norm(x, g_x_pre[l])
        mem_n = rms_norm(mem, g_mem[l])
        y = memory_cross_attention(h, mem_n, w_cq[l], w_ckv[l], w_co[l])
        x = x + rms_norm(y, g_x_post[l])
        h = rms_norm(x, g_ffn_pre[l])
        y = swiglu(h, w_gate[l], w_up[l], w_down[l])
        x = x + rms_norm(y, g_ffn_post[l])
    return x
```

```python
import functools
import math

import jax
import jax.numpy as jnp
from jax import lax
from jax.experimental import pallas as pl
from jax.experimental.pallas import tpu as pltpu

F32 = jnp.float32
BF16 = jnp.bfloat16

EPS = 1e-6
NEG_INF = -1e30
ROPE_THETA = 10000.0

LANES = 128
HEAD_DIM = 128
N_DIFF, N_SB, N_FOX = 4, 6, 6
DIFF_QK_DIM = HEAD_DIM // 2
A_W, SB_W, FOX_W = N_DIFF * HEAD_DIM, N_SB * HEAD_DIM, N_FOX * HEAD_DIM
QKV_COLS = 3 * (A_W + SB_W + FOX_W)
N_CROSS_HEADS = 4
CROSS_W = N_CROSS_HEADS * HEAD_DIM
GATE_ROWS = 8

VMEM_LIMIT = 56 * 1024 * 1024

QA_BLK, KA_BLK, VA_BLK = 0, A_W // LANES, 2 * A_W // LANES
QB_BLK = 3 * A_W // LANES
KB_BLK, VB_BLK = QB_BLK + N_SB, QB_BLK + 2 * N_SB
QC_BLK = QB_BLK + 3 * N_SB
KC_BLK, VC_BLK = QC_BLK + N_FOX, QC_BLK + 2 * N_FOX


def _params(*semantics):
    return pltpu.CompilerParams(dimension_semantics=semantics, vmem_limit_bytes=VMEM_LIMIT)


def _rms(xf, g):
    return xf * lax.rsqrt(jnp.mean(xf * xf, axis=-1, keepdims=True) + EPS) * g


def _dot(a, b):
    return jnp.dot(a, b, preferred_element_type=F32)


def _dot_nt(a, b):
    return lax.dot_general(a, b, (((1,), (1,)), ((), ())), preferred_element_type=F32)


def _log_sigmoid_parts(z):
    return jnp.log1p(jnp.exp(-jnp.abs(z)))


def _prenorm_kernel(x_ref, g_ref, h_ref):
    h_ref[...] = _rms(x_ref[...], g_ref[...]).astype(BF16)


def prenorm(x, g, *, bm):
    m, d = x.shape
    return pl.pallas_call(
        _prenorm_kernel,
        grid=(m // bm,),
        in_specs=[pl.BlockSpec((bm, d), lambda i: (i, 0)), pl.BlockSpec((1, d), lambda i: (0, 0))],
        out_specs=pl.BlockSpec((bm, d), lambda i: (i, 0)),
        out_shape=jax.ShapeDtypeStruct((m, d), BF16),
        compiler_params=_params("parallel"),
        name="prenorm",
    )(x, g.reshape(1, d))


def _rope_table_kernel(pos_ref, invf_ref, sign_ref, cos_ref, sin_ref):
    ang = pos_ref[...].astype(F32) * invf_ref[...]
    cos_ref[...] = jnp.cos(ang)
    sin_ref[...] = jnp.sin(ang) * sign_ref[...]


def rope_tables(positions, *, bm):
    m = positions.size
    half = DIFF_QK_DIM // 2
    inv_freq = ROPE_THETA ** (-jnp.arange(half, dtype=F32) / half)
    invf = jnp.tile(inv_freq, LANES // half).reshape(1, LANES)
    sign = jnp.where((jnp.arange(LANES) % DIFF_QK_DIM) < half, -1.0, 1.0).astype(F32).reshape(1, LANES)
    row = pl.BlockSpec((bm, LANES), lambda i: (i, 0))
    const = pl.BlockSpec((1, LANES), lambda i: (0, 0))
    return pl.pallas_call(
        _rope_table_kernel,
        grid=(m // bm,),
        in_specs=[pl.BlockSpec((bm, 1), lambda i: (i, 0)), const, const],
        out_specs=[row, row],
        out_shape=[jax.ShapeDtypeStruct((m, LANES), F32)] * 2,
        compiler_params=_params("parallel"),
        name="rope_tables",
    )(positions.reshape(m, 1), invf, sign)


def _in_proj_kernel(h_ref, w_ref, cs_ref, cos_ref, sin_ref, o_ref, *, n_rope_blocks):
    j = pl.program_id(1)
    acc = _dot(h_ref[...], w_ref[...]) * cs_ref[...]
    bm, bn = acc.shape

    @pl.when(j < n_rope_blocks)
    def _():
        c, s = cos_ref[...], sin_ref[...]
        first_half = (lax.broadcasted_iota(jnp.int32, (bm, LANES), 1) % DIFF_QK_DIM) < DIFF_QK_DIM // 2
        for t in range(bn // LANES):
            a = acc[:, t * LANES:(t + 1) * LANES]
            partner = jnp.where(first_half, pltpu.roll(a, LANES - DIFF_QK_DIM // 2, 1),
                                pltpu.roll(a, DIFF_QK_DIM // 2, 1))
            o_ref[:, t * LANES:(t + 1) * LANES] = (a * c + partner * s).astype(BF16)

    @pl.when(j >= n_rope_blocks)
    def _():
        o_ref[...] = acc.astype(BF16)


def in_proj(h, w, colscale, cos_t, sin_t, *, bm, bn):
    m, d = h.shape
    n = w.shape[1]
    rope_cols = 2 * A_W
    assert rope_cols % bn == 0 and n % bn == 0 and m % bm == 0
    row_tab = pl.BlockSpec((bm, LANES), lambda i, j: (i, 0))
    return pl.pallas_call(
        functools.partial(_in_proj_kernel, n_rope_blocks=rope_cols // bn),
        grid=(m // bm, n // bn),
        in_specs=[pl.BlockSpec((bm, d), lambda i, j: (i, 0)),
                  pl.BlockSpec((d, bn), lambda i, j: (0, j)),
                  pl.BlockSpec((1, bn), lambda i, j: (0, j)),
                  row_tab, row_tab],
        out_specs=pl.BlockSpec((bm, bn), lambda i, j: (i, j)),
        out_shape=jax.ShapeDtypeStruct((m, n), BF16),
        compiler_params=_params("parallel", "arbitrary"),
        name="in_proj",
    )(h, w, colscale, cos_t, sin_t)


def _forget_cumsum_kernel(h_ref, wf_ref, bf_ref, cum_ref):
    s = h_ref.shape[0]
    fc = _dot_nt(wf_ref[...], h_ref[...]) + bf_ref[...]
    acc = jnp.minimum(fc, 0.0) - _log_sigmoid_parts(fc)
    lane = lax.broadcasted_iota(jnp.int32, acc.shape, 1)
    k = 1
    while k < s:
        acc = acc + jnp.where(lane >= k, pltpu.roll(acc, k, 1), 0.0)
        k *= 2
    cum_ref[...] = acc


def forget_cumsum(h, wf_t, bf, *, batch):
    m, d = h.shape
    s = m // batch
    return pl.pallas_call(
        _forget_cumsum_kernel,
        grid=(batch,),
        in_specs=[pl.BlockSpec((s, d), lambda b: (b, 0)),
                  pl.BlockSpec((GATE_ROWS, d), lambda b: (0, 0)),
                  pl.BlockSpec((GATE_ROWS, 1), lambda b: (0, 0))],
        out_specs=pl.BlockSpec((None, GATE_ROWS, s), lambda b: (b, 0, 0)),
        out_shape=jax.ShapeDtypeStruct((batch, GATE_ROWS, s), F32),
        compiler_params=_params("parallel"),
        name="forget_cumsum",
    )(h, wf_t, bf)


def _causal_masks(tq):
    row = lax.broadcasted_iota(jnp.int32, (tq, tq), 0)
    col = lax.broadcasted_iota(jnp.int32, (tq, tq), 1)
    return row, col


def _softmax_step(s, v, m_ref, l_ref, acc_ref):
    m_prev = m_ref[...]
    m_new = jnp.maximum(m_prev, jnp.max(s, axis=1, keepdims=True))
    alpha = jnp.exp(m_prev - m_new)
    p = jnp.exp(s - m_new)
    l_ref[...] = alpha * l_ref[...] + jnp.sum(p, axis=1, keepdims=True)
    acc_ref[...] = alpha * acc_ref[...] + _dot(p.astype(BF16), v)
    m_ref[...] = m_new


def _diff_attn_kernel(lq_ref, lk_ref, g_ref, q_ref, k_ref, v_ref, o_ref,
                      m1_ref, l1_ref, acc1_ref, m2_ref, l2_ref, acc2_ref, *, tq, lam_init):
    i = pl.program_id(2)
    q = q_ref[...]
    low = lax.broadcasted_iota(jnp.int32, q.shape, 1) < DIFF_QK_DIM
    zero = jnp.zeros_like(q)
    q1 = jnp.where(low, q, zero)
    q2 = jnp.where(low, zero, q)
    for m_ref, l_ref, acc_ref in ((m1_ref, l1_ref, acc1_ref), (m2_ref, l2_ref, acc2_ref)):
        m_ref[...] = jnp.full(m_ref.shape, NEG_INF, F32)
        l_ref[...] = jnp.zeros(l_ref.shape, F32)
        acc_ref[...] = jnp.zeros(acc_ref.shape, F32)

    def step(kb, masked):
        k = k_ref[pl.ds(pl.multiple_of(kb * tq, tq), tq), :]
        v = v_ref[pl.ds(pl.multiple_of(kb * tq, tq), tq), :]
        s1, s2 = _dot_nt(q1, k), _dot_nt(q2, k)
        if masked:
            row, col = _causal_masks(tq)
            s1 = jnp.where(row >= col, s1, NEG_INF)
            s2 = jnp.where(row >= col, s2, NEG_INF)
        _softmax_step(s1, v, m1_ref, l1_ref, acc1_ref)
        _softmax_step(s2, v, m2_ref, l2_ref, acc2_ref)

    def body(kb, carry):
        step(kb, False)
        return carry

    lax.fori_loop(0, i, body, 0)
    step(i, True)

    prod = lq_ref[...] * lk_ref[...]
    first = lax.broadcasted_iota(jnp.int32, prod.shape, 1) < DIFF_QK_DIM
    e1 = jnp.exp(jnp.sum(jnp.where(first, prod, 0.0), axis=1, keepdims=True))
    e2 = jnp.exp(jnp.sum(jnp.where(first, 0.0, prod), axis=1, keepdims=True))
    lam = (e1 - e2) + lam_init
    o = acc1_ref[...] / l1_ref[...] - lam * (acc2_ref[...] / l2_ref[...])
    o_ref[...] = (_rms(o, g_ref[...]) * (1.0 - lam_init)).astype(BF16)


def diff_attention(proj, lam_q, lam_k, g_sub, lam_init, *, batch, tq):
    m = proj.shape[0]
    s = m // batch
    nq = s // tq
    vec = pl.BlockSpec((1, LANES), lambda b, h, i: (0, 0))
    return pl.pallas_call(
        functools.partial(_diff_attn_kernel, tq=tq, lam_init=lam_init),
        grid=(batch, N_DIFF, nq),
        in_specs=[vec, vec, vec,
                  pl.BlockSpec((tq, LANES), lambda b, h, i: (b * nq + i, QA_BLK + h)),
                  pl.BlockSpec((s, LANES), lambda b, h, i: (b, KA_BLK + h)),
                  pl.BlockSpec((s, LANES), lambda b, h, i: (b, VA_BLK + h))],
        out_specs=pl.BlockSpec((tq, LANES), lambda b, h, i: (b * nq + i, h)),
        out_shape=jax.ShapeDtypeStruct((m, A_W), BF16),
        scratch_shapes=[pltpu.VMEM((tq, 1), F32), pltpu.VMEM((tq, 1), F32), pltpu.VMEM((tq, HEAD_DIM), F32)] * 2,
        compiler_params=_params("parallel", "parallel", "arbitrary"),
        name="diff_attention",
    )(lam_q, lam_k, g_sub, proj, proj, proj)


def _sb_attn_kernel(q_ref, k_ref, v_ref, o_ref, r_ref, acc_ref, *, tq):
    i = pl.program_id(2)
    q = q_ref[...]
    r_ref[...] = jnp.zeros(r_ref.shape, F32)
    acc_ref[...] = jnp.zeros(acc_ref.shape, F32)
    jj = lax.broadcasted_iota(jnp.int32, (2 * tq, tq), 0) % tq
    ss = lax.broadcasted_iota(jnp.int32, (2 * tq, tq), 1)
    tri = (jj > ss).astype(BF16)

    def step(kb, masked):
        k = k_ref[pl.ds(pl.multiple_of(kb * tq, tq), tq), :]
        v = v_ref[pl.ds(pl.multiple_of(kb * tq, tq), tq), :]
        z = _dot_nt(q, k)
        sp = _log_sigmoid_parts(z)
        log_remain = -(jnp.maximum(z, 0.0) + sp)
        log_beta = jnp.minimum(z, 0.0) - sp
        if masked:
            row, col = _causal_masks(tq)
            strict = row > col
            log_remain = jnp.where(strict, log_remain, 0.0)
        hi = log_remain.astype(BF16)
        lo = (log_remain - hi.astype(F32)).astype(BF16)
        after = _dot(jnp.concatenate([hi, lo], axis=1), tri)
        w = jnp.exp(log_beta + (after + r_ref[...]))
        if masked:
            w = jnp.where(strict, w, 0.0)
        acc_ref[...] += _dot(w.astype(BF16), v)
        r_ref[...] += jnp.sum(log_remain, axis=1, keepdims=True)

    step(i, True)

    def body(t, carry):
        step(i - 1 - t, False)
        return carry

    lax.fori_loop(0, i, body, 0)
    o_ref[...] = acc_ref[...].astype(BF16)


def stick_breaking_attention(proj, *, batch, tq):
    m = proj.shape[0]
    s = m // batch
    nq = s // tq
    return pl.pallas_call(
        functools.partial(_sb_attn_kernel, tq=tq),
        grid=(batch, N_SB, nq),
        in_specs=[pl.BlockSpec((tq, LANES), lambda b, h, i: (b * nq + i, QB_BLK + h)),
                  pl.BlockSpec((s, LANES), lambda b, h, i: (b, KB_BLK + h)),
                  pl.BlockSpec((s, LANES), lambda b, h, i: (b, VB_BLK + h))],
        out_specs=pl.BlockSpec((tq, LANES), lambda b, h, i: (b * nq + i, h)),
        out_shape=jax.ShapeDtypeStruct((m, SB_W), BF16),
        scratch_shapes=[pltpu.VMEM((tq, 1), F32), pltpu.VMEM((tq, HEAD_DIM), F32)],
        compiler_params=_params("parallel", "parallel", "arbitrary"),
        name="stick_breaking_attention",
    )(proj, proj, proj)


def _fox_attn_kernel(cum_ref, q_ref, k_ref, v_ref, o_ref, m_ref, l_ref, acc_ref, *, tq):
    h = pl.program_id(1)
    i = pl.program_id(2)
    q = q_ref[...]
    m_ref[...] = jnp.full(m_ref.shape, NEG_INF, F32)
    l_ref[...] = jnp.zeros(l_ref.shape, F32)
    acc_ref[...] = jnp.zeros(acc_ref.shape, F32)
    row, col = _causal_masks(tq)
    cum_q_row = cum_ref[pl.ds(h, 1), pl.ds(pl.multiple_of(i * tq, tq), tq)]
    cum_q = jnp.sum(jnp.where(row == col, cum_q_row, 0.0), axis=1, keepdims=True)

    def step(kb, masked):
        k = k_ref[pl.ds(pl.multiple_of(kb * tq, tq), tq), :]
        v = v_ref[pl.ds(pl.multiple_of(kb * tq, tq), tq), :]
        cum_k = cum_ref[pl.ds(h, 1), pl.ds(pl.multiple_of(kb * tq, tq), tq)]
        s = _dot_nt(q, k) + (cum_q - cum_k)
        if masked:
            s = jnp.where(row >= col, s, NEG_INF)
        _softmax_step(s, v, m_ref, l_ref, acc_ref)

    def body(kb, carry):
        step(kb, False)
        return carry

    lax.fori_loop(0, i, body, 0)
    step(i, True)
    o_ref[...] = (acc_ref[...] / l_ref[...]).astype(BF16)


def forgetting_attention(proj, cum, *, batch, tq):
    m = proj.shape[0]
    s = m // batch
    nq = s // tq
    return pl.pallas_call(
        functools.partial(_fox_attn_kernel, tq=tq),
        grid=(batch, N_FOX, nq),
        in_specs=[pl.BlockSpec((None, GATE_ROWS, s), lambda b, h, i: (b, 0, 0)),
                  pl.BlockSpec((tq, LANES), lambda b, h, i: (b * nq + i, QC_BLK + h)),
                  pl.BlockSpec((s, LANES), lambda b, h, i: (b, KC_BLK + h)),
                  pl.BlockSpec((s, LANES), lambda b, h, i: (b, VC_BLK + h))],
        out_specs=pl.BlockSpec((tq, LANES), lambda b, h, i: (b * nq + i, h)),
        out_shape=jax.ShapeDtypeStruct((m, FOX_W), BF16),
        scratch_shapes=[pltpu.VMEM((tq, 1), F32), pltpu.VMEM((tq, 1), F32), pltpu.VMEM((tq, HEAD_DIM), F32)],
        compiler_params=_params("parallel", "parallel", "arbitrary"),
        name="forgetting_attention",
    )(cum, proj, proj, proj)


def _residual_tail(y, x, g_post, g_next, x_out_ref, h_out_ref):
    x_new = x + _rms(y, g_post)
    x_out_ref[...] = x_new
    if h_out_ref is not None:
        h_out_ref[...] = _rms(x_new, g_next).astype(BF16)


def _mix_out_kernel(oa_ref, ob_ref, oc_ref, x_ref, w_ref, gb_ref, gc_ref, gp_ref, gn_ref, xo_ref, ho_ref):
    ob = _rms(ob_ref[...].astype(F32), gb_ref[...]).astype(BF16)
    oc = _rms(oc_ref[...].astype(F32), gc_ref[...]).astype(BF16)
    y = (_dot(oa_ref[...], w_ref[0:A_W, :]) + _dot(ob, w_ref[A_W:A_W + SB_W, :])
         + _dot(oc, w_ref[A_W + SB_W:, :]))
    _residual_tail(y, x_ref[...], gp_ref[...], gn_ref[...], xo_ref, ho_ref)


def mix_out(oa, ob, oc, x, w, g_sb, g_fox, g_post, g_next, *, bm):
    m, d = x.shape
    rows = lambda width: pl.BlockSpec((bm, width), lambda i: (i, 0))
    const = lambda width: pl.BlockSpec((1, width), lambda i: (0, 0))
    return pl.pallas_call(
        _mix_out_kernel,
        grid=(m // bm,),
        in_specs=[rows(A_W), rows(SB_W), rows(FOX_W), rows(d),
                  pl.BlockSpec(w.shape, lambda i: (0, 0)),
                  const(SB_W), const(FOX_W), const(d), const(d)],
        out_specs=[rows(d), rows(d)],
        out_shape=[jax.ShapeDtypeStruct((m, d), F32), jax.ShapeDtypeStruct((m, d), BF16)],
        compiler_params=_params("parallel"),
        name="mix_out",
    )(oa, ob, oc, x, w, g_sb.reshape(1, -1), g_fox.reshape(1, -1), g_post.reshape(1, -1), g_next.reshape(1, -1))


def _norm_matmul_kernel(x_ref, g_ref, w_ref, o_ref):
    o_ref[...] = _dot(_rms(x_ref[...], g_ref[...]).astype(BF16), w_ref[...]).astype(BF16)


def norm_matmul(x, g, w, *, bm):
    m, d = x.shape
    n = w.shape[1]
    return pl.pallas_call(
        _norm_matmul_kernel,
        grid=(m // bm,),
        in_specs=[pl.BlockSpec((bm, d), lambda i: (i, 0)), pl.BlockSpec((1, d), lambda i: (0, 0)),
                  pl.BlockSpec((d, n), lambda i: (0, 0))],
        out_specs=pl.BlockSpec((bm, n), lambda i: (i, 0)),
        out_shape=jax.ShapeDtypeStruct((m, n), BF16),
        compiler_params=_params("parallel"),
        name="memory_kv",
    )(x, g.reshape(1, d), w)


def _cross_attn_kernel(h_ref, x_ref, kv_ref, wq_ref, wo_ref, gp_ref, gn_ref, xo_ref, ho_ref):
    q = (_dot(h_ref[...], wq_ref[...]) * HEAD_DIM ** -0.5).astype(BF16)
    heads = []
    for hd in range(N_CROSS_HEADS):
        k = kv_ref[:, hd * HEAD_DIM:(hd + 1) * HEAD_DIM]
        v = kv_ref[:, CROSS_W + hd * HEAD_DIM:CROSS_W + (hd + 1) * HEAD_DIM]
        s = _dot_nt(q[:, hd * HEAD_DIM:(hd + 1) * HEAD_DIM], k)
        p = jnp.exp(s - jnp.max(s, axis=1, keepdims=True))
        o = _dot(p.astype(BF16), v) / jnp.sum(p, axis=1, keepdims=True)
        heads.append(o.astype(BF16))
    y = _dot(jnp.concatenate(heads, axis=1), wo_ref[...])
    _residual_tail(y, x_ref[...], gp_ref[...], gn_ref[...], xo_ref, ho_ref)


def cross_attention(h, x, kv, wq, wo, g_post, g_next, *, batch, bm):
    m, d = x.shape
    s = m // batch
    mem_len = kv.shape[0] // batch
    nb = s // bm
    rows = lambda width: pl.BlockSpec((bm, width), lambda b, i: (b * nb + i, 0))
    const = lambda shape: pl.BlockSpec(shape, lambda b, i: (0, 0))
    return pl.pallas_call(
        _cross_attn_kernel,
        grid=(batch, nb),
        in_specs=[rows(d), rows(d),
                  pl.BlockSpec((mem_len, 2 * CROSS_W), lambda b, i: (b, 0)),
                  const(wq.shape), const(wo.shape), const((1, d)), const((1, d))],
        out_specs=[rows(d), rows(d)],
        out_shape=[jax.ShapeDtypeStruct((m, d), F32), jax.ShapeDtypeStruct((m, d), BF16)],
        compiler_params=_params("parallel", "parallel"),
        name="cross_attention",
    )(h, x, kv, wq, wo, g_post.reshape(1, d), g_next.reshape(1, d))


def _ffn_up_kernel(h_ref, wg_ref, wu_ref, o_ref):
    h = h_ref[...]
    gate = _dot(h, wg_ref[...])
    up = _dot(h, wu_ref[...])
    o_ref[...] = (gate * jax.nn.sigmoid(gate) * up).astype(BF16)


def ffn_up(h, wg, wu, *, bm, bn):
    m, d = h.shape
    n = wg.shape[1]
    wspec = pl.BlockSpec((d, bn), lambda i, j: (0, j))
    return pl.pallas_call(
        _ffn_up_kernel,
        grid=(m // bm, n // bn),
        in_specs=[pl.BlockSpec((bm, d), lambda i, j: (i, 0)), wspec, wspec],
        out_specs=pl.BlockSpec((bm, bn), lambda i, j: (i, j)),
        out_shape=jax.ShapeDtypeStruct((m, n), BF16),
        compiler_params=_params("parallel", "arbitrary"),
        name="ffn_up",
    )(h, wg, wu)


def _ffn_down_kernel(a_ref, w_ref, x_ref, gp_ref, gn_ref, xo_ref, *rest, emit_h):
    ho_ref, acc_ref = rest if emit_h else (None, rest[0])
    kk = pl.program_id(1)

    @pl.when(kk == 0)
    def _():
        acc_ref[...] = jnp.zeros(acc_ref.shape, F32)

    acc_ref[...] += _dot(a_ref[...], w_ref[...])

    @pl.when(kk == pl.num_programs(1) - 1)
    def _():
        _residual_tail(acc_ref[...], x_ref[...], gp_ref[...], gn_ref[...], xo_ref, ho_ref)


def ffn_down(a, w, x, g_post, g_next, *, bm, bk, emit_h):
    m, d = x.shape
    kdim = a.shape[1]
    rows = pl.BlockSpec((bm, d), lambda i, k: (i, 0))
    const = pl.BlockSpec((1, d), lambda i, k: (0, 0))
    out_shape = [jax.ShapeDtypeStruct((m, d), F32)] + ([jax.ShapeDtypeStruct((m, d), BF16)] if emit_h else [])
    return pl.pallas_call(
        functools.partial(_ffn_down_kernel, emit_h=emit_h),
        grid=(m // bm, kdim // bk),
        in_specs=[pl.BlockSpec((bm, bk), lambda i, k: (i, k)),
                  pl.BlockSpec((bk, d), lambda i, k: (k, 0)),
                  rows, const, const],
        out_specs=[rows] * len(out_shape),
        out_shape=out_shape,
        scratch_shapes=[pltpu.VMEM((bm, d), F32)],
        compiler_params=_params("parallel", "arbitrary"),
        name="ffn_down",
    )(a, w, x, g_post.reshape(1, d), g_next.reshape(1, d))


def _pick(n, *candidates):
    for c in candidates:
        if n % c == 0:
            return c
    return n


def kernel(x, mem, positions, g_mix_pre, g_mix_post, w_in, b_f, lam_q1, lam_k1, lam_q2, lam_k2, g_diff_sub, g_sb_out, g_fox_out, w_out, g_x_pre, g_x_post, g_mem, w_cq, w_ckv, w_co, g_ffn_pre, g_ffn_post, w_gate, w_up, w_down):
    batch, seq, d = x.shape
    depth = w_in.shape[0]
    m = batch * seq
    d_ff = w_gate.shape[2]
    assert w_in.shape[2] == QKV_COLS + N_FOX and d == A_W + SB_W + FOX_W

    bm_row = _pick(m, 512, 256, 128)
    bm_mm = _pick(m, 1024, 512, 256, 128)
    bn_in = 2 * A_W
    bn_ff = _pick(d_ff, 512, 256, 128)
    tq = _pick(seq, 256, 128)
    bm_x = _pick(seq, 512, 256, 128)
    bm_mem = _pick(mem.shape[0] * mem.shape[1], 512, 256, 128)

    w_qkv = w_in[:, :, :QKV_COLS].astype(BF16)
    wf_t = jnp.pad(jnp.swapaxes(w_in[:, :, QKV_COLS:], 1, 2), ((0, 0), (0, GATE_ROWS - N_FOX), (0, 0))).astype(BF16)
    bf = jnp.pad(b_f, ((0, 0), (0, GATE_ROWS - N_FOX))).reshape(depth, GATE_ROWS, 1)
    w_out_b, w_cq_b, w_ckv_b, w_co_b = (w.astype(BF16) for w in (w_out, w_cq, w_ckv, w_co))
    w_gate_b, w_up_b, w_down_b = (w.astype(BF16) for w in (w_gate, w_up, w_down))
    lam_q = jnp.concatenate([lam_q1, lam_q2], axis=1).reshape(depth, 1, LANES)
    lam_k = jnp.concatenate([lam_k1, lam_k2], axis=1).reshape(depth, 1, LANES)

    colscale = jnp.ones((QKV_COLS,), F32)
    colscale = colscale.at[QA_BLK * LANES:KA_BLK * LANES].set(DIFF_QK_DIM ** -0.5)
    colscale = colscale.at[QB_BLK * LANES:KB_BLK * LANES].set(HEAD_DIM ** -0.5)
    colscale = colscale.at[QC_BLK * LANES:KC_BLK * LANES].set(HEAD_DIM ** -0.5).reshape(1, QKV_COLS)

    x = x.reshape(m, d)
    mem2 = mem.reshape(-1, d)
    cos_t, sin_t = rope_tables(positions, bm=bm_row)
    h = prenorm(x, g_mix_pre[0], bm=bm_row)

    for l in range(depth):
        lam_init = 0.8 - 0.6 * math.exp(-0.3 * l)
        proj = in_proj(h, w_qkv[l], colscale, cos_t, sin_t, bm=bm_mm, bn=bn_in)
        cum = forget_cumsum(h, wf_t[l], bf[l], batch=batch)
        oa = diff_attention(proj, lam_q[l], lam_k[l], g_diff_sub[l].reshape(1, LANES), lam_init, batch=batch, tq=tq)
        ob = stick_breaking_attention(proj, batch=batch, tq=tq)
        oc = forgetting_attention(proj, cum, batch=batch, tq=tq)
        x, h = mix_out(oa, ob, oc, x, w_out_b[l], g_sb_out[l], g_fox_out[l], g_mix_post[l], g_x_pre[l], bm=bm_row // 2)
        kv = norm_matmul(mem2, g_mem[l], w_ckv_b[l], bm=bm_mem)
        x, h = cross_attention(h, x, kv, w_cq_b[l], w_co_b[l], g_x_post[l], g_ffn_pre[l], batch=batch, bm=bm_x)
        a = ffn_up(h, w_gate_b[l], w_up_b[l], bm=bm_mm, bn=bn_ff)
        last = l == depth - 1
        g_next = g_ffn_post[l] if last else g_mix_pre[l + 1]
        outs = ffn_down(a, w_down_b[l], x, g_ffn_post[l], g_next, bm=bm_row, bk=bn_ff, emit_h=not last)
        x = outs[0]
        h = None if last else outs[1]
    return x.reshape(batch, seq, d)
```

```python
import functools
import math

import jax
import jax.numpy as jnp
from jax import lax
from jax.experimental import pallas as pl
from jax.experimental.pallas import tpu as pltpu

F32 = jnp.float32
BF16 = jnp.bfloat16

EPS = 1e-6
NEG_INF = -1e30
ROPE_THETA = 10000.0

LANES = 128
HEAD_DIM = 128
N_DIFF, N_SB, N_FOX = 4, 6, 6
DIFF_QK_DIM = HEAD_DIM // 2
A_W, SB_W, FOX_W = N_DIFF * HEAD_DIM, N_SB * HEAD_DIM, N_FOX * HEAD_DIM
QKV_COLS = 3 * (A_W + SB_W + FOX_W)
N_CROSS_HEADS = 4
CROSS_W = N_CROSS_HEADS * HEAD_DIM
GATE_ROWS = 8

VMEM_LIMIT = 56 * 1024 * 1024

QA_BLK, KA_BLK, VA_BLK = 0, A_W // LANES, 2 * A_W // LANES
QB_BLK = 3 * A_W // LANES
KB_BLK, VB_BLK = QB_BLK + N_SB, QB_BLK + 2 * N_SB
QC_BLK = QB_BLK + 3 * N_SB
KC_BLK, VC_BLK = QC_BLK + N_FOX, QC_BLK + 2 * N_FOX


def _params(*semantics):
    return pltpu.CompilerParams(dimension_semantics=semantics, vmem_limit_bytes=VMEM_LIMIT)


def _rms(xf, g):
    return xf * lax.rsqrt(jnp.mean(xf * xf, axis=-1, keepdims=True) + EPS) * g


def _dot(a, b):
    return jnp.dot(a, b, preferred_element_type=F32)


def _dot_nt(a, b):
    return lax.dot_general(a, b, (((1,), (1,)), ((), ())), preferred_element_type=F32)


def _log_sigmoid_parts(z):
    return jnp.log1p(jnp.exp(-jnp.abs(z)))


def _prenorm_kernel(x_ref, g_ref, h_ref):
    h_ref[...] = _rms(x_ref[...], g_ref[...]).astype(BF16)


def prenorm(x, g, *, bm):
    m, d = x.shape
    return pl.pallas_call(
        _prenorm_kernel,
        grid=(m // bm,),
        in_specs=[pl.BlockSpec((bm, d), lambda i: (i, 0)), pl.BlockSpec((1, d), lambda i: (0, 0))],
        out_specs=pl.BlockSpec((bm, d), lambda i: (i, 0)),
        out_shape=jax.ShapeDtypeStruct((m, d), BF16),
        compiler_params=_params("parallel"),
        name="prenorm",
    )(x, g.reshape(1, d))


def _rope_table_kernel(pos_ref, invf_ref, sign_ref, cos_ref, sin_ref):
    ang = pos_ref[...].astype(F32) * invf_ref[...]
    cos_ref[...] = jnp.cos(ang)
    sin_ref[...] = jnp.sin(ang) * sign_ref[...]


def rope_tables(positions, *, bm):
    m = positions.size
    half = DIFF_QK_DIM // 2
    inv_freq = ROPE_THETA ** (-jnp.arange(half, dtype=F32) / half)
    invf = jnp.tile(inv_freq, LANES // half).reshape(1, LANES)
    sign = jnp.where((jnp.arange(LANES) % DIFF_QK_DIM) < half, -1.0, 1.0).astype(F32).reshape(1, LANES)
    row = pl.BlockSpec((bm, LANES), lambda i: (i, 0))
    const = pl.BlockSpec((1, LANES), lambda i: (0, 0))
    return pl.pallas_call(
        _rope_table_kernel,
        grid=(m // bm,),
        in_specs=[pl.BlockSpec((bm, 1), lambda i: (i, 0)), const, const],
        out_specs=[row, row],
        out_shape=[jax.ShapeDtypeStruct((m, LANES), F32)] * 2,
        compiler_params=_params("parallel"),
        name="rope_tables",
    )(positions.reshape(m, 1), invf, sign)


def _in_proj_kernel(h_ref, w_ref, cs_ref, cos_ref, sin_ref, o_ref, *, n_rope_blocks):
    j = pl.program_id(1)
    acc = _dot(h_ref[...], w_ref[...]) * cs_ref[...]
    bm, bn = acc.shape

    @pl.when(j < n_rope_blocks)
    def _():
        c, s = cos_ref[...], sin_ref[...]
        first_half = (lax.broadcasted_iota(jnp.int32, (bm, LANES), 1) % DIFF_QK_DIM) < DIFF_QK_DIM // 2
        for t in range(bn // LANES):
            a = acc[:, t * LANES:(t + 1) * LANES]
            partner = jnp.where(first_half, pltpu.roll(a, LANES - DIFF_QK_DIM // 2, 1),
                                pltpu.roll(a, DIFF_QK_DIM // 2, 1))
            o_ref[:, t * LANES:(t + 1) * LANES] = (a * c + partner * s).astype(BF16)

    @pl.when(j >= n_rope_blocks)
    def _():
        o_ref[...] = acc.astype(BF16)


def in_proj(h, w, colscale, cos_t, sin_t, *, bm, bn):
    m, d = h.shape
    n = w.shape[1]
    rope_cols = 2 * A_W
    assert rope_cols % bn == 0 and n % bn == 0 and m % bm == 0
    row_tab = pl.BlockSpec((bm, LANES), lambda i, j: (i, 0))
    return pl.pallas_call(
        functools.partial(_in_proj_kernel, n_rope_blocks=rope_cols // bn),
        grid=(m // bm, n // bn),
        in_specs=[pl.BlockSpec((bm, d), lambda i, j: (i, 0)),
                  pl.BlockSpec((d, bn), lambda i, j: (0, j)),
                  pl.BlockSpec((1, bn), lambda i, j: (0, j)),
                  row_tab, row_tab],
        out_specs=pl.BlockSpec((bm, bn), lambda i, j: (i, j)),
        out_shape=jax.ShapeDtypeStruct((m, n), BF16),
        compiler_params=_params("parallel", "arbitrary"),
        name="in_proj",
    )(h, w, colscale, cos_t, sin_t)


def _forget_cumsum_kernel(h_ref, wf_ref, bf_ref, cum_ref):
    s = h_ref.shape[0]
    fc = _dot_nt(wf_ref[...], h_ref[...]) + bf_ref[...]
    acc = jnp.minimum(fc, 0.0) - _log_sigmoid_parts(fc)
    lane = lax.broadcasted_iota(jnp.int32, acc.shape, 1)
    k = 1
    while k < s:
        acc = acc + jnp.where(lane >= k, pltpu.roll(acc, k, 1), 0.0)
        k *= 2
    cum_ref[...] = acc


def forget_cumsum(h, wf_t, bf, *, batch):
    m, d = h.shape
    s = m // batch
    return pl.pallas_call(
        _forget_cumsum_kernel,
        grid=(batch,),
        in_specs=[pl.BlockSpec((s, d), lambda b: (b, 0)),
                  pl.BlockSpec((GATE_ROWS, d), lambda b: (0, 0)),
                  pl.BlockSpec((GATE_ROWS, 1), lambda b: (0, 0))],
        out_specs=pl.BlockSpec((None, GATE_ROWS, s), lambda b: (b, 0, 0)),
        out_shape=jax.ShapeDtypeStruct((batch, GATE_ROWS, s), F32),
        compiler_params=_params("parallel"),
        name="forget_cumsum",
    )(h, wf_t, bf)


def _tile_lanes(a, width):
    return a if width == LANES else jnp.concatenate([a] * (width // LANES), axis=1)


def _head(ref, h):
    return ref[:, h * HEAD_DIM:(h + 1) * HEAD_DIM]


def _key_block(ref, kb, tk, h):
    return ref[pl.ds(pl.multiple_of(kb * tk, tk), tk), h * HEAD_DIM:(h + 1) * HEAD_DIM]


def _softmax_update(s, v, m_ref, l_ref, acc_ref, h):
    m_prev = m_ref[h]
    m_new = jnp.maximum(m_prev, jnp.max(s, axis=1, keepdims=True))
    alpha = jnp.exp(m_prev - m_new)
    p = jnp.exp(s - _tile_lanes(m_new, s.shape[1]))
    l_ref[h] = alpha * l_ref[h] + jnp.sum(p, axis=1, keepdims=True)
    acc_ref[h] = alpha * acc_ref[h] + _dot(p.astype(BF16), v)
    m_ref[h] = m_new


def _init_softmax_stats(m_ref, l_ref, acc_ref):
    m_ref[...] = jnp.full(m_ref.shape, NEG_INF, F32)
    l_ref[...] = jnp.zeros(l_ref.shape, F32)
    acc_ref[...] = jnp.zeros(acc_ref.shape, F32)


def _causal_loop(i, step):
    def body(kb, carry):
        step(kb, False)
        return carry

    lax.fori_loop(0, i, body, 0)
    step(i, True)


def _diff_attn_kernel(lq_ref, lk_ref, g_ref, q_ref, k_ref, v_ref, o_ref, qq_ref, m_ref, l_ref, acc_ref,
                      *, tq, lam_init):
    i = pl.program_id(1)
    low = lax.broadcasted_iota(jnp.int32, (tq, LANES), 1) < DIFF_QK_DIM
    for h in range(N_DIFF):
        q = _head(q_ref, h)
        zero = jnp.zeros_like(q)
        qq_ref[h, 0:tq, :] = jnp.where(low, q, zero)
        qq_ref[h, tq:2 * tq, :] = jnp.where(low, zero, q)
    _init_softmax_stats(m_ref, l_ref, acc_ref)

    def step(kb, masked):
        if masked:
            row = lax.broadcasted_iota(jnp.int32, (2 * tq, tq), 0)
            col = lax.broadcasted_iota(jnp.int32, (2 * tq, tq), 1)
            keep = jnp.where(row >= tq, row - tq, row) >= col
        for h in range(N_DIFF):
            s = _dot_nt(qq_ref[h], _key_block(k_ref, kb, tq, h))
            if masked:
                s = jnp.where(keep, s, NEG_INF)
            _softmax_update(s, _key_block(v_ref, kb, tq, h), m_ref, l_ref, acc_ref, h)

    _causal_loop(i, step)

    prod = lq_ref[...] * lk_ref[...]
    first = lax.broadcasted_iota(jnp.int32, prod.shape, 1) < DIFF_QK_DIM
    e1 = jnp.exp(jnp.sum(jnp.where(first, prod, 0.0), axis=1, keepdims=True))
    e2 = jnp.exp(jnp.sum(jnp.where(first, 0.0, prod), axis=1, keepdims=True))
    lam = (e1 - e2) + lam_init
    for h in range(N_DIFF):
        o = (acc_ref[h, 0:tq, :] / l_ref[h, 0:tq, :]
             - lam * (acc_ref[h, tq:2 * tq, :] / l_ref[h, tq:2 * tq, :]))
        o_ref[:, h * HEAD_DIM:(h + 1) * HEAD_DIM] = (_rms(o, g_ref[...]) * (1.0 - lam_init)).astype(BF16)


def _group_specs(s, tq, width, first_col_block):
    nq = s // tq
    q_spec = pl.BlockSpec((tq, width), lambda b, i: (b * nq + i, first_col_block))
    k_spec = pl.BlockSpec((s, width), lambda b, i: (b, first_col_block + 1))
    v_spec = pl.BlockSpec((s, width), lambda b, i: (b, first_col_block + 2))
    o_spec = pl.BlockSpec((tq, width), lambda b, i: (b * nq + i, 0))
    return q_spec, k_spec, v_spec, o_spec


def diff_attention(proj, lam_q, lam_k, g_sub, lam_init, *, batch, tq):
    m = proj.shape[0]
    s = m // batch
    vec = pl.BlockSpec((1, LANES), lambda b, i: (0, 0))
    q_spec, k_spec, v_spec, o_spec = _group_specs(s, tq, A_W, QA_BLK * LANES // A_W)
    stat = pltpu.VMEM((N_DIFF, 2 * tq, HEAD_DIM), F32)
    return pl.pallas_call(
        functools.partial(_diff_attn_kernel, tq=tq, lam_init=lam_init),
        grid=(batch, s // tq),
        in_specs=[vec, vec, vec, q_spec, k_spec, v_spec],
        out_specs=o_spec,
        out_shape=jax.ShapeDtypeStruct((m, A_W), BF16),
        scratch_shapes=[pltpu.VMEM((N_DIFF, 2 * tq, HEAD_DIM), BF16), stat, stat, stat],
        compiler_params=_params("parallel", "arbitrary"),
        name="diff_attention",
    )(lam_q, lam_k, g_sub, proj, proj, proj)


def _sb_attn_kernel(q_ref, k_ref, v_ref, o_ref, qn_ref, r_ref, acc_ref, *, tq):
    i = pl.program_id(1)
    qn_ref[...] = -q_ref[...]
    r_ref[...] = jnp.zeros(r_ref.shape, F32)
    acc_ref[...] = jnp.zeros(acc_ref.shape, F32)
    jj = lax.broadcasted_iota(jnp.int32, (2 * tq, tq), 0)
    ss = lax.broadcasted_iota(jnp.int32, (2 * tq, tq), 1)
    tri = (jnp.where(jj >= tq, jj - tq, jj) > ss).astype(BF16)

    def step(kb, masked):
        if masked:
            strict = lax.broadcasted_iota(jnp.int32, (tq, tq), 0) > lax.broadcasted_iota(jnp.int32, (tq, tq), 1)
        for h in range(N_SB):
            zn = _dot_nt(_head(qn_ref, h), _key_block(k_ref, kb, tq, h))
            sp = jnp.log(1.0 + jnp.exp(-jnp.abs(zn)))
            log_remain = jnp.minimum(zn, 0.0) - sp
            log_beta = log_remain - zn
            if masked:
                log_remain = jnp.where(strict, log_remain, 0.0)
            hi = log_remain.astype(BF16)
            lo = (log_remain - hi.astype(F32)).astype(BF16)
            after = _dot(jnp.concatenate([hi, lo], axis=1), tri)
            w = jnp.exp(log_beta + (after + _tile_lanes(r_ref[h], tq)))
            if masked:
                w = jnp.where(strict, w, 0.0)
            acc_ref[h] += _dot(w.astype(BF16), _key_block(v_ref, kb, tq, h))
            r_ref[h] += jnp.sum(log_remain, axis=1, keepdims=True)

    step(i, True)

    def body(t, carry):
        step(i - 1 - t, False)
        return carry

    lax.fori_loop(0, i, body, 0)
    for h in range(N_SB):
        o_ref[:, h * HEAD_DIM:(h + 1) * HEAD_DIM] = acc_ref[h].astype(BF16)


def stick_breaking_attention(proj, *, batch, tq):
    m = proj.shape[0]
    s = m // batch
    q_spec, k_spec, v_spec, o_spec = _group_specs(s, tq, SB_W, QB_BLK * LANES // SB_W)
    stat = pltpu.VMEM((N_SB, tq, HEAD_DIM), F32)
    return pl.pallas_call(
        functools.partial(_sb_attn_kernel, tq=tq),
        grid=(batch, s // tq),
        in_specs=[q_spec, k_spec, v_spec],
        out_specs=o_spec,
        out_shape=jax.ShapeDtypeStruct((m, SB_W), BF16),
        scratch_shapes=[pltpu.VMEM((tq, SB_W), BF16), stat, stat],
        compiler_params=_params("parallel", "arbitrary"),
        name="stick_breaking_attention",
    )(proj, proj, proj)


def _fox_attn_kernel(cum_ref, q_ref, k_ref, v_ref, o_ref, cq_ref, m_ref, l_ref, acc_ref, *, tq):
    i = pl.program_id(1)
    row = lax.broadcasted_iota(jnp.int32, (tq, tq), 0)
    col = lax.broadcasted_iota(jnp.int32, (tq, tq), 1)
    _init_softmax_stats(m_ref, l_ref, acc_ref)
    for h in range(N_FOX):
        cum_q_row = cum_ref[h:h + 1, pl.ds(pl.multiple_of(i * tq, tq), tq)]
        cum_q = jnp.sum(jnp.where(row == col, cum_q_row, 0.0), axis=1, keepdims=True)
        cq_ref[h] = jnp.broadcast_to(cum_q, (tq, HEAD_DIM))

    def step(kb, masked):
        for h in range(N_FOX):
            cum_k = cum_ref[h:h + 1, pl.ds(pl.multiple_of(kb * tq, tq), tq)]
            s = _dot_nt(_head(q_ref, h), _key_block(k_ref, kb, tq, h)) + (_tile_lanes(cq_ref[h], tq) - cum_k)
            if masked:
                s = jnp.where(row >= col, s, NEG_INF)
            _softmax_update(s, _key_block(v_ref, kb, tq, h), m_ref, l_ref, acc_ref, h)

    _causal_loop(i, step)
    for h in range(N_FOX):
        o_ref[:, h * HEAD_DIM:(h + 1) * HEAD_DIM] = (acc_ref[h] / l_ref[h]).astype(BF16)


def forgetting_attention(proj, cum, *, batch, tq):
    m = proj.shape[0]
    s = m // batch
    q_spec, k_spec, v_spec, o_spec = _group_specs(s, tq, FOX_W, QC_BLK * LANES // FOX_W)
    stat = pltpu.VMEM((N_FOX, tq, HEAD_DIM), F32)
    return pl.pallas_call(
        functools.partial(_fox_attn_kernel, tq=tq),
        grid=(batch, s // tq),
        in_specs=[pl.BlockSpec((None, GATE_ROWS, s), lambda b, i: (b, 0, 0)), q_spec, k_spec, v_spec],
        out_specs=o_spec,
        out_shape=jax.ShapeDtypeStruct((m, FOX_W), BF16),
        scratch_shapes=[stat, stat, stat, stat],
        compiler_params=_params("parallel", "arbitrary"),
        name="forgetting_attention",
    )(cum, proj, proj, proj)


def _residual_tail(y, x, g_post, g_next, x_out_ref, h_out_ref):
    x_new = x + _rms(y, g_post)
    x_out_ref[...] = x_new
    if h_out_ref is not None:
        h_out_ref[...] = _rms(x_new, g_next).astype(BF16)


def _mix_out_kernel(oa_ref, ob_ref, oc_ref, x_ref, w_ref, gb_ref, gc_ref, gp_ref, gn_ref, xo_ref, ho_ref):
    ob = _rms(ob_ref[...].astype(F32), gb_ref[...]).astype(BF16)
    oc = _rms(oc_ref[...].astype(F32), gc_ref[...]).astype(BF16)
    y = (_dot(oa_ref[...], w_ref[0:A_W, :]) + _dot(ob, w_ref[A_W:A_W + SB_W, :])
         + _dot(oc, w_ref[A_W + SB_W:, :]))
    _residual_tail(y, x_ref[...], gp_ref[...], gn_ref[...], xo_ref, ho_ref)


def mix_out(oa, ob, oc, x, w, g_sb, g_fox, g_post, g_next, *, bm):
    m, d = x.shape
    rows = lambda width: pl.BlockSpec((bm, width), lambda i: (i, 0))
    const = lambda width: pl.BlockSpec((1, width), lambda i: (0, 0))
    return pl.pallas_call(
        _mix_out_kernel,
        grid=(m // bm,),
        in_specs=[rows(A_W), rows(SB_W), rows(FOX_W), rows(d),
                  pl.BlockSpec(w.shape, lambda i: (0, 0)),
                  const(SB_W), const(FOX_W), const(d), const(d)],
        out_specs=[rows(d), rows(d)],
        out_shape=[jax.ShapeDtypeStruct((m, d), F32), jax.ShapeDtypeStruct((m, d), BF16)],
        compiler_params=_params("parallel"),
        name="mix_out",
    )(oa, ob, oc, x, w, g_sb.reshape(1, -1), g_fox.reshape(1, -1), g_post.reshape(1, -1), g_next.reshape(1, -1))


def _norm_matmul_kernel(x_ref, g_ref, w_ref, o_ref):
    o_ref[...] = _dot(_rms(x_ref[...], g_ref[...]).astype(BF16), w_ref[...]).astype(BF16)


def norm_matmul(x, g, w, *, bm):
    m, d = x.shape
    n = w.shape[1]
    return pl.pallas_call(
        _norm_matmul_kernel,
        grid=(m // bm,),
        in_specs=[pl.BlockSpec((bm, d), lambda i: (i, 0)), pl.BlockSpec((1, d), lambda i: (0, 0)),
                  pl.BlockSpec((d, n), lambda i: (0, 0))],
        out_specs=pl.BlockSpec((bm, n), lambda i: (i, 0)),
        out_shape=jax.ShapeDtypeStruct((m, n), BF16),
        compiler_params=_params("parallel"),
        name="memory_kv",
    )(x, g.reshape(1, d), w)


def _cross_attn_kernel(h_ref, x_ref, kv_ref, wq_ref, wo_ref, gp_ref, gn_ref, xo_ref, ho_ref):
    q = (_dot(h_ref[...], wq_ref[...]) * HEAD_DIM ** -0.5).astype(BF16)
    heads = []
    for hd in range(N_CROSS_HEADS):
        k = kv_ref[:, hd * HEAD_DIM:(hd + 1) * HEAD_DIM]
        v = kv_ref[:, CROSS_W + hd * HEAD_DIM:CROSS_W + (hd + 1) * HEAD_DIM]
        s = _dot_nt(q[:, hd * HEAD_DIM:(hd + 1) * HEAD_DIM], k)
        p = jnp.exp(s - jnp.max(s, axis=1, keepdims=True))
        o = _dot(p.astype(BF16), v) / jnp.sum(p, axis=1, keepdims=True)
        heads.append(o.astype(BF16))
    y = _dot(jnp.concatenate(heads, axis=1), wo_ref[...])
    _residual_tail(y, x_ref[...], gp_ref[...], gn_ref[...], xo_ref, ho_ref)


def cross_attention(h, x, kv, wq, wo, g_post, g_next, *, batch, bm):
    m, d = x.shape
    s = m // batch
    mem_len = kv.shape[0] // batch
    nb = s // bm
    rows = lambda width: pl.BlockSpec((bm, width), lambda b, i: (b * nb + i, 0))
    const = lambda shape: pl.BlockSpec(shape, lambda b, i: (0, 0))
    return pl.pallas_call(
        _cross_attn_kernel,
        grid=(batch, nb),
        in_specs=[rows(d), rows(d),
                  pl.BlockSpec((mem_len, 2 * CROSS_W), lambda b, i: (b, 0)),
                  const(wq.shape), const(wo.shape), const((1, d)), const((1, d))],
        out_specs=[rows(d), rows(d)],
        out_shape=[jax.ShapeDtypeStruct((m, d), F32), jax.ShapeDtypeStruct((m, d), BF16)],
        compiler_params=_params("parallel", "parallel"),
        name="cross_attention",
    )(h, x, kv, wq, wo, g_post.reshape(1, d), g_next.reshape(1, d))


def _ffn_up_kernel(h_ref, wg_ref, wu_ref, o_ref):
    h = h_ref[...]
    gate = _dot(h, wg_ref[...])
    up = _dot(h, wu_ref[...])
    o_ref[...] = (gate * jax.nn.sigmoid(gate) * up).astype(BF16)


def ffn_up(h, wg, wu, *, bm, bn):
    m, d = h.shape
    n = wg.shape[1]
    wspec = pl.BlockSpec((d, bn), lambda i, j: (0, j))
    return pl.pallas_call(
        _ffn_up_kernel,
        grid=(m // bm, n // bn),
        in_specs=[pl.BlockSpec((bm, d), lambda i, j: (i, 0)), wspec, wspec],
        out_specs=pl.BlockSpec((bm, bn), lambda i, j: (i, j)),
        out_shape=jax.ShapeDtypeStruct((m, n), BF16),
        compiler_params=_params("parallel", "arbitrary"),
        name="ffn_up",
    )(h, wg, wu)


def _ffn_down_kernel(*refs, emit_h):
    if emit_h:
        a_ref, w_ref, x_ref, gp_ref, gn_ref, xo_ref, ho_ref, acc_ref = refs
    else:
        a_ref, w_ref, x_ref, gp_ref, xo_ref, acc_ref = refs
        gn_ref = ho_ref = None
    kk = pl.program_id(1)

    @pl.when(kk == 0)
    def _():
        acc_ref[...] = jnp.zeros(acc_ref.shape, F32)

    acc_ref[...] += _dot(a_ref[...], w_ref[...])

    @pl.when(kk == pl.num_programs(1) - 1)
    def _():
        g_next = gn_ref[...] if emit_h else None
        _residual_tail(acc_ref[...], x_ref[...], gp_ref[...], g_next, xo_ref, ho_ref)


def ffn_down(a, w, x, g_post, g_next, *, bm, bk):
    m, d = x.shape
    kdim = a.shape[1]
    emit_h = g_next is not None
    rows = pl.BlockSpec((bm, d), lambda i, k: (i, 0))
    const = pl.BlockSpec((1, d), lambda i, k: (0, 0))
    gains = [g_post.reshape(1, d)] + ([g_next.reshape(1, d)] if emit_h else [])
    out_shape = [jax.ShapeDtypeStruct((m, d), F32)] + ([jax.ShapeDtypeStruct((m, d), BF16)] if emit_h else [])
    return pl.pallas_call(
        functools.partial(_ffn_down_kernel, emit_h=emit_h),
        grid=(m // bm, kdim // bk),
        in_specs=[pl.BlockSpec((bm, bk), lambda i, k: (i, k)),
                  pl.BlockSpec((bk, d), lambda i, k: (k, 0)),
                  rows] + [const] * len(gains),
        out_specs=[rows] * len(out_shape),
        out_shape=out_shape,
        scratch_shapes=[pltpu.VMEM((bm, d), F32)],
        compiler_params=_params("parallel", "arbitrary"),
        name="ffn_down",
    )(a, w, x, *gains)


def _pick(n, *candidates):
    for c in candidates:
        if n % c == 0:
            return c
    return n


def kernel(x, mem, positions, g_mix_pre, g_mix_post, w_in, b_f, lam_q1, lam_k1, lam_q2, lam_k2, g_diff_sub, g_sb_out, g_fox_out, w_out, g_x_pre, g_x_post, g_mem, w_cq, w_ckv, w_co, g_ffn_pre, g_ffn_post, w_gate, w_up, w_down):
    batch, seq, d = x.shape
    depth = w_in.shape[0]
    m = batch * seq
    d_ff = w_gate.shape[2]
    assert w_in.shape[2] == QKV_COLS + N_FOX and d == A_W + SB_W + FOX_W

    bm_row = _pick(m, 512, 256, 128)
    bm_mm = _pick(m, 1024, 512, 256, 128)
    bn_in = 2 * A_W
    bn_ff = _pick(d_ff, 512, 256, 128)
    bk_ff = _pick(d_ff, 1408, 1024, 512, 256, 128)
    tq = _pick(seq, 256, 128)
    bm_x = _pick(seq, 512, 256, 128)
    bm_mem = _pick(mem.shape[0] * mem.shape[1], 512, 256, 128)

    w_qkv = w_in[:, :, :QKV_COLS].astype(BF16)
    wf_t = jnp.pad(jnp.swapaxes(w_in[:, :, QKV_COLS:], 1, 2), ((0, 0), (0, GATE_ROWS - N_FOX), (0, 0))).astype(BF16)
    bf = jnp.pad(b_f, ((0, 0), (0, GATE_ROWS - N_FOX))).reshape(depth, GATE_ROWS, 1)
    w_out_b, w_cq_b, w_ckv_b, w_co_b = (w.astype(BF16) for w in (w_out, w_cq, w_ckv, w_co))
    w_gate_b, w_up_b, w_down_b = (w.astype(BF16) for w in (w_gate, w_up, w_down))
    lam_q = jnp.concatenate([lam_q1, lam_q2], axis=1).reshape(depth, 1, LANES)
    lam_k = jnp.concatenate([lam_k1, lam_k2], axis=1).reshape(depth, 1, LANES)

    colscale = jnp.ones((QKV_COLS,), F32)
    colscale = colscale.at[QA_BLK * LANES:KA_BLK * LANES].set(DIFF_QK_DIM ** -0.5)
    colscale = colscale.at[QB_BLK * LANES:KB_BLK * LANES].set(HEAD_DIM ** -0.5)
    colscale = colscale.at[QC_BLK * LANES:KC_BLK * LANES].set(HEAD_DIM ** -0.5).reshape(1, QKV_COLS)

    x = x.reshape(m, d)
    mem2 = mem.reshape(-1, d)
    cos_t, sin_t = rope_tables(positions, bm=bm_row)
    h = prenorm(x, g_mix_pre[0], bm=bm_row)

    for l in range(depth):
        lam_init = 0.8 - 0.6 * math.exp(-0.3 * l)
        proj = in_proj(h, w_qkv[l], colscale, cos_t, sin_t, bm=bm_mm, bn=bn_in)
        cum = forget_cumsum(h, wf_t[l], bf[l], batch=batch)
        oa = diff_attention(proj, lam_q[l], lam_k[l], g_diff_sub[l].reshape(1, LANES), lam_init, batch=batch, tq=tq)
        ob = stick_breaking_attention(proj, batch=batch, tq=tq)
        oc = forgetting_attention(proj, cum, batch=batch, tq=tq)
        x, h = mix_out(oa, ob, oc, x, w_out_b[l], g_sb_out[l], g_fox_out[l], g_mix_post[l], g_x_pre[l], bm=bm_row // 2)
        kv = norm_matmul(mem2, g_mem[l], w_ckv_b[l], bm=bm_mem)
        x, h = cross_attention(h, x, kv, w_cq_b[l], w_co_b[l], g_x_post[l], g_ffn_pre[l], batch=batch, bm=bm_x)
        a = ffn_up(h, w_gate_b[l], w_up_b[l], bm=bm_mm, bn=bn_ff)
        g_next = g_mix_pre[l + 1] if l + 1 < depth else None
        x, *rest = ffn_down(a, w_down_b[l], x, g_ffn_post[l], g_next, bm=bm_row, bk=bk_ff)
        h = rest[0] if rest else None
    return x.reshape(batch, seq, d)
```

```python
import functools
import math

import jax
import jax.numpy as jnp
from jax import lax
from jax.experimental import pallas as pl
from jax.experimental.pallas import tpu as pltpu

F32 = jnp.float32
BF16 = jnp.bfloat16

EPS = 1e-6
NEG_INF = -1e30
ROPE_THETA = 10000.0

LANES = 128
HEAD_DIM = 128
N_DIFF, N_SB, N_FOX = 4, 6, 6
DIFF_QK_DIM = HEAD_DIM // 2
A_W, SB_W, FOX_W = N_DIFF * HEAD_DIM, N_SB * HEAD_DIM, N_FOX * HEAD_DIM
QKV_COLS = 3 * (A_W + SB_W + FOX_W)
N_CROSS_HEADS = 4
CROSS_W = N_CROSS_HEADS * HEAD_DIM
GATE_ROWS = 8

VMEM_LIMIT = 56 * 1024 * 1024

QA_BLK, KA_BLK, VA_BLK = 0, A_W // LANES, 2 * A_W // LANES
QB_BLK = 3 * A_W // LANES
KB_BLK, VB_BLK = QB_BLK + N_SB, QB_BLK + 2 * N_SB
QC_BLK = QB_BLK + 3 * N_SB
KC_BLK, VC_BLK = QC_BLK + N_FOX, QC_BLK + 2 * N_FOX


def _params(*semantics):
    return pltpu.CompilerParams(dimension_semantics=semantics, vmem_limit_bytes=VMEM_LIMIT)


def _rms(xf, g):
    return xf * lax.rsqrt(jnp.mean(xf * xf, axis=-1, keepdims=True) + EPS) * g


def _dot(a, b):
    return jnp.dot(a, b, preferred_element_type=F32)


def _dot_nt(a, b):
    return lax.dot_general(a, b, (((1,), (1,)), ((), ())), preferred_element_type=F32)


def _log_sigmoid_parts(z):
    return jnp.log1p(jnp.exp(-jnp.abs(z)))


def _prenorm_kernel(x_ref, g_ref, h_ref):
    h_ref[...] = _rms(x_ref[...], g_ref[...]).astype(BF16)


def prenorm(x, g, *, bm):
    m, d = x.shape
    return pl.pallas_call(
        _prenorm_kernel,
        grid=(m // bm,),
        in_specs=[pl.BlockSpec((bm, d), lambda i: (i, 0)), pl.BlockSpec((1, d), lambda i: (0, 0))],
        out_specs=pl.BlockSpec((bm, d), lambda i: (i, 0)),
        out_shape=jax.ShapeDtypeStruct((m, d), BF16),
        compiler_params=_params("parallel"),
        name="prenorm",
    )(x, g.reshape(1, d))


def _rope_table_kernel(pos_ref, invf_ref, sign_ref, cos_ref, sin_ref):
    ang = pos_ref[...].astype(F32) * invf_ref[...]
    cos_ref[...] = jnp.cos(ang)
    sin_ref[...] = jnp.sin(ang) * sign_ref[...]


def rope_tables(positions, *, bm):
    m = positions.size
    half = DIFF_QK_DIM // 2
    inv_freq = ROPE_THETA ** (-jnp.arange(half, dtype=F32) / half)
    invf = jnp.tile(inv_freq, LANES // half).reshape(1, LANES)
    sign = jnp.where((jnp.arange(LANES) % DIFF_QK_DIM) < half, -1.0, 1.0).astype(F32).reshape(1, LANES)
    row = pl.BlockSpec((bm, LANES), lambda i: (i, 0))
    const = pl.BlockSpec((1, LANES), lambda i: (0, 0))
    return pl.pallas_call(
        _rope_table_kernel,
        grid=(m // bm,),
        in_specs=[pl.BlockSpec((bm, 1), lambda i: (i, 0)), const, const],
        out_specs=[row, row],
        out_shape=[jax.ShapeDtypeStruct((m, LANES), F32)] * 2,
        compiler_params=_params("parallel"),
        name="rope_tables",
    )(positions.reshape(m, 1), invf, sign)


def _in_proj_kernel(h_ref, w_ref, cs_ref, cos_ref, sin_ref, o_ref, *, n_rope_blocks):
    j = pl.program_id(1)
    acc = _dot(h_ref[...], w_ref[...]) * cs_ref[...]
    bm, bn = acc.shape

    @pl.when(j < n_rope_blocks)
    def _():
        c, s = cos_ref[...], sin_ref[...]
        first_half = (lax.broadcasted_iota(jnp.int32, (bm, LANES), 1) % DIFF_QK_DIM) < DIFF_QK_DIM // 2
        for t in range(bn // LANES):
            a = acc[:, t * LANES:(t + 1) * LANES]
            partner = jnp.where(first_half, pltpu.roll(a, LANES - DIFF_QK_DIM // 2, 1),
                                pltpu.roll(a, DIFF_QK_DIM // 2, 1))
            o_ref[:, t * LANES:(t + 1) * LANES] = (a * c + partner * s).astype(BF16)

    @pl.when(j >= n_rope_blocks)
    def _():
        o_ref[...] = acc.astype(BF16)


def in_proj(h, w, colscale, cos_t, sin_t, *, bm, bn):
    m, d = h.shape
    n = w.shape[1]
    rope_cols = 2 * A_W
    assert rope_cols % bn == 0 and n % bn == 0 and m % bm == 0
    row_tab = pl.BlockSpec((bm, LANES), lambda i, j: (i, 0))
    return pl.pallas_call(
        functools.partial(_in_proj_kernel, n_rope_blocks=rope_cols // bn),
        grid=(m // bm, n // bn),
        in_specs=[pl.BlockSpec((bm, d), lambda i, j: (i, 0)),
                  pl.BlockSpec((d, bn), lambda i, j: (0, j)),
                  pl.BlockSpec((1, bn), lambda i, j: (0, j)),
                  row_tab, row_tab],
        out_specs=pl.BlockSpec((bm, bn), lambda i, j: (i, j)),
        out_shape=jax.ShapeDtypeStruct((m, n), BF16),
        compiler_params=_params("parallel", "arbitrary"),
        name="in_proj",
    )(h, w, colscale, cos_t, sin_t)


def _forget_cumsum_kernel(h_ref, wf_ref, bf_ref, cum_ref):
    s = h_ref.shape[0]
    fc = _dot_nt(wf_ref[...], h_ref[...]) + bf_ref[...]
    acc = jnp.minimum(fc, 0.0) - _log_sigmoid_parts(fc)
    lane = lax.broadcasted_iota(jnp.int32, acc.shape, 1)
    k = 1
    while k < s:
        acc = acc + jnp.where(lane >= k, pltpu.roll(acc, k, 1), 0.0)
        k *= 2
    cum_ref[...] = acc


def forget_cumsum(h, wf_t, bf, *, batch):
    m, d = h.shape
    s = m // batch
    return pl.pallas_call(
        _forget_cumsum_kernel,
        grid=(batch,),
        in_specs=[pl.BlockSpec((s, d), lambda b: (b, 0)),
                  pl.BlockSpec((GATE_ROWS, d), lambda b: (0, 0)),
                  pl.BlockSpec((GATE_ROWS, 1), lambda b: (0, 0))],
        out_specs=pl.BlockSpec((None, GATE_ROWS, s), lambda b: (b, 0, 0)),
        out_shape=jax.ShapeDtypeStruct((batch, GATE_ROWS, s), F32),
        compiler_params=_params("parallel"),
        name="forget_cumsum",
    )(h, wf_t, bf)


def _tile_lanes(a, width):
    return a if width == LANES else jnp.concatenate([a] * (width // LANES), axis=1)


def _head(ref, h):
    return ref[:, h * HEAD_DIM:(h + 1) * HEAD_DIM]


def _key_block(ref, kb, tk, h):
    return ref[pl.ds(pl.multiple_of(kb * tk, tk), tk), h * HEAD_DIM:(h + 1) * HEAD_DIM]


def _softmax_heads(n_heads, scores, values, m_ref, l_ref, acc_ref, *, lag):
    s, alpha, pv = ([None] * n_heads for _ in range(3))

    def probabilities(h):
        m_prev = m_ref[h]
        m_new = jnp.maximum(m_prev, jnp.max(s[h], axis=1, keepdims=True))
        alpha[h] = jnp.exp(m_prev - m_new)
        p = jnp.exp(s[h] - _tile_lanes(m_new, s[h].shape[1]))
        l_ref[h] = alpha[h] * l_ref[h] + jnp.sum(p, axis=1, keepdims=True)
        m_ref[h] = m_new
        pv[h] = _dot(p.astype(BF16), values(h))

    for t in range(n_heads + 2 * lag):
        if t < n_heads:
            s[t] = scores(t)
        if 0 <= t - lag < n_heads:
            probabilities(t - lag)
        if 0 <= t - 2 * lag < n_heads:
            acc_ref[t - 2 * lag] = alpha[t - 2 * lag] * acc_ref[t - 2 * lag] + pv[t - 2 * lag]


def _init_softmax_stats(m_ref, l_ref, acc_ref):
    m_ref[...] = jnp.full(m_ref.shape, NEG_INF, F32)
    l_ref[...] = jnp.zeros(l_ref.shape, F32)
    acc_ref[...] = jnp.zeros(acc_ref.shape, F32)


def _causal_loop(i, step):
    def body(kb, carry):
        step(kb, False)
        return carry

    lax.fori_loop(0, i, body, 0)
    step(i, True)


def _diff_attn_kernel(lq_ref, lk_ref, g_ref, q_ref, k_ref, v_ref, o_ref, qq_ref, m_ref, l_ref, acc_ref,
                      *, tq, lam_init):
    i = pl.program_id(1)
    low = lax.broadcasted_iota(jnp.int32, (tq, LANES), 1) < DIFF_QK_DIM
    for h in range(N_DIFF):
        q = _head(q_ref, h)
        zero = jnp.zeros_like(q)
        qq_ref[h, 0:tq, :] = jnp.where(low, q, zero)
        qq_ref[h, tq:2 * tq, :] = jnp.where(low, zero, q)
    _init_softmax_stats(m_ref, l_ref, acc_ref)

    def step(kb, masked):
        if masked:
            row = lax.broadcasted_iota(jnp.int32, (2 * tq, tq), 0)
            col = lax.broadcasted_iota(jnp.int32, (2 * tq, tq), 1)
            keep = jnp.where(row >= tq, row - tq, row) >= col

        def scores(h):
            s = _dot_nt(qq_ref[h], _key_block(k_ref, kb, tq, h))
            return jnp.where(keep, s, NEG_INF) if masked else s

        _softmax_heads(N_DIFF, scores, lambda h: _key_block(v_ref, kb, tq, h), m_ref, l_ref, acc_ref, lag=0)

    _causal_loop(i, step)

    prod = lq_ref[...] * lk_ref[...]
    first = lax.broadcasted_iota(jnp.int32, prod.shape, 1) < DIFF_QK_DIM
    e1 = jnp.exp(jnp.sum(jnp.where(first, prod, 0.0), axis=1, keepdims=True))
    e2 = jnp.exp(jnp.sum(jnp.where(first, 0.0, prod), axis=1, keepdims=True))
    lam = (e1 - e2) + lam_init
    for h in range(N_DIFF):
        o = (acc_ref[h, 0:tq, :] / l_ref[h, 0:tq, :]
             - lam * (acc_ref[h, tq:2 * tq, :] / l_ref[h, tq:2 * tq, :]))
        o_ref[:, h * HEAD_DIM:(h + 1) * HEAD_DIM] = (_rms(o, g_ref[...]) * (1.0 - lam_init)).astype(BF16)


def _group_specs(s, tq, width, first_col_block):
    nq = s // tq
    q_spec = pl.BlockSpec((tq, width), lambda b, i: (b * nq + i, first_col_block))
    k_spec = pl.BlockSpec((s, width), lambda b, i: (b, first_col_block + 1))
    v_spec = pl.BlockSpec((s, width), lambda b, i: (b, first_col_block + 2))
    o_spec = pl.BlockSpec((tq, width), lambda b, i: (b * nq + i, 0))
    return q_spec, k_spec, v_spec, o_spec


def diff_attention(proj, lam_q, lam_k, g_sub, lam_init, *, batch, tq):
    m = proj.shape[0]
    s = m // batch
    vec = pl.BlockSpec((1, LANES), lambda b, i: (0, 0))
    q_spec, k_spec, v_spec, o_spec = _group_specs(s, tq, A_W, QA_BLK * LANES // A_W)
    stat = pltpu.VMEM((N_DIFF, 2 * tq, HEAD_DIM), F32)
    return pl.pallas_call(
        functools.partial(_diff_attn_kernel, tq=tq, lam_init=lam_init),
        grid=(batch, s // tq),
        in_specs=[vec, vec, vec, q_spec, k_spec, v_spec],
        out_specs=o_spec,
        out_shape=jax.ShapeDtypeStruct((m, A_W), BF16),
        scratch_shapes=[pltpu.VMEM((N_DIFF, 2 * tq, HEAD_DIM), BF16), stat, stat, stat],
        compiler_params=_params("parallel", "arbitrary"),
        name="diff_attention",
    )(lam_q, lam_k, g_sub, proj, proj, proj)


def _sb_attn_kernel(q_ref, k_ref, v_ref, o_ref, qn_ref, r_ref, acc_ref, *, tq):
    i = pl.program_id(1)
    qn_ref[...] = -q_ref[...]
    r_ref[...] = jnp.zeros(r_ref.shape, F32)
    acc_ref[...] = jnp.zeros(acc_ref.shape, F32)
    jj = lax.broadcasted_iota(jnp.int32, (2 * tq, tq), 0)
    ss = lax.broadcasted_iota(jnp.int32, (2 * tq, tq), 1)
    tri = (jnp.where(jj >= tq, jj - tq, jj) > ss).astype(BF16)

    def step(kb, masked):
        if masked:
            strict = lax.broadcasted_iota(jnp.int32, (tq, tq), 0) > lax.broadcasted_iota(jnp.int32, (tq, tq), 1)
        zn, log_beta, after, row_sum = ([None] * N_SB for _ in range(4))

        def scores(h):
            zn[h] = _dot_nt(_head(qn_ref, h), _key_block(k_ref, kb, tq, h))

        def suffix_sums(h):
            sp = jnp.log(1.0 + jnp.exp(-jnp.abs(zn[h])))
            log_remain = jnp.minimum(zn[h], 0.0) - sp
            log_beta[h] = log_remain - zn[h]
            if masked:
                log_remain = jnp.where(strict, log_remain, 0.0)
            hi = log_remain.astype(BF16)
            lo = (log_remain - hi.astype(F32)).astype(BF16)
            after[h] = _dot(jnp.concatenate([hi, lo], axis=1), tri)
            row_sum[h] = jnp.sum(log_remain, axis=1, keepdims=True)

        def accumulate(h):
            w = jnp.exp(log_beta[h] + (after[h] + _tile_lanes(r_ref[h], tq)))
            if masked:
                w = jnp.where(strict, w, 0.0)
            acc_ref[h] += _dot(w.astype(BF16), _key_block(v_ref, kb, tq, h))
            r_ref[h] += row_sum[h]

        for t in range(N_SB + 2):
            if t < N_SB:
                scores(t)
            if 0 <= t - 1 < N_SB:
                suffix_sums(t - 1)
            if 0 <= t - 2 < N_SB:
                accumulate(t - 2)

    step(i, True)

    def body(t, carry):
        step(i - 1 - t, False)
        return carry

    lax.fori_loop(0, i, body, 0)
    for h in range(N_SB):
        o_ref[:, h * HEAD_DIM:(h + 1) * HEAD_DIM] = acc_ref[h].astype(BF16)


def stick_breaking_attention(proj, *, batch, tq):
    m = proj.shape[0]
    s = m // batch
    q_spec, k_spec, v_spec, o_spec = _group_specs(s, tq, SB_W, QB_BLK * LANES // SB_W)
    stat = pltpu.VMEM((N_SB, tq, HEAD_DIM), F32)
    return pl.pallas_call(
        functools.partial(_sb_attn_kernel, tq=tq),
        grid=(batch, s // tq),
        in_specs=[q_spec, k_spec, v_spec],
        out_specs=o_spec,
        out_shape=jax.ShapeDtypeStruct((m, SB_W), BF16),
        scratch_shapes=[pltpu.VMEM((tq, SB_W), BF16), stat, stat],
        compiler_params=_params("parallel", "arbitrary"),
        name="stick_breaking_attention",
    )(proj, proj, proj)


def _fox_attn_kernel(cum_ref, q_ref, k_ref, v_ref, o_ref, cq_ref, m_ref, l_ref, acc_ref, *, tq):
    i = pl.program_id(1)
    row = lax.broadcasted_iota(jnp.int32, (tq, tq), 0)
    col = lax.broadcasted_iota(jnp.int32, (tq, tq), 1)
    _init_softmax_stats(m_ref, l_ref, acc_ref)
    for h in range(N_FOX):
        cum_q_row = cum_ref[h:h + 1, pl.ds(pl.multiple_of(i * tq, tq), tq)]
        cum_q = jnp.sum(jnp.where(row == col, cum_q_row, 0.0), axis=1, keepdims=True)
        cq_ref[h] = jnp.broadcast_to(cum_q, (tq, HEAD_DIM))

    def step(kb, masked):
        def scores(h):
            cum_k = cum_ref[h:h + 1, pl.ds(pl.multiple_of(kb * tq, tq), tq)]
            s = _dot_nt(_head(q_ref, h), _key_block(k_ref, kb, tq, h)) + (_tile_lanes(cq_ref[h], tq) - cum_k)
            return jnp.where(row >= col, s, NEG_INF) if masked else s

        _softmax_heads(N_FOX, scores, lambda h: _key_block(v_ref, kb, tq, h), m_ref, l_ref, acc_ref, lag=1)

    _causal_loop(i, step)
    for h in range(N_FOX):
        o_ref[:, h * HEAD_DIM:(h + 1) * HEAD_DIM] = (acc_ref[h] / l_ref[h]).astype(BF16)


def forgetting_attention(proj, cum, *, batch, tq):
    m = proj.shape[0]
    s = m // batch
    q_spec, k_spec, v_spec, o_spec = _group_specs(s, tq, FOX_W, QC_BLK * LANES // FOX_W)
    stat = pltpu.VMEM((N_FOX, tq, HEAD_DIM), F32)
    return pl.pallas_call(
        functools.partial(_fox_attn_kernel, tq=tq),
        grid=(batch, s // tq),
        in_specs=[pl.BlockSpec((None, GATE_ROWS, s), lambda b, i: (b, 0, 0)), q_spec, k_spec, v_spec],
        out_specs=o_spec,
        out_shape=jax.ShapeDtypeStruct((m, FOX_W), BF16),
        scratch_shapes=[stat, stat, stat, stat],
        compiler_params=_params("parallel", "arbitrary"),
        name="forgetting_attention",
    )(cum, proj, proj, proj)


def _residual_tail(y, x, g_post, g_next, x_out_ref, h_out_ref):
    x_new = x + _rms(y, g_post)
    x_out_ref[...] = x_new
    if h_out_ref is not None:
        h_out_ref[...] = _rms(x_new, g_next).astype(BF16)


def _mix_out_kernel(oa_ref, ob_ref, oc_ref, x_ref, w_ref, gb_ref, gc_ref, gp_ref, gn_ref, xo_ref, ho_ref):
    ob = _rms(ob_ref[...].astype(F32), gb_ref[...]).astype(BF16)
    oc = _rms(oc_ref[...].astype(F32), gc_ref[...]).astype(BF16)
    y = (_dot(oa_ref[...], w_ref[0:A_W, :]) + _dot(ob, w_ref[A_W:A_W + SB_W, :])
         + _dot(oc, w_ref[A_W + SB_W:, :]))
    _residual_tail(y, x_ref[...], gp_ref[...], gn_ref[...], xo_ref, ho_ref)


def mix_out(oa, ob, oc, x, w, g_sb, g_fox, g_post, g_next, *, bm):
    m, d = x.shape
    rows = lambda width: pl.BlockSpec((bm, width), lambda i: (i, 0))
    const = lambda width: pl.BlockSpec((1, width), lambda i: (0, 0))
    return pl.pallas_call(
        _mix_out_kernel,
        grid=(m // bm,),
        in_specs=[rows(A_W), rows(SB_W), rows(FOX_W), rows(d),
                  pl.BlockSpec(w.shape, lambda i: (0, 0)),
                  const(SB_W), const(FOX_W), const(d), const(d)],
        out_specs=[rows(d), rows(d)],
        out_shape=[jax.ShapeDtypeStruct((m, d), F32), jax.ShapeDtypeStruct((m, d), BF16)],
        compiler_params=_params("parallel"),
        name="mix_out",
    )(oa, ob, oc, x, w, g_sb.reshape(1, -1), g_fox.reshape(1, -1), g_post.reshape(1, -1), g_next.reshape(1, -1))


def _norm_matmul_kernel(x_ref, g_ref, w_ref, o_ref):
    o_ref[...] = _dot(_rms(x_ref[...], g_ref[...]).astype(BF16), w_ref[...]).astype(BF16)


def norm_matmul(x, g, w, *, bm):
    m, d = x.shape
    n = w.shape[1]
    return pl.pallas_call(
        _norm_matmul_kernel,
        grid=(m // bm,),
        in_specs=[pl.BlockSpec((bm, d), lambda i: (i, 0)), pl.BlockSpec((1, d), lambda i: (0, 0)),
                  pl.BlockSpec((d, n), lambda i: (0, 0))],
        out_specs=pl.BlockSpec((bm, n), lambda i: (i, 0)),
        out_shape=jax.ShapeDtypeStruct((m, n), BF16),
        compiler_params=_params("parallel"),
        name="memory_kv",
    )(x, g.reshape(1, d), w)


def _cross_attn_kernel(h_ref, x_ref, kv_ref, wq_ref, wo_ref, gp_ref, gn_ref, xo_ref, ho_ref):
    q = (_dot(h_ref[...], wq_ref[...]) * HEAD_DIM ** -0.5).astype(BF16)
    heads = []
    for hd in range(N_CROSS_HEADS):
        k = kv_ref[:, hd * HEAD_DIM:(hd + 1) * HEAD_DIM]
        v = kv_ref[:, CROSS_W + hd * HEAD_DIM:CROSS_W + (hd + 1) * HEAD_DIM]
        s = _dot_nt(q[:, hd * HEAD_DIM:(hd + 1) * HEAD_DIM], k)
        p = jnp.exp(s - jnp.max(s, axis=1, keepdims=True))
        o = _dot(p.astype(BF16), v) / jnp.sum(p, axis=1, keepdims=True)
        heads.append(o.astype(BF16))
    y = _dot(jnp.concatenate(heads, axis=1), wo_ref[...])
    _residual_tail(y, x_ref[...], gp_ref[...], gn_ref[...], xo_ref, ho_ref)


def cross_attention(h, x, kv, wq, wo, g_post, g_next, *, batch, bm):
    m, d = x.shape
    s = m // batch
    mem_len = kv.shape[0] // batch
    nb = s // bm
    rows = lambda width: pl.BlockSpec((bm, width), lambda b, i: (b * nb + i, 0))
    const = lambda shape: pl.BlockSpec(shape, lambda b, i: (0, 0))
    return pl.pallas_call(
        _cross_attn_kernel,
        grid=(batch, nb),
        in_specs=[rows(d), rows(d),
                  pl.BlockSpec((mem_len, 2 * CROSS_W), lambda b, i: (b, 0)),
                  const(wq.shape), const(wo.shape), const((1, d)), const((1, d))],
        out_specs=[rows(d), rows(d)],
        out_shape=[jax.ShapeDtypeStruct((m, d), F32), jax.ShapeDtypeStruct((m, d), BF16)],
        compiler_params=_params("parallel", "parallel"),
        name="cross_attention",
    )(h, x, kv, wq, wo, g_post.reshape(1, d), g_next.reshape(1, d))


def _ffn_up_kernel(h_ref, wg_ref, wu_ref, o_ref):
    h = h_ref[...]
    gate = _dot(h, wg_ref[...])
    up = _dot(h, wu_ref[...])
    o_ref[...] = (gate * jax.nn.sigmoid(gate) * up).astype(BF16)


def ffn_up(h, wg, wu, *, bm, bn):
    m, d = h.shape
    n = wg.shape[1]
    wspec = pl.BlockSpec((d, bn), lambda i, j: (0, j))
    return pl.pallas_call(
        _ffn_up_kernel,
        grid=(m // bm, n // bn),
        in_specs=[pl.BlockSpec((bm, d), lambda i, j: (i, 0)), wspec, wspec],
        out_specs=pl.BlockSpec((bm, bn), lambda i, j: (i, j)),
        out_shape=jax.ShapeDtypeStruct((m, n), BF16),
        compiler_params=_params("parallel", "arbitrary"),
        name="ffn_up",
    )(h, wg, wu)


def _ffn_down_kernel(*refs, emit_h):
    if emit_h:
        a_ref, w_ref, x_ref, gp_ref, gn_ref, xo_ref, ho_ref, acc_ref = refs
    else:
        a_ref, w_ref, x_ref, gp_ref, xo_ref, acc_ref = refs
        gn_ref = ho_ref = None
    kk = pl.program_id(1)

    @pl.when(kk == 0)
    def _():
        acc_ref[...] = jnp.zeros(acc_ref.shape, F32)

    acc_ref[...] += _dot(a_ref[...], w_ref[...])

    @pl.when(kk == pl.num_programs(1) - 1)
    def _():
        g_next = gn_ref[...] if emit_h else None
        _residual_tail(acc_ref[...], x_ref[...], gp_ref[...], g_next, xo_ref, ho_ref)


def ffn_down(a, w, x, g_post, g_next, *, bm, bk):
    m, d = x.shape
    kdim = a.shape[1]
    emit_h = g_next is not None
    rows = pl.BlockSpec((bm, d), lambda i, k: (i, 0))
    const = pl.BlockSpec((1, d), lambda i, k: (0, 0))
    gains = [g_post.reshape(1, d)] + ([g_next.reshape(1, d)] if emit_h else [])
    out_shape = [jax.ShapeDtypeStruct((m, d), F32)] + ([jax.ShapeDtypeStruct((m, d), BF16)] if emit_h else [])
    return pl.pallas_call(
        functools.partial(_ffn_down_kernel, emit_h=emit_h),
        grid=(m // bm, kdim // bk),
        in_specs=[pl.BlockSpec((bm, bk), lambda i, k: (i, k)),
                  pl.BlockSpec((bk, d), lambda i, k: (k, 0)),
                  rows] + [const] * len(gains),
        out_specs=[rows] * len(out_shape),
        out_shape=out_shape,
        scratch_shapes=[pltpu.VMEM((bm, d), F32)],
        compiler_params=_params("parallel", "arbitrary"),
        name="ffn_down",
    )(a, w, x, *gains)


def _pick(n, *candidates):
    for c in candidates:
        if n % c == 0:
            return c
    return n


def kernel(x, mem, positions, g_mix_pre, g_mix_post, w_in, b_f, lam_q1, lam_k1, lam_q2, lam_k2, g_diff_sub, g_sb_out, g_fox_out, w_out, g_x_pre, g_x_post, g_mem, w_cq, w_ckv, w_co, g_ffn_pre, g_ffn_post, w_gate, w_up, w_down):
    batch, seq, d = x.shape
    depth = w_in.shape[0]
    m = batch * seq
    d_ff = w_gate.shape[2]
    assert w_in.shape[2] == QKV_COLS + N_FOX and d == A_W + SB_W + FOX_W

    bm_row = _pick(m, 512, 256, 128)
    bm_mm = _pick(m, 1024, 512, 256, 128)
    bn_in = 2 * A_W
    bn_ff = _pick(d_ff, 512, 256, 128)
    bk_ff = _pick(d_ff, 1408, 1024, 512, 256, 128)
    tq = _pick(seq, 256, 128)
    bm_x = _pick(seq, 512, 256, 128)
    bm_mem = _pick(mem.shape[0] * mem.shape[1], 512, 256, 128)

    w_qkv = w_in[:, :, :QKV_COLS].astype(BF16)
    wf_t = jnp.pad(jnp.swapaxes(w_in[:, :, QKV_COLS:], 1, 2), ((0, 0), (0, GATE_ROWS - N_FOX), (0, 0))).astype(BF16)
    bf = jnp.pad(b_f, ((0, 0), (0, GATE_ROWS - N_FOX))).reshape(depth, GATE_ROWS, 1)
    w_out_b, w_cq_b, w_ckv_b, w_co_b = (w.astype(BF16) for w in (w_out, w_cq, w_ckv, w_co))
    w_gate_b, w_up_b, w_down_b = (w.astype(BF16) for w in (w_gate, w_up, w_down))
    lam_q = jnp.concatenate([lam_q1, lam_q2], axis=1).reshape(depth, 1, LANES)
    lam_k = jnp.concatenate([lam_k1, lam_k2], axis=1).reshape(depth, 1, LANES)

    colscale = jnp.ones((QKV_COLS,), F32)
    colscale = colscale.at[QA_BLK * LANES:KA_BLK * LANES].set(DIFF_QK_DIM ** -0.5)
    colscale = colscale.at[QB_BLK * LANES:KB_BLK * LANES].set(HEAD_DIM ** -0.5)
    colscale = colscale.at[QC_BLK * LANES:KC_BLK * LANES].set(HEAD_DIM ** -0.5).reshape(1, QKV_COLS)

    x = x.reshape(m, d)
    mem2 = mem.reshape(-1, d)
    cos_t, sin_t = rope_tables(positions, bm=bm_row)
    h = prenorm(x, g_mix_pre[0], bm=bm_row)

    for l in range(depth):
        lam_init = 0.8 - 0.6 * math.exp(-0.3 * l)
        proj = in_proj(h, w_qkv[l], colscale, cos_t, sin_t, bm=bm_mm, bn=bn_in)
        cum = forget_cumsum(h, wf_t[l], bf[l], batch=batch)
        oa = diff_attention(proj, lam_q[l], lam_k[l], g_diff_sub[l].reshape(1, LANES), lam_init, batch=batch, tq=tq)
        ob = stick_breaking_attention(proj, batch=batch, tq=tq)
        oc = forgetting_attention(proj, cum, batch=batch, tq=tq)
        x, h = mix_out(oa, ob, oc, x, w_out_b[l], g_sb_out[l], g_fox_out[l], g_mix_post[l], g_x_pre[l], bm=bm_row // 2)
        kv = norm_matmul(mem2, g_mem[l], w_ckv_b[l], bm=bm_mem)
        x, h = cross_attention(h, x, kv, w_cq_b[l], w_co_b[l], g_x_post[l], g_ffn_pre[l], batch=batch, bm=bm_x)
        a = ffn_up(h, w_gate_b[l], w_up_b[l], bm=bm_mm, bn=bn_ff)
        g_next = g_mix_pre[l + 1] if l + 1 < depth else None
        x, *rest = ffn_down(a, w_down_b[l], x, g_ffn_post[l], g_next, bm=bm_row, bk=bk_ff)
        h = rest[0] if rest else None
    return x.reshape(batch, seq, d)
```

```python
import functools
import math

import jax
import jax.numpy as jnp
from jax import lax
from jax.experimental import pallas as pl
from jax.experimental.pallas import tpu as pltpu

F32 = jnp.float32
BF16 = jnp.bfloat16

EPS = 1e-6
NEG_INF = -1e30
ROPE_THETA = 10000.0

LANES = 128
HEAD_DIM = 128
N_DIFF, N_SB, N_FOX = 4, 6, 6
DIFF_QK_DIM = HEAD_DIM // 2
A_W, SB_W, FOX_W = N_DIFF * HEAD_DIM, N_SB * HEAD_DIM, N_FOX * HEAD_DIM
QK_COLS = 2 * (A_W + SB_W + FOX_W)
V_ROWS = A_W + SB_W + FOX_W
N_CROSS_HEADS = 4
CROSS_W = N_CROSS_HEADS * HEAD_DIM
GATE_PARTS = 3
GATE_LANES = GATE_PARTS * N_FOX

VMEM_LIMIT = 56 * 1024 * 1024

_IN_OFFSETS = {}
_off = 0
for _name, _width in (("qa", A_W), ("ka", A_W), ("va", A_W), ("qb", SB_W), ("kb", SB_W), ("vb", SB_W),
                      ("qc", FOX_W), ("kc", FOX_W), ("vc", FOX_W), ("fc", N_FOX)):
    _IN_OFFSETS[_name] = (_off, _off + _width)
    _off += _width
QK_ORDER = ("qb", "kb", "qc", "kc", "qa", "ka")
V_ORDER = ("vb", "vc", "va")
QB_COL, QC_COL, QA_COL = 0, 2, (4 * SB_W) // A_W
VB_ROW, VC_ROW, VA_ROW = 0, 1, (2 * SB_W) // A_W


def _params(*semantics):
    return pltpu.CompilerParams(dimension_semantics=semantics, vmem_limit_bytes=VMEM_LIMIT)


def _rms(xf, g):
    return xf * lax.rsqrt(jnp.mean(xf * xf, axis=-1, keepdims=True) + EPS) * g


def _dot(a, b):
    return jnp.dot(a, b, preferred_element_type=F32)


def _dot_nt(a, b):
    return lax.dot_general(a, b, (((1,), (1,)), ((), ())), preferred_element_type=F32)


def _bf16_parts(x):
    p0 = x.astype(BF16)
    r1 = x - p0.astype(F32)
    p1 = r1.astype(BF16)
    p2 = (r1 - p1.astype(F32)).astype(BF16)
    return p0, p1, p2


def _prenorm_kernel(x_ref, g_ref, h_ref):
    h_ref[...] = _rms(x_ref[...], g_ref[...]).astype(BF16)


def prenorm(x, g, *, bm):
    m, d = x.shape
    return pl.pallas_call(
        _prenorm_kernel,
        grid=(m // bm,),
        in_specs=[pl.BlockSpec((bm, d), lambda i: (i, 0)), pl.BlockSpec((1, d), lambda i: (0, 0))],
        out_specs=pl.BlockSpec((bm, d), lambda i: (i, 0)),
        out_shape=jax.ShapeDtypeStruct((m, d), BF16),
        compiler_params=_params("parallel"),
        name="prenorm",
    )(x, g.reshape(1, d))


def _rope_table_kernel(pos_ref, invf_ref, sign_ref, cos_ref, sin_ref):
    ang = pos_ref[...].astype(F32) * invf_ref[...]
    cos_ref[...] = jnp.cos(ang)
    sin_ref[...] = jnp.sin(ang) * sign_ref[...]


def rope_tables(positions, *, bm):
    m = positions.size
    half = DIFF_QK_DIM // 2
    inv_freq = ROPE_THETA ** (-jnp.arange(half, dtype=F32) / half)
    invf = jnp.tile(inv_freq, LANES // half).reshape(1, LANES)
    sign = jnp.where((jnp.arange(LANES) % DIFF_QK_DIM) < half, -1.0, 1.0).astype(F32).reshape(1, LANES)
    row = pl.BlockSpec((bm, LANES), lambda i: (i, 0))
    const = pl.BlockSpec((1, LANES), lambda i: (0, 0))
    return pl.pallas_call(
        _rope_table_kernel,
        grid=(m // bm,),
        in_specs=[pl.BlockSpec((bm, 1), lambda i: (i, 0)), const, const],
        out_specs=[row, row],
        out_shape=[jax.ShapeDtypeStruct((m, LANES), F32)] * 2,
        compiler_params=_params("parallel"),
        name="rope_tables",
    )(positions.reshape(m, 1), invf, sign)


def _in_proj_qk_kernel(h_ref, w_ref, cs_ref, cos_ref, sin_ref, o_ref, *, first_rope_block):
    j = pl.program_id(1)
    acc = _dot(h_ref[...], w_ref[...]) * cs_ref[...]
    bm, bn = acc.shape

    @pl.when(j >= first_rope_block)
    def _():
        c, s = cos_ref[...], sin_ref[...]
        first_half = (lax.broadcasted_iota(jnp.int32, (bm, LANES), 1) % DIFF_QK_DIM) < DIFF_QK_DIM // 2
        for t in range(bn // LANES):
            a = acc[:, t * LANES:(t + 1) * LANES]
            partner = jnp.where(first_half, pltpu.roll(a, LANES - DIFF_QK_DIM // 2, 1),
                                pltpu.roll(a, DIFF_QK_DIM // 2, 1))
            o_ref[:, t * LANES:(t + 1) * LANES] = (a * c + partner * s).astype(BF16)

    @pl.when(j < first_rope_block)
    def _():
        o_ref[...] = acc.astype(BF16)


def in_proj_qk(h, w, colscale, cos_t, sin_t, *, bm, bn):
    m, d = h.shape
    n = w.shape[1]
    rope_cols = 2 * A_W
    assert rope_cols % bn == 0 and n % bn == 0 and m % bm == 0
    row_tab = pl.BlockSpec((bm, LANES), lambda i, j: (i, 0))
    return pl.pallas_call(
        functools.partial(_in_proj_qk_kernel, first_rope_block=(n - rope_cols) // bn),
        grid=(m // bm, n // bn),
        in_specs=[pl.BlockSpec((bm, d), lambda i, j: (i, 0)),
                  pl.BlockSpec((d, bn), lambda i, j: (0, j)),
                  pl.BlockSpec((1, bn), lambda i, j: (0, j)),
                  row_tab, row_tab],
        out_specs=pl.BlockSpec((bm, bn), lambda i, j: (i, j)),
        out_shape=jax.ShapeDtypeStruct((m, n), BF16),
        compiler_params=_params("parallel", "arbitrary"),
        name="in_proj_qk",
    )(h, w, colscale, cos_t, sin_t)


def _in_proj_vt_kernel(h_ref, wt_ref, o_ref, *, tk):
    res = _dot_nt(wt_ref[...], h_ref[...]).astype(BF16)
    for c in range(o_ref.shape[0]):
        o_ref[c] = res[:, c * tk:(c + 1) * tk]


def in_proj_vt(h, wt, *, bm, bn, tk):
    m, d = h.shape
    n = wt.shape[0]
    return pl.pallas_call(
        functools.partial(_in_proj_vt_kernel, tk=tk),
        grid=(m // bm, n // bn),
        in_specs=[pl.BlockSpec((bm, d), lambda i, j: (i, 0)),
                  pl.BlockSpec((bn, d), lambda i, j: (j, 0))],
        out_specs=pl.BlockSpec((bm // tk, bn, tk), lambda i, j: (i, j, 0)),
        out_shape=jax.ShapeDtypeStruct((m // tk, n, tk), BF16),
        compiler_params=_params("parallel", "arbitrary"),
        name="in_proj_vt",
    )(h, wt)


def _forget_features_kernel(h_ref, wf_ref, bf_ref, part_ref, fk_ref, fq_ref, *, blk):
    s = h_ref.shape[0]
    lane = lax.broadcasted_iota(jnp.int32, (blk, LANES), 1)
    part = jnp.broadcast_to(part_ref[...], (blk, LANES))
    lower = (lax.broadcasted_iota(jnp.int32, (blk, blk), 0)
             >= lax.broadcasted_iota(jnp.int32, (blk, blk), 1)).astype(BF16)
    one = jnp.ones((blk, LANES), F32)
    zero = jnp.zeros((blk, LANES), F32)
    carry = jnp.zeros((1, LANES), F32)
    for t in range(s // blk):
        rows = slice(t * blk, (t + 1) * blk)
        fc = _dot(h_ref[rows, :], wf_ref[...]) + bf_ref[...]
        log_f = jnp.minimum(fc, 0.0) - jnp.log1p(jnp.exp(-jnp.abs(fc)))
        x0, x1, x2 = _bf16_parts(log_f)
        cum = _dot(lower, x0) + _dot(lower, x1) + _dot(lower, x2) + carry
        carry = cum[blk - 1:blk, :]
        c0, c1, c2 = (c.astype(F32) for c in _bf16_parts(cum))
        parts = jnp.where(part == 0, c0, jnp.where(part == 1, c1, c2))
        fk = jnp.where(lane < GATE_LANES, parts, jnp.where(lane < 2 * GATE_LANES, one, zero))
        fq = jnp.where(lane < GATE_LANES, -one, jnp.where(lane < 2 * GATE_LANES, parts, zero))
        fk_ref[rows, :] = fk.astype(BF16)
        fq_ref[rows, :] = fq.astype(BF16)


def forget_features(h, wf_rep, bf_rep, *, batch, blk):
    m, d = h.shape
    s = m // batch
    part = (jnp.arange(LANES, dtype=jnp.int32) % GATE_PARTS).reshape(1, LANES)
    rows = pl.BlockSpec((s, LANES), lambda b: (b, 0))
    vec = pl.BlockSpec((1, LANES), lambda b: (0, 0))
    return pl.pallas_call(
        functools.partial(_forget_features_kernel, blk=blk),
        grid=(batch,),
        in_specs=[pl.BlockSpec((s, d), lambda b: (b, 0)),
                  pl.BlockSpec((d, LANES), lambda b: (0, 0)), vec, vec],
        out_specs=[rows, rows],
        out_shape=[jax.ShapeDtypeStruct((m, LANES), BF16)] * 2,
        compiler_params=_params("parallel"),
        name="forget_features",
    )(h, wf_rep, bf_rep, part)


def _head(ref, h):
    return ref[:, h * HEAD_DIM:(h + 1) * HEAD_DIM]


def _key_block(ref, kb, tk, h):
    return ref[pl.ds(pl.multiple_of(kb * tk, tk), tk), h * HEAD_DIM:(h + 1) * HEAD_DIM]


def _value_block_t(ref, kb, h):
    return ref[kb, h * HEAD_DIM:(h + 1) * HEAD_DIM, :]


def _run_pipeline(i, n_heads, stages, lag):
    n = len(stages)
    lead = [(n - 1 - k) * lag for k in range(n)]
    assert lead[0] <= n_heads

    def positions(cur, nxt, cur_masked, first, last):
        for t in range(first, last):
            for k in range(n):
                idx = t + lead[k]
                if 0 <= idx < n_heads:
                    stages[k](cur, idx, cur_masked)
                elif idx >= n_heads and nxt is not None:
                    stages[k](nxt, idx - n_heads, False)

    positions(i, None, True, -lead[0], 0)
    positions(i, jnp.maximum(i - 1, 0), True, 0, n_heads)

    def body(j, carry):
        cur = i - j
        positions(cur, jnp.maximum(cur - 1, 0), False, 0, n_heads)
        return carry

    lax.fori_loop(1, i + 1, body, 0)


def _softmax_stages(scores, values_t, keep, m_ref, l_ref, acc_ref, s_buf, mn_buf, al_buf, ps_buf, pv_buf):
    def stage_scores(kb, h, masked):
        s_buf[h] = scores(kb, h)

    def stage_probabilities(kb, h, masked):
        s = s_buf[h]
        if masked:
            s = jnp.where(keep, s, NEG_INF)
        m_prev = m_ref[h]
        m_new = jnp.maximum(m_prev, jnp.max(s, axis=0, keepdims=True))
        p = jnp.exp(s - m_new)
        mn_buf[h] = m_new
        al_buf[h] = jnp.exp(m_prev - m_new)
        ps_buf[h] = jnp.sum(p, axis=0, keepdims=True)
        pv_buf[h] = _dot(values_t(kb, h), p.astype(BF16))

    def stage_commit(kb, h, masked):
        alpha = al_buf[h]
        m_ref[h] = mn_buf[h]
        l_ref[h] = alpha * l_ref[h] + ps_buf[h]
        acc_ref[h] = alpha * acc_ref[h] + pv_buf[h]

    return stage_scores, stage_probabilities, stage_commit


def _softmax_scratch(n_heads, tk, nq):
    stat = pltpu.VMEM((n_heads, 1, nq), F32)
    wide = pltpu.VMEM((n_heads, HEAD_DIM, nq), F32)
    return [stat, stat, wide,
            pltpu.VMEM((n_heads, tk, nq), F32), stat, stat, stat, wide]


def _init_softmax_stats(m_ref, l_ref, acc_ref):
    m_ref[...] = jnp.full(m_ref.shape, NEG_INF, F32)
    l_ref[...] = jnp.zeros(l_ref.shape, F32)
    acc_ref[...] = jnp.zeros(acc_ref.shape, F32)


def _group_specs(s, tq, width, q_col, v_row):
    nq = s // tq
    q_spec = pl.BlockSpec((tq, width), lambda b, i: (b * nq + i, q_col))
    k_spec = pl.BlockSpec((s, width), lambda b, i: (b, q_col + 1))
    vt_spec = pl.BlockSpec((nq, width, tq), lambda b, i: (b, v_row, 0))
    o_spec = pl.BlockSpec((tq, width), lambda b, i: (b * nq + i, 0))
    return q_spec, k_spec, vt_spec, o_spec


def _diff_attn_kernel(lq_ref, lk_ref, g_ref, q_ref, k_ref, vt_ref, o_ref, qq_ref, m_ref, l_ref, acc_ref,
                      *bufs, tq, lam_init, lag):
    i = pl.program_id(1)
    low = lax.broadcasted_iota(jnp.int32, (tq, LANES), 1) < DIFF_QK_DIM
    for h in range(N_DIFF):
        q = _head(q_ref, h)
        zero = jnp.zeros_like(q)
        qq_ref[h, 0:tq, :] = jnp.where(low, q, zero)
        qq_ref[h, tq:2 * tq, :] = jnp.where(low, zero, q)
    _init_softmax_stats(m_ref, l_ref, acc_ref)

    key = lax.broadcasted_iota(jnp.int32, (tq, 2 * tq), 0)
    qry = lax.broadcasted_iota(jnp.int32, (tq, 2 * tq), 1)
    keep = jnp.where(qry >= tq, qry - tq, qry) >= key

    def scores(kb, h):
        return _dot_nt(_key_block(k_ref, kb, tq, h), qq_ref[h])

    stages = _softmax_stages(scores, lambda kb, h: _value_block_t(vt_ref, kb, h), keep,
                             m_ref, l_ref, acc_ref, *bufs)
    _run_pipeline(i, N_DIFF, stages, lag)

    prod = lq_ref[...] * lk_ref[...]
    first = lax.broadcasted_iota(jnp.int32, prod.shape, 1) < DIFF_QK_DIM
    e1 = jnp.exp(jnp.sum(jnp.where(first, prod, 0.0), axis=1, keepdims=True))
    e2 = jnp.exp(jnp.sum(jnp.where(first, 0.0, prod), axis=1, keepdims=True))
    lam = (e1 - e2) + lam_init
    for h in range(N_DIFF):
        o_t = (acc_ref[h, :, 0:tq] / l_ref[h, :, 0:tq]
               - lam * (acc_ref[h, :, tq:2 * tq] / l_ref[h, :, tq:2 * tq]))
        o_ref[:, h * HEAD_DIM:(h + 1) * HEAD_DIM] = (_rms(o_t.T, g_ref[...]) * (1.0 - lam_init)).astype(BF16)


def diff_attention(qk, vt, lam_q, lam_k, g_sub, lam_init, *, batch, tq, lag):
    m = qk.shape[0]
    s = m // batch
    vec = pl.BlockSpec((1, LANES), lambda b, i: (0, 0))
    q_spec, k_spec, vt_spec, o_spec = _group_specs(s, tq, A_W, QA_COL, VA_ROW)
    return pl.pallas_call(
        functools.partial(_diff_attn_kernel, tq=tq, lam_init=lam_init, lag=lag),
        grid=(batch, s // tq),
        in_specs=[vec, vec, vec, q_spec, k_spec, vt_spec],
        out_specs=o_spec,
        out_shape=jax.ShapeDtypeStruct((m, A_W), BF16),
        scratch_shapes=[pltpu.VMEM((N_DIFF, 2 * tq, HEAD_DIM), BF16)] + _softmax_scratch(N_DIFF, tq, 2 * tq),
        compiler_params=_params("parallel", "arbitrary"),
        name="diff_attention",
    )(lam_q, lam_k, g_sub, qk, qk, vt)


def _sb_attn_kernel(q_ref, k_ref, vt_ref, o_ref, qn_ref, r_ref, acc_ref,
                    zn_buf, lb_buf, after_buf, cs_buf, pv_buf, *, tq, lag):
    i = pl.program_id(1)
    qn_ref[...] = -q_ref[...]
    r_ref[...] = jnp.zeros(r_ref.shape, F32)
    acc_ref[...] = jnp.zeros(acc_ref.shape, F32)
    ss = lax.broadcasted_iota(jnp.int32, (tq, 2 * tq), 0)
    jj = lax.broadcasted_iota(jnp.int32, (tq, 2 * tq), 1)
    tri = (jnp.where(jj >= tq, jj - tq, jj) > ss).astype(BF16)

    strict = lax.broadcasted_iota(jnp.int32, (tq, tq), 0) < lax.broadcasted_iota(jnp.int32, (tq, tq), 1)

    def scores(kb, h, masked):
        zn_buf[h] = _dot_nt(_key_block(k_ref, kb, tq, h), _head(qn_ref, h))

    def suffix_sums(kb, h, masked):
        zn = zn_buf[h]
        sp = jnp.log(1.0 + jnp.exp(-jnp.abs(zn)))
        log_remain = jnp.minimum(zn, 0.0) - sp
        lb_buf[h] = log_remain - zn
        if masked:
            log_remain = jnp.where(strict, log_remain, 0.0)
        hi = log_remain.astype(BF16)
        lo = (log_remain - hi.astype(F32)).astype(BF16)
        after_buf[h] = _dot(tri, jnp.concatenate([hi, lo], axis=0))
        cs_buf[h] = jnp.sum(log_remain, axis=0, keepdims=True)

    def weights(kb, h, masked):
        w = jnp.exp(lb_buf[h] + (after_buf[h] + r_ref[h]))
        if masked:
            w = jnp.where(strict, w, 0.0)
        pv_buf[h] = _dot(_value_block_t(vt_ref, kb, h), w.astype(BF16))
        r_ref[h] += cs_buf[h]

    def accumulate(kb, h, masked):
        acc_ref[h] += pv_buf[h]

    _run_pipeline(i, N_SB, (scores, suffix_sums, weights, accumulate), lag)
    for h in range(N_SB):
        o_ref[:, h * HEAD_DIM:(h + 1) * HEAD_DIM] = acc_ref[h].T.astype(BF16)


def stick_breaking_attention(qk, vt, *, batch, tq, lag):
    m = qk.shape[0]
    s = m // batch
    q_spec, k_spec, vt_spec, o_spec = _group_specs(s, tq, SB_W, QB_COL, VB_ROW)
    stat = pltpu.VMEM((N_SB, 1, tq), F32)
    wide = pltpu.VMEM((N_SB, HEAD_DIM, tq), F32)
    tile = pltpu.VMEM((N_SB, tq, tq), F32)
    return pl.pallas_call(
        functools.partial(_sb_attn_kernel, tq=tq, lag=lag),
        grid=(batch, s // tq),
        in_specs=[q_spec, k_spec, vt_spec],
        out_specs=o_spec,
        out_shape=jax.ShapeDtypeStruct((m, SB_W), BF16),
        scratch_shapes=[pltpu.VMEM((tq, SB_W), BF16), stat, wide,
                        tile, tile, tile, stat, wide],
        compiler_params=_params("parallel", "arbitrary"),
        name="stick_breaking_attention",
    )(qk, qk, vt)


def _fox_attn_kernel(fq_ref, fk_ref, q_ref, k_ref, vt_ref, o_ref, qa_ref, m_ref, l_ref, acc_ref, *bufs, tq, lag):
    i = pl.program_id(1)
    lane = lax.broadcasted_iota(jnp.int32, (tq, LANES), 1)
    fq = fq_ref[...]
    for h in range(N_FOX):
        lo = GATE_PARTS * h
        mine = (((lane >= lo) & (lane < lo + GATE_PARTS))
                | ((lane >= GATE_LANES + lo) & (lane < GATE_LANES + lo + GATE_PARTS)))
        qa_ref[h, :, 0:HEAD_DIM] = _head(q_ref, h)
        qa_ref[h, :, HEAD_DIM:2 * HEAD_DIM] = jnp.where(mine, fq, jnp.zeros_like(fq))
    _init_softmax_stats(m_ref, l_ref, acc_ref)

    keep = lax.broadcasted_iota(jnp.int32, (tq, tq), 1) >= lax.broadcasted_iota(jnp.int32, (tq, tq), 0)

    def scores(kb, h):
        fk = fk_ref[pl.ds(pl.multiple_of(kb * tq, tq), tq), :]
        return _dot_nt(jnp.concatenate([_key_block(k_ref, kb, tq, h), fk], axis=1), qa_ref[h])

    stages = _softmax_stages(scores, lambda kb, h: _value_block_t(vt_ref, kb, h), keep,
                             m_ref, l_ref, acc_ref, *bufs)
    _run_pipeline(i, N_FOX, stages, lag)
    for h in range(N_FOX):
        o_ref[:, h * HEAD_DIM:(h + 1) * HEAD_DIM] = (acc_ref[h] / l_ref[h]).T.astype(BF16)


def forgetting_attention(qk, vt, fk, fq, *, batch, tq, lag):
    m = qk.shape[0]
    s = m // batch
    nq = s // tq
    q_spec, k_spec, vt_spec, o_spec = _group_specs(s, tq, FOX_W, QC_COL, VC_ROW)
    return pl.pallas_call(
        functools.partial(_fox_attn_kernel, tq=tq, lag=lag),
        grid=(batch, nq),
        in_specs=[pl.BlockSpec((tq, LANES), lambda b, i: (b * nq + i, 0)),
                  pl.BlockSpec((s, LANES), lambda b, i: (b, 0)),
                  q_spec, k_spec, vt_spec],
        out_specs=o_spec,
        out_shape=jax.ShapeDtypeStruct((m, FOX_W), BF16),
        scratch_shapes=[pltpu.VMEM((N_FOX, tq, 2 * HEAD_DIM), BF16)] + _softmax_scratch(N_FOX, tq, tq),
        compiler_params=_params("parallel", "arbitrary"),
        name="forgetting_attention",
    )(fq, fk, qk, qk, vt)


def _residual_tail(y, x, g_post, g_next, x_out_ref, h_out_ref):
    x_new = x + _rms(y, g_post)
    x_out_ref[...] = x_new
    if h_out_ref is not None:
        h_out_ref[...] = _rms(x_new, g_next).astype(BF16)


def _mix_out_kernel(oa_ref, ob_ref, oc_ref, x_ref, w_ref, gb_ref, gc_ref, gp_ref, gn_ref, xo_ref, ho_ref):
    ob = _rms(ob_ref[...].astype(F32), gb_ref[...]).astype(BF16)
    oc = _rms(oc_ref[...].astype(F32), gc_ref[...]).astype(BF16)
    y = (_dot(oa_ref[...], w_ref[0:A_W, :]) + _dot(ob, w_ref[A_W:A_W + SB_W, :])
         + _dot(oc, w_ref[A_W + SB_W:, :]))
    _residual_tail(y, x_ref[...], gp_ref[...], gn_ref[...], xo_ref, ho_ref)


def mix_out(oa, ob, oc, x, w, g_sb, g_fox, g_post, g_next, *, bm):
    m, d = x.shape
    rows = lambda width: pl.BlockSpec((bm, width), lambda i: (i, 0))
    const = lambda width: pl.BlockSpec((1, width), lambda i: (0, 0))
    return pl.pallas_call(
        _mix_out_kernel,
        grid=(m // bm,),
        in_specs=[rows(A_W), rows(SB_W), rows(FOX_W), rows(d),
                  pl.BlockSpec(w.shape, lambda i: (0, 0)),
                  const(SB_W), const(FOX_W), const(d), const(d)],
        out_specs=[rows(d), rows(d)],
        out_shape=[jax.ShapeDtypeStruct((m, d), F32), jax.ShapeDtypeStruct((m, d), BF16)],
        compiler_params=_params("parallel"),
        name="mix_out",
    )(oa, ob, oc, x, w, g_sb.reshape(1, -1), g_fox.reshape(1, -1), g_post.reshape(1, -1), g_next.reshape(1, -1))


def _norm_matmul_kernel(x_ref, g_ref, w_ref, o_ref):
    o_ref[...] = _dot(_rms(x_ref[...], g_ref[...]).astype(BF16), w_ref[...]).astype(BF16)


def norm_matmul(x, g, w, *, bm):
    m, d = x.shape
    n = w.shape[1]
    return pl.pallas_call(
        _norm_matmul_kernel,
        grid=(m // bm,),
        in_specs=[pl.BlockSpec((bm, d), lambda i: (i, 0)), pl.BlockSpec((1, d), lambda i: (0, 0)),
                  pl.BlockSpec((d, n), lambda i: (0, 0))],
        out_specs=pl.BlockSpec((bm, n), lambda i: (i, 0)),
        out_shape=jax.ShapeDtypeStruct((m, n), BF16),
        compiler_params=_params("parallel"),
        name="memory_kv",
    )(x, g.reshape(1, d), w)


def _cross_attn_kernel(h_ref, x_ref, kv_ref, wq_ref, wo_ref, gp_ref, gn_ref, xo_ref, ho_ref):
    q = (_dot(h_ref[...], wq_ref[...]) * HEAD_DIM ** -0.5).astype(BF16)
    heads = []
    for hd in range(N_CROSS_HEADS):
        k = kv_ref[:, hd * HEAD_DIM:(hd + 1) * HEAD_DIM]
        v = kv_ref[:, CROSS_W + hd * HEAD_DIM:CROSS_W + (hd + 1) * HEAD_DIM]
        s = _dot_nt(q[:, hd * HEAD_DIM:(hd + 1) * HEAD_DIM], k)
        p = jnp.exp(s - jnp.max(s, axis=1, keepdims=True))
        o = _dot(p.astype(BF16), v) / jnp.sum(p, axis=1, keepdims=True)
        heads.append(o.astype(BF16))
    y = _dot(jnp.concatenate(heads, axis=1), wo_ref[...])
    _residual_tail(y, x_ref[...], gp_ref[...], gn_ref[...], xo_ref, ho_ref)


def cross_attention(h, x, kv, wq, wo, g_post, g_next, *, batch, bm):
    m, d = x.shape
    s = m // batch
    mem_len = kv.shape[0] // batch
    nb = s // bm
    rows = lambda width: pl.BlockSpec((bm, width), lambda b, i: (b * nb + i, 0))
    const = lambda shape: pl.BlockSpec(shape, lambda b, i: (0, 0))
    return pl.pallas_call(
        _cross_attn_kernel,
        grid=(batch, nb),
        in_specs=[rows(d), rows(d),
                  pl.BlockSpec((mem_len, 2 * CROSS_W), lambda b, i: (b, 0)),
                  const(wq.shape), const(wo.shape), const((1, d)), const((1, d))],
        out_specs=[rows(d), rows(d)],
        out_shape=[jax.ShapeDtypeStruct((m, d), F32), jax.ShapeDtypeStruct((m, d), BF16)],
        compiler_params=_params("parallel", "parallel"),
        name="cross_attention",
    )(h, x, kv, wq, wo, g_post.reshape(1, d), g_next.reshape(1, d))


def _ffn_up_kernel(h_ref, wg_ref, wu_ref, o_ref):
    h = h_ref[...]
    gate = _dot(h, wg_ref[...])
    up = _dot(h, wu_ref[...])
    o_ref[...] = (gate * jax.nn.sigmoid(gate) * up).astype(BF16)


def ffn_up(h, wg, wu, *, bm, bn):
    m, d = h.shape
    n = wg.shape[1]
    wspec = pl.BlockSpec((d, bn), lambda i, j: (0, j))
    return pl.pallas_call(
        _ffn_up_kernel,
        grid=(m // bm, n // bn),
        in_specs=[pl.BlockSpec((bm, d), lambda i, j: (i, 0)), wspec, wspec],
        out_specs=pl.BlockSpec((bm, bn), lambda i, j: (i, j)),
        out_shape=jax.ShapeDtypeStruct((m, n), BF16),
        compiler_params=_params("parallel", "arbitrary"),
        name="ffn_up",
    )(h, wg, wu)


def _ffn_down_kernel(*refs, emit_h):
    if emit_h:
        a_ref, w_ref, x_ref, gp_ref, gn_ref, xo_ref, ho_ref, acc_ref = refs
    else:
        a_ref, w_ref, x_ref, gp_ref, xo_ref, acc_ref = refs
        gn_ref = ho_ref = None
    kk = pl.program_id(1)

    @pl.when(kk == 0)
    def _():
        acc_ref[...] = jnp.zeros(acc_ref.shape, F32)

    acc_ref[...] += _dot(a_ref[...], w_ref[...])

    @pl.when(kk == pl.num_programs(1) - 1)
    def _():
        g_next = gn_ref[...] if emit_h else None
        _residual_tail(acc_ref[...], x_ref[...], gp_ref[...], g_next, xo_ref, ho_ref)


def ffn_down(a, w, x, g_post, g_next, *, bm, bk):
    m, d = x.shape
    kdim = a.shape[1]
    emit_h = g_next is not None
    rows = pl.BlockSpec((bm, d), lambda i, k: (i, 0))
    const = pl.BlockSpec((1, d), lambda i, k: (0, 0))
    gains = [g_post.reshape(1, d)] + ([g_next.reshape(1, d)] if emit_h else [])
    out_shape = [jax.ShapeDtypeStruct((m, d), F32)] + ([jax.ShapeDtypeStruct((m, d), BF16)] if emit_h else [])
    return pl.pallas_call(
        functools.partial(_ffn_down_kernel, emit_h=emit_h),
        grid=(m // bm, kdim // bk),
        in_specs=[pl.BlockSpec((bm, bk), lambda i, k: (i, k)),
                  pl.BlockSpec((bk, d), lambda i, k: (k, 0)),
                  rows] + [const] * len(gains),
        out_specs=[rows] * len(out_shape),
        out_shape=out_shape,
        scratch_shapes=[pltpu.VMEM((bm, d), F32)],
        compiler_params=_params("parallel", "arbitrary"),
        name="ffn_down",
    )(a, w, x, *gains)


def _pick(n, *candidates):
    for c in candidates:
        if n % c == 0:
            return c
    return n


def _columns(w, names):
    return jnp.concatenate([w[..., _IN_OFFSETS[n][0]:_IN_OFFSETS[n][1]] for n in names], axis=-1)


def kernel(x, mem, positions, g_mix_pre, g_mix_post, w_in, b_f, lam_q1, lam_k1, lam_q2, lam_k2, g_diff_sub, g_sb_out, g_fox_out, w_out, g_x_pre, g_x_post, g_mem, w_cq, w_ckv, w_co, g_ffn_pre, g_ffn_post, w_gate, w_up, w_down):
    batch, seq, d = x.shape
    depth = w_in.shape[0]
    m = batch * seq
    d_ff = w_gate.shape[2]
    assert w_in.shape[2] == _IN_OFFSETS["fc"][1] and d == V_ROWS

    bm_row = _pick(m, 512, 256, 128)
    bm_mm = _pick(m, 1024, 512, 256, 128)
    bn_in = 2 * A_W
    bn_ff = _pick(d_ff, 512, 256, 128)
    tq = _pick(seq, 256, 128)
    bm_x = _pick(seq, 512, 256, 128)
    bm_mem = _pick(mem.shape[0] * mem.shape[1], 512, 256, 128)

    w_qk = _columns(w_in, QK_ORDER).astype(BF16)
    w_vt = jnp.swapaxes(_columns(w_in, V_ORDER), 1, 2).astype(BF16)
    gate_lane_head = jnp.arange(2 * GATE_LANES) % GATE_LANES // GATE_PARTS
    wf_rep = jnp.pad(_columns(w_in, ("fc",))[:, :, gate_lane_head],
                     ((0, 0), (0, 0), (0, LANES - 2 * GATE_LANES))).astype(BF16)
    bf_rep = jnp.pad(b_f[:, gate_lane_head], ((0, 0), (0, LANES - 2 * GATE_LANES))).reshape(depth, 1, LANES)
    w_out_b, w_cq_b, w_ckv_b, w_co_b = (w.astype(BF16) for w in (w_out, w_cq, w_ckv, w_co))
    w_gate_b, w_up_b, w_down_b = (w.astype(BF16) for w in (w_gate, w_up, w_down))
    lam_q = jnp.concatenate([lam_q1, lam_q2], axis=1).reshape(depth, 1, LANES)
    lam_k = jnp.concatenate([lam_k1, lam_k2], axis=1).reshape(depth, 1, LANES)

    colscale = jnp.concatenate([
        jnp.full((w,), s, F32) for w, s in ((SB_W, HEAD_DIM ** -0.5), (SB_W, 1.0), (FOX_W, HEAD_DIM ** -0.5),
                                            (FOX_W, 1.0), (A_W, DIFF_QK_DIM ** -0.5), (A_W, 1.0))]).reshape(1, QK_COLS)

    x = x.reshape(m, d)
    mem2 = mem.reshape(-1, d)
    cos_t, sin_t = rope_tables(positions, bm=bm_row)
    h = prenorm(x, g_mix_pre[0], bm=bm_row)

    for l in range(depth):
        lam_init = 0.8 - 0.6 * math.exp(-0.3 * l)
        qk = in_proj_qk(h, w_qk[l], colscale, cos_t, sin_t, bm=bm_mm, bn=bn_in)
        vt = in_proj_vt(h, w_vt[l], bm=bm_mm, bn=bn_in, tk=tq)
        fk, fq = forget_features(h, wf_rep[l], bf_rep[l], batch=batch, blk=tq)
        oa = diff_attention(qk, vt, lam_q[l], lam_k[l], g_diff_sub[l].reshape(1, LANES), lam_init, batch=batch, tq=tq, lag=2)
        ob = stick_breaking_attention(qk, vt, batch=batch, tq=tq, lag=1)
        oc = forgetting_attention(qk, vt, fk, fq, batch=batch, tq=tq, lag=2)
        x, h = mix_out(oa, ob, oc, x, w_out_b[l], g_sb_out[l], g_fox_out[l], g_mix_post[l], g_x_pre[l], bm=bm_row // 2)
        kv = norm_matmul(mem2, g_mem[l], w_ckv_b[l], bm=bm_mem)
        x, h = cross_attention(h, x, kv, w_cq_b[l], w_co_b[l], g_x_post[l], g_ffn_pre[l], batch=batch, bm=bm_x)
        a = ffn_up(h, w_gate_b[l], w_up_b[l], bm=bm_mm, bn=bn_ff)
        g_next = g_mix_pre[l + 1] if l + 1 < depth else None
        x, *rest = ffn_down(a, w_down_b[l], x, g_ffn_post[l], g_next, bm=bm_row, bk=bn_ff)
        h = rest[0] if rest else None
    return x.reshape(batch, seq, d)
```

```python
import functools
import math

import jax
import jax.numpy as jnp
from jax import lax
from jax.experimental import pallas as pl
from jax.experimental.pallas import tpu as pltpu

F32 = jnp.float32
BF16 = jnp.bfloat16

EPS = 1e-6
NEG_INF = -1e30
ROPE_THETA = 10000.0
LOG2E = math.log2(math.e)

LANES = 128
HEAD_DIM = 128
N_DIFF, N_SB, N_FOX = 4, 6, 6
DIFF_QK_DIM = HEAD_DIM // 2
A_W, SB_W, FOX_W = N_DIFF * HEAD_DIM, N_SB * HEAD_DIM, N_FOX * HEAD_DIM
QK_COLS = 2 * (A_W + SB_W + FOX_W)
V_ROWS = A_W + SB_W + FOX_W
N_CROSS_HEADS = 4
CROSS_W = N_CROSS_HEADS * HEAD_DIM
GATE_PARTS = 3
GATE_LANES = GATE_PARTS * N_FOX

VMEM_LIMIT = 56 * 1024 * 1024

_IN_OFFSETS = {}
_off = 0
for _name, _width in (("qa", A_W), ("ka", A_W), ("va", A_W), ("qb", SB_W), ("kb", SB_W), ("vb", SB_W),
                      ("qc", FOX_W), ("kc", FOX_W), ("vc", FOX_W), ("fc", N_FOX)):
    _IN_OFFSETS[_name] = (_off, _off + _width)
    _off += _width
QK_ORDER = ("qb", "kb", "qc", "kc", "qa", "ka")
V_ORDER = ("vb", "vc", "va")
QB_COL, QC_COL, QA_COL = 0, 2, (4 * SB_W) // A_W
VB_ROW, VC_ROW, VA_ROW = 0, 1, (2 * SB_W) // A_W


def _params(*semantics):
    return pltpu.CompilerParams(dimension_semantics=semantics, vmem_limit_bytes=VMEM_LIMIT)


def _rms(xf, g):
    return xf * lax.rsqrt(jnp.mean(xf * xf, axis=-1, keepdims=True) + EPS) * g


def _dot(a, b):
    return jnp.dot(a, b, preferred_element_type=F32)


def _dot_nt(a, b):
    return lax.dot_general(a, b, (((1,), (1,)), ((), ())), preferred_element_type=F32)


def _bf16_parts(x):
    p0 = x.astype(BF16)
    r1 = x - p0.astype(F32)
    p1 = r1.astype(BF16)
    p2 = (r1 - p1.astype(F32)).astype(BF16)
    return p0, p1, p2


def _prenorm_kernel(x_ref, g_ref, h_ref):
    h_ref[...] = _rms(x_ref[...], g_ref[...]).astype(BF16)


def prenorm(x, g, *, bm):
    m, d = x.shape
    return pl.pallas_call(
        _prenorm_kernel,
        grid=(m // bm,),
        in_specs=[pl.BlockSpec((bm, d), lambda i: (i, 0)), pl.BlockSpec((1, d), lambda i: (0, 0))],
        out_specs=pl.BlockSpec((bm, d), lambda i: (i, 0)),
        out_shape=jax.ShapeDtypeStruct((m, d), BF16),
        compiler_params=_params("parallel"),
        name="prenorm",
    )(x, g.reshape(1, d))


def _rope_table_kernel(pos_ref, invf_ref, sign_ref, cos_ref, sin_ref):
    ang = pos_ref[...].astype(F32) * invf_ref[...]
    cos_ref[...] = jnp.cos(ang)
    sin_ref[...] = jnp.sin(ang) * sign_ref[...]


def rope_tables(positions, *, bm):
    m = positions.size
    half = DIFF_QK_DIM // 2
    inv_freq = ROPE_THETA ** (-jnp.arange(half, dtype=F32) / half)
    invf = jnp.tile(inv_freq, LANES // half).reshape(1, LANES)
    sign = jnp.where((jnp.arange(LANES) % DIFF_QK_DIM) < half, -1.0, 1.0).astype(F32).reshape(1, LANES)
    row = pl.BlockSpec((bm, LANES), lambda i: (i, 0))
    const = pl.BlockSpec((1, LANES), lambda i: (0, 0))
    return pl.pallas_call(
        _rope_table_kernel,
        grid=(m // bm,),
        in_specs=[pl.BlockSpec((bm, 1), lambda i: (i, 0)), const, const],
        out_specs=[row, row],
        out_shape=[jax.ShapeDtypeStruct((m, LANES), F32)] * 2,
        compiler_params=_params("parallel"),
        name="rope_tables",
    )(positions.reshape(m, 1), invf, sign)


def _in_proj_qk_kernel(h_ref, w_ref, cs_ref, cos_ref, sin_ref, o_ref, *, first_rope_block):
    j = pl.program_id(1)
    acc = _dot(h_ref[...], w_ref[...]) * cs_ref[...]
    bm, bn = acc.shape

    @pl.when(j >= first_rope_block)
    def _():
        c, s = cos_ref[...], sin_ref[...]
        first_half = (lax.broadcasted_iota(jnp.int32, (bm, LANES), 1) % DIFF_QK_DIM) < DIFF_QK_DIM // 2
        for t in range(bn // LANES):
            a = acc[:, t * LANES:(t + 1) * LANES]
            partner = jnp.where(first_half, pltpu.roll(a, LANES - DIFF_QK_DIM // 2, 1),
                                pltpu.roll(a, DIFF_QK_DIM // 2, 1))
            o_ref[:, t * LANES:(t + 1) * LANES] = (a * c + partner * s).astype(BF16)

    @pl.when(j < first_rope_block)
    def _():
        o_ref[...] = acc.astype(BF16)


def in_proj_qk(h, w, colscale, cos_t, sin_t, *, bm, bn):
    m, d = h.shape
    n = w.shape[1]
    rope_cols = 2 * A_W
    assert rope_cols % bn == 0 and n % bn == 0 and m % bm == 0
    row_tab = pl.BlockSpec((bm, LANES), lambda i, j: (i, 0))
    return pl.pallas_call(
        functools.partial(_in_proj_qk_kernel, first_rope_block=(n - rope_cols) // bn),
        grid=(m // bm, n // bn),
        in_specs=[pl.BlockSpec((bm, d), lambda i, j: (i, 0)),
                  pl.BlockSpec((d, bn), lambda i, j: (0, j)),
                  pl.BlockSpec((1, bn), lambda i, j: (0, j)),
                  row_tab, row_tab],
        out_specs=pl.BlockSpec((bm, bn), lambda i, j: (i, j)),
        out_shape=jax.ShapeDtypeStruct((m, n), BF16),
        compiler_params=_params("parallel", "arbitrary"),
        name="in_proj_qk",
    )(h, w, colscale, cos_t, sin_t)


def _in_proj_vt_kernel(h_ref, wt_ref, o_ref, *, tk):
    res = _dot_nt(wt_ref[...], h_ref[...]).astype(BF16)
    for c in range(o_ref.shape[0]):
        o_ref[c] = res[:, c * tk:(c + 1) * tk]


def in_proj_vt(h, wt, *, bm, bn, tk):
    m, d = h.shape
    n = wt.shape[0]
    return pl.pallas_call(
        functools.partial(_in_proj_vt_kernel, tk=tk),
        grid=(m // bm, n // bn),
        in_specs=[pl.BlockSpec((bm, d), lambda i, j: (i, 0)),
                  pl.BlockSpec((bn, d), lambda i, j: (j, 0))],
        out_specs=pl.BlockSpec((bm // tk, bn, tk), lambda i, j: (i, j, 0)),
        out_shape=jax.ShapeDtypeStruct((m // tk, n, tk), BF16),
        compiler_params=_params("parallel", "arbitrary"),
        name="in_proj_vt",
    )(h, wt)


def _forget_features_kernel(h_ref, wf_ref, bf_ref, part_ref, fk_ref, fq_ref, *, blk):
    s = h_ref.shape[0]
    lane = lax.broadcasted_iota(jnp.int32, (blk, LANES), 1)
    part = jnp.broadcast_to(part_ref[...], (blk, LANES))
    lower = (lax.broadcasted_iota(jnp.int32, (blk, blk), 0)
             >= lax.broadcasted_iota(jnp.int32, (blk, blk), 1)).astype(BF16)
    one = jnp.ones((blk, LANES), F32)
    zero = jnp.zeros((blk, LANES), F32)
    carry = jnp.zeros((1, LANES), F32)
    for t in range(s // blk):
        rows = slice(t * blk, (t + 1) * blk)
        fc = _dot(h_ref[rows, :], wf_ref[...]) + bf_ref[...]
        log_f = (jnp.minimum(fc, 0.0) - jnp.log1p(jnp.exp(-jnp.abs(fc)))) * LOG2E
        x0, x1, x2 = _bf16_parts(log_f)
        cum = _dot(lower, x0) + _dot(lower, x1) + _dot(lower, x2) + carry
        carry = cum[blk - 1:blk, :]
        c0, c1, c2 = (c.astype(F32) for c in _bf16_parts(cum))
        parts = jnp.where(part == 0, c0, jnp.where(part == 1, c1, c2))
        fk = jnp.where(lane < GATE_LANES, parts, jnp.where(lane < 2 * GATE_LANES, one, zero))
        fq = jnp.where(lane < GATE_LANES, -one, jnp.where(lane < 2 * GATE_LANES, parts, zero))
        fk_ref[rows, :] = fk.astype(BF16)
        fq_ref[rows, :] = fq.astype(BF16)


def forget_features(h, wf_rep, bf_rep, *, batch, blk):
    m, d = h.shape
    s = m // batch
    part = (jnp.arange(LANES, dtype=jnp.int32) % GATE_PARTS).reshape(1, LANES)
    rows = pl.BlockSpec((s, LANES), lambda b: (b, 0))
    vec = pl.BlockSpec((1, LANES), lambda b: (0, 0))
    return pl.pallas_call(
        functools.partial(_forget_features_kernel, blk=blk),
        grid=(batch,),
        in_specs=[pl.BlockSpec((s, d), lambda b: (b, 0)),
                  pl.BlockSpec((d, LANES), lambda b: (0, 0)), vec, vec],
        out_specs=[rows, rows],
        out_shape=[jax.ShapeDtypeStruct((m, LANES), BF16)] * 2,
        compiler_params=_params("parallel"),
        name="forget_features",
    )(h, wf_rep, bf_rep, part)


def _head(ref, h):
    return ref[:, h * HEAD_DIM:(h + 1) * HEAD_DIM]


def _key_block(ref, kb, tk, h):
    return ref[pl.ds(pl.multiple_of(kb * tk, tk), tk), h * HEAD_DIM:(h + 1) * HEAD_DIM]


def _value_block_t(ref, kb, h):
    return ref[kb, h * HEAD_DIM:(h + 1) * HEAD_DIM, :]


def _run_pipeline(i, n_heads, stages, lag):
    n = len(stages)
    lead = [(n - 1 - k) * lag for k in range(n)]
    assert lead[0] <= n_heads

    def positions(cur, nxt, cur_masked, first, last):
        for t in range(first, last):
            for k in range(n):
                idx = t + lead[k]
                if 0 <= idx < n_heads:
                    stages[k](cur, idx, cur_masked)
                elif idx >= n_heads and nxt is not None:
                    stages[k](nxt, idx - n_heads, False)

    positions(i, None, True, -lead[0], 0)
    positions(i, jnp.maximum(i - 1, 0), True, 0, n_heads)

    def body(j, carry):
        cur = i - j
        positions(cur, jnp.maximum(cur - 1, 0), False, 0, n_heads)
        return carry

    lax.fori_loop(1, i + 1, body, 0)


def _softmax_stages(scores, values_t, keep, m_ref, l_ref, acc_ref, s_buf, mn_buf, al_buf, ps_buf, pv_buf):
    def stage_scores(kb, h, masked):
        s_buf[h] = scores(kb, h)

    def stage_probabilities(kb, h, masked):
        s = s_buf[h]
        if masked:
            s = jnp.where(keep, s, NEG_INF)
        m_prev = m_ref[h]
        m_new = jnp.maximum(m_prev, jnp.max(s, axis=0, keepdims=True))
        p = jnp.exp2(s - m_new)
        mn_buf[h] = m_new
        al_buf[h] = jnp.exp2(m_prev - m_new)
        ps_buf[h] = jnp.sum(p, axis=0, keepdims=True)
        pv_buf[h] = _dot(values_t(kb, h), p.astype(BF16))

    def stage_commit(kb, h, masked):
        alpha = al_buf[h]
        m_ref[h] = mn_buf[h]
        l_ref[h] = alpha * l_ref[h] + ps_buf[h]
        acc_ref[h] = alpha * acc_ref[h] + pv_buf[h]

    return stage_scores, stage_probabilities, stage_commit


def _softmax_scratch(n_heads, tk, nq):
    stat = pltpu.VMEM((n_heads, 1, nq), F32)
    wide = pltpu.VMEM((n_heads, HEAD_DIM, nq), F32)
    return [stat, stat, wide,
            pltpu.VMEM((n_heads, tk, nq), F32), stat, stat, stat, wide]


def _init_softmax_stats(m_ref, l_ref, acc_ref):
    m_ref[...] = jnp.full(m_ref.shape, NEG_INF, F32)
    l_ref[...] = jnp.zeros(l_ref.shape, F32)
    acc_ref[...] = jnp.zeros(acc_ref.shape, F32)


def _group_specs(s, tq, width, q_col, v_row):
    nq = s // tq
    q_spec = pl.BlockSpec((tq, width), lambda b, i: (b * nq + i, q_col))
    k_spec = pl.BlockSpec((s, width), lambda b, i: (b, q_col + 1))
    vt_spec = pl.BlockSpec((nq, width, tq), lambda b, i: (b, v_row, 0))
    o_spec = pl.BlockSpec((tq, width), lambda b, i: (b * nq + i, 0))
    return q_spec, k_spec, vt_spec, o_spec


def _diff_attn_kernel(lq_ref, lk_ref, g_ref, q_ref, k_ref, vt_ref, o_ref, qq_ref, m_ref, l_ref, acc_ref,
                      *bufs, tq, lam_init, lag):
    i = pl.program_id(1)
    low = lax.broadcasted_iota(jnp.int32, (HEAD_DIM, tq), 0) < DIFF_QK_DIM
    for h in range(N_DIFF):
        q_t = _head(q_ref, h).astype(F32).T
        qq_ref[h, :, 0:tq] = jnp.where(low, q_t, 0.0).astype(BF16)
        qq_ref[h, :, tq:2 * tq] = jnp.where(low, 0.0, q_t).astype(BF16)
    _init_softmax_stats(m_ref, l_ref, acc_ref)

    key = lax.broadcasted_iota(jnp.int32, (tq, 2 * tq), 0)
    qry = lax.broadcasted_iota(jnp.int32, (tq, 2 * tq), 1)
    keep = jnp.where(qry >= tq, qry - tq, qry) >= key

    def scores(kb, h):
        return _dot(_key_block(k_ref, kb, tq, h), qq_ref[h])

    stages = _softmax_stages(scores, lambda kb, h: _value_block_t(vt_ref, kb, h), keep,
                             m_ref, l_ref, acc_ref, *bufs)
    _run_pipeline(i, N_DIFF, stages, lag)

    prod = lq_ref[...] * lk_ref[...]
    first = lax.broadcasted_iota(jnp.int32, prod.shape, 1) < DIFF_QK_DIM
    e1 = jnp.exp(jnp.sum(jnp.where(first, prod, 0.0), axis=1, keepdims=True))
    e2 = jnp.exp(jnp.sum(jnp.where(first, 0.0, prod), axis=1, keepdims=True))
    lam = (e1 - e2) + lam_init
    for h in range(N_DIFF):
        o_t = (acc_ref[h, :, 0:tq] / l_ref[h, :, 0:tq]
               - lam * (acc_ref[h, :, tq:2 * tq] / l_ref[h, :, tq:2 * tq]))
        o_ref[:, h * HEAD_DIM:(h + 1) * HEAD_DIM] = (_rms(o_t.T, g_ref[...]) * (1.0 - lam_init)).astype(BF16)


def diff_attention(qk, vt, lam_q, lam_k, g_sub, lam_init, *, batch, tq, lag):
    m = qk.shape[0]
    s = m // batch
    vec = pl.BlockSpec((1, LANES), lambda b, i: (0, 0))
    q_spec, k_spec, vt_spec, o_spec = _group_specs(s, tq, A_W, QA_COL, VA_ROW)
    return pl.pallas_call(
        functools.partial(_diff_attn_kernel, tq=tq, lam_init=lam_init, lag=lag),
        grid=(batch, s // tq),
        in_specs=[vec, vec, vec, q_spec, k_spec, vt_spec],
        out_specs=o_spec,
        out_shape=jax.ShapeDtypeStruct((m, A_W), BF16),
        scratch_shapes=[pltpu.VMEM((N_DIFF, HEAD_DIM, 2 * tq), BF16)] + _softmax_scratch(N_DIFF, tq, 2 * tq),
        compiler_params=_params("parallel", "arbitrary"),
        name="diff_attention",
    )(lam_q, lam_k, g_sub, qk, qk, vt)


def _sb_attn_kernel(q_ref, k_ref, vt_ref, o_ref, qn_ref, r_ref, acc_ref,
                    zn_buf, lb_buf, after_buf, cs_buf, pv_buf, *, tq, lag):
    i = pl.program_id(1)
    for h in range(N_SB):
        qn_ref[h] = (-_head(q_ref, h).astype(F32)).T.astype(BF16)
    r_ref[...] = jnp.zeros(r_ref.shape, F32)
    acc_ref[...] = jnp.zeros(acc_ref.shape, F32)
    ss = lax.broadcasted_iota(jnp.int32, (tq, 2 * tq), 0)
    jj = lax.broadcasted_iota(jnp.int32, (tq, 2 * tq), 1)
    tri = (jnp.where(jj >= tq, jj - tq, jj) > ss).astype(BF16)

    strict = lax.broadcasted_iota(jnp.int32, (tq, tq), 0) < lax.broadcasted_iota(jnp.int32, (tq, tq), 1)

    def scores(kb, h, masked):
        zn_buf[h] = _dot(_key_block(k_ref, kb, tq, h), qn_ref[h])

    def suffix_sums(kb, h, masked):
        zn = zn_buf[h]
        sp = jnp.log(1.0 + jnp.exp(-jnp.abs(zn)))
        log_remain = jnp.minimum(zn, 0.0) - sp
        lb_buf[h] = log_remain - zn
        if masked:
            log_remain = jnp.where(strict, log_remain, 0.0)
        hi = log_remain.astype(BF16)
        lo = (log_remain - hi.astype(F32)).astype(BF16)
        after_buf[h] = _dot(tri, jnp.concatenate([hi, lo], axis=0))
        cs_buf[h] = jnp.sum(log_remain, axis=0, keepdims=True)

    def weights(kb, h, masked):
        w = jnp.exp(lb_buf[h] + (after_buf[h] + r_ref[h]))
        if masked:
            w = jnp.where(strict, w, 0.0)
        pv_buf[h] = _dot(_value_block_t(vt_ref, kb, h), w.astype(BF16))
        r_ref[h] += cs_buf[h]

    def accumulate(kb, h, masked):
        acc_ref[h] += pv_buf[h]

    _run_pipeline(i, N_SB, (scores, suffix_sums, weights, accumulate), lag)
    for h in range(N_SB):
        o_ref[:, h * HEAD_DIM:(h + 1) * HEAD_DIM] = acc_ref[h].T.astype(BF16)


def stick_breaking_attention(qk, vt, *, batch, tq, lag):
    m = qk.shape[0]
    s = m // batch
    q_spec, k_spec, vt_spec, o_spec = _group_specs(s, tq, SB_W, QB_COL, VB_ROW)
    stat = pltpu.VMEM((N_SB, 1, tq), F32)
    wide = pltpu.VMEM((N_SB, HEAD_DIM, tq), F32)
    tile = pltpu.VMEM((N_SB, tq, tq), F32)
    return pl.pallas_call(
        functools.partial(_sb_attn_kernel, tq=tq, lag=lag),
        grid=(batch, s // tq),
        in_specs=[q_spec, k_spec, vt_spec],
        out_specs=o_spec,
        out_shape=jax.ShapeDtypeStruct((m, SB_W), BF16),
        scratch_shapes=[pltpu.VMEM((N_SB, HEAD_DIM, tq), BF16), stat, wide,
                        tile, tile, tile, stat, wide],
        compiler_params=_params("parallel", "arbitrary"),
        name="stick_breaking_attention",
    )(qk, qk, vt)


def _fox_attn_kernel(fq_ref, fk_ref, q_ref, k_ref, vt_ref, o_ref, qa_ref, m_ref, l_ref, acc_ref, *bufs, tq, lag):
    i = pl.program_id(1)
    feat = lax.broadcasted_iota(jnp.int32, (LANES, tq), 0)
    fq_t = fq_ref[...].astype(F32).T
    for h in range(N_FOX):
        lo = GATE_PARTS * h
        mine = (((feat >= lo) & (feat < lo + GATE_PARTS))
                | ((feat >= GATE_LANES + lo) & (feat < GATE_LANES + lo + GATE_PARTS)))
        qa_ref[h, 0:HEAD_DIM, :] = _head(q_ref, h).astype(F32).T.astype(BF16)
        qa_ref[h, HEAD_DIM:2 * HEAD_DIM, :] = jnp.where(mine, fq_t, 0.0).astype(BF16)
    _init_softmax_stats(m_ref, l_ref, acc_ref)

    keep = lax.broadcasted_iota(jnp.int32, (tq, tq), 1) >= lax.broadcasted_iota(jnp.int32, (tq, tq), 0)

    def scores(kb, h):
        fk = fk_ref[pl.ds(pl.multiple_of(kb * tq, tq), tq), :]
        return _dot(jnp.concatenate([_key_block(k_ref, kb, tq, h), fk], axis=1), qa_ref[h])

    stages = _softmax_stages(scores, lambda kb, h: _value_block_t(vt_ref, kb, h), keep,
                             m_ref, l_ref, acc_ref, *bufs)
    _run_pipeline(i, N_FOX, stages, lag)
    for h in range(N_FOX):
        o_ref[:, h * HEAD_DIM:(h + 1) * HEAD_DIM] = (acc_ref[h] / l_ref[h]).T.astype(BF16)


def forgetting_attention(qk, vt, fk, fq, *, batch, tq, lag):
    m = qk.shape[0]
    s = m // batch
    nq = s // tq
    q_spec, k_spec, vt_spec, o_spec = _group_specs(s, tq, FOX_W, QC_COL, VC_ROW)
    return pl.pallas_call(
        functools.partial(_fox_attn_kernel, tq=tq, lag=lag),
        grid=(batch, nq),
        in_specs=[pl.BlockSpec((tq, LANES), lambda b, i: (b * nq + i, 0)),
                  pl.BlockSpec((s, LANES), lambda b, i: (b, 0)),
                  q_spec, k_spec, vt_spec],
        out_specs=o_spec,
        out_shape=jax.ShapeDtypeStruct((m, FOX_W), BF16),
        scratch_shapes=[pltpu.VMEM((N_FOX, 2 * HEAD_DIM, tq), BF16)] + _softmax_scratch(N_FOX, tq, tq),
        compiler_params=_params("parallel", "arbitrary"),
        name="forgetting_attention",
    )(fq, fk, qk, qk, vt)


def _residual_tail(y, x, g_post, g_next, x_out_ref, h_out_ref):
    x_new = x + _rms(y, g_post)
    x_out_ref[...] = x_new
    if h_out_ref is not None:
        h_out_ref[...] = _rms(x_new, g_next).astype(BF16)


def _mix_out_kernel(oa_ref, ob_ref, oc_ref, x_ref, w_ref, gb_ref, gc_ref, gp_ref, gn_ref, xo_ref, ho_ref):
    ob = _rms(ob_ref[...].astype(F32), gb_ref[...]).astype(BF16)
    oc = _rms(oc_ref[...].astype(F32), gc_ref[...]).astype(BF16)
    y = (_dot(oa_ref[...], w_ref[0:A_W, :]) + _dot(ob, w_ref[A_W:A_W + SB_W, :])
         + _dot(oc, w_ref[A_W + SB_W:, :]))
    _residual_tail(y, x_ref[...], gp_ref[...], gn_ref[...], xo_ref, ho_ref)


def mix_out(oa, ob, oc, x, w, g_sb, g_fox, g_post, g_next, *, bm):
    m, d = x.shape
    rows = lambda width: pl.BlockSpec((bm, width), lambda i: (i, 0))
    const = lambda width: pl.BlockSpec((1, width), lambda i: (0, 0))
    return pl.pallas_call(
        _mix_out_kernel,
        grid=(m // bm,),
        in_specs=[rows(A_W), rows(SB_W), rows(FOX_W), rows(d),
                  pl.BlockSpec(w.shape, lambda i: (0, 0)),
                  const(SB_W), const(FOX_W), const(d), const(d)],
        out_specs=[rows(d), rows(d)],
        out_shape=[jax.ShapeDtypeStruct((m, d), F32), jax.ShapeDtypeStruct((m, d), BF16)],
        compiler_params=_params("parallel"),
        name="mix_out",
    )(oa, ob, oc, x, w, g_sb.reshape(1, -1), g_fox.reshape(1, -1), g_post.reshape(1, -1), g_next.reshape(1, -1))


def _norm_matmul_kernel(x_ref, g_ref, w_ref, o_ref):
    o_ref[...] = _dot(_rms(x_ref[...], g_ref[...]).astype(BF16), w_ref[...]).astype(BF16)


def norm_matmul(x, g, w, *, bm):
    m, d = x.shape
    n = w.shape[1]
    return pl.pallas_call(
        _norm_matmul_kernel,
        grid=(m // bm,),
        in_specs=[pl.BlockSpec((bm, d), lambda i: (i, 0)), pl.BlockSpec((1, d), lambda i: (0, 0)),
                  pl.BlockSpec((d, n), lambda i: (0, 0))],
        out_specs=pl.BlockSpec((bm, n), lambda i: (i, 0)),
        out_shape=jax.ShapeDtypeStruct((m, n), BF16),
        compiler_params=_params("parallel"),
        name="memory_kv",
    )(x, g.reshape(1, d), w)


def _cross_attn_kernel(h_ref, x_ref, kv_ref, wq_ref, wo_ref, gp_ref, gn_ref, xo_ref, ho_ref):
    q = (_dot(h_ref[...], wq_ref[...]) * HEAD_DIM ** -0.5).astype(BF16)
    heads = []
    for hd in range(N_CROSS_HEADS):
        k = kv_ref[:, hd * HEAD_DIM:(hd + 1) * HEAD_DIM]
        v = kv_ref[:, CROSS_W + hd * HEAD_DIM:CROSS_W + (hd + 1) * HEAD_DIM]
        s = _dot_nt(q[:, hd * HEAD_DIM:(hd + 1) * HEAD_DIM], k)
        p = jnp.exp(s - jnp.max(s, axis=1, keepdims=True))
        o = _dot(p.astype(BF16), v) / jnp.sum(p, axis=1, keepdims=True)
        heads.append(o.astype(BF16))
    y = _dot(jnp.concatenate(heads, axis=1), wo_ref[...])
    _residual_tail(y, x_ref[...], gp_ref[...], gn_ref[...], xo_ref, ho_ref)


def cross_attention(h, x, kv, wq, wo, g_post, g_next, *, batch, bm):
    m, d = x.shape
    s = m // batch
    mem_len = kv.shape[0] // batch
    nb = s // bm
    rows = lambda width: pl.BlockSpec((bm, width), lambda b, i: (b * nb + i, 0))
    const = lambda shape: pl.BlockSpec(shape, lambda b, i: (0, 0))
    return pl.pallas_call(
        _cross_attn_kernel,
        grid=(batch, nb),
        in_specs=[rows(d), rows(d),
                  pl.BlockSpec((mem_len, 2 * CROSS_W), lambda b, i: (b, 0)),
                  const(wq.shape), const(wo.shape), const((1, d)), const((1, d))],
        out_specs=[rows(d), rows(d)],
        out_shape=[jax.ShapeDtypeStruct((m, d), F32), jax.ShapeDtypeStruct((m, d), BF16)],
        compiler_params=_params("parallel", "parallel"),
        name="cross_attention",
    )(h, x, kv, wq, wo, g_post.reshape(1, d), g_next.reshape(1, d))


def _ffn_up_kernel(h_ref, wg_ref, wu_ref, o_ref):
    h = h_ref[...]
    gate = _dot(h, wg_ref[...])
    up = _dot(h, wu_ref[...])
    o_ref[...] = (gate * jax.nn.sigmoid(gate) * up).astype(BF16)


def ffn_up(h, wg, wu, *, bm, bn):
    m, d = h.shape
    n = wg.shape[1]
    wspec = pl.BlockSpec((d, bn), lambda i, j: (0, j))
    return pl.pallas_call(
        _ffn_up_kernel,
        grid=(m // bm, n // bn),
        in_specs=[pl.BlockSpec((bm, d), lambda i, j: (i, 0)), wspec, wspec],
        out_specs=pl.BlockSpec((bm, bn), lambda i, j: (i, j)),
        out_shape=jax.ShapeDtypeStruct((m, n), BF16),
        compiler_params=_params("parallel", "arbitrary"),
        name="ffn_up",
    )(h, wg, wu)


def _ffn_down_kernel(*refs, emit_h):
    if emit_h:
        a_ref, w_ref, x_ref, gp_ref, gn_ref, xo_ref, ho_ref, acc_ref = refs
    else:
        a_ref, w_ref, x_ref, gp_ref, xo_ref, acc_ref = refs
        gn_ref = ho_ref = None
    kk = pl.program_id(1)

    @pl.when(kk == 0)
    def _():
        acc_ref[...] = jnp.zeros(acc_ref.shape, F32)

    acc_ref[...] += _dot(a_ref[...], w_ref[...])

    @pl.when(kk == pl.num_programs(1) - 1)
    def _():
        g_next = gn_ref[...] if emit_h else None
        _residual_tail(acc_ref[...], x_ref[...], gp_ref[...], g_next, xo_ref, ho_ref)


def ffn_down(a, w, x, g_post, g_next, *, bm, bk):
    m, d = x.shape
    kdim = a.shape[1]
    emit_h = g_next is not None
    rows = pl.BlockSpec((bm, d), lambda i, k: (i, 0))
    const = pl.BlockSpec((1, d), lambda i, k: (0, 0))
    gains = [g_post.reshape(1, d)] + ([g_next.reshape(1, d)] if emit_h else [])
    out_shape = [jax.ShapeDtypeStruct((m, d), F32)] + ([jax.ShapeDtypeStruct((m, d), BF16)] if emit_h else [])
    return pl.pallas_call(
        functools.partial(_ffn_down_kernel, emit_h=emit_h),
        grid=(m // bm, kdim // bk),
        in_specs=[pl.BlockSpec((bm, bk), lambda i, k: (i, k)),
                  pl.BlockSpec((bk, d), lambda i, k: (k, 0)),
                  rows] + [const] * len(gains),
        out_specs=[rows] * len(out_shape),
        out_shape=out_shape,
        scratch_shapes=[pltpu.VMEM((bm, d), F32)],
        compiler_params=_params("parallel", "arbitrary"),
        name="ffn_down",
    )(a, w, x, *gains)


def _pick(n, *candidates):
    for c in candidates:
        if n % c == 0:
            return c
    return n


def _columns(w, names):
    return jnp.concatenate([w[..., _IN_OFFSETS[n][0]:_IN_OFFSETS[n][1]] for n in names], axis=-1)


def kernel(x, mem, positions, g_mix_pre, g_mix_post, w_in, b_f, lam_q1, lam_k1, lam_q2, lam_k2, g_diff_sub, g_sb_out, g_fox_out, w_out, g_x_pre, g_x_post, g_mem, w_cq, w_ckv, w_co, g_ffn_pre, g_ffn_post, w_gate, w_up, w_down):
    batch, seq, d = x.shape
    depth = w_in.shape[0]
    m = batch * seq
    d_ff = w_gate.shape[2]
    assert w_in.shape[2] == _IN_OFFSETS["fc"][1] and d == V_ROWS

    bm_row = _pick(m, 512, 256, 128)
    bm_mm = _pick(m, 1024, 512, 256, 128)
    bn_in = 2 * A_W
    bn_ff = _pick(d_ff, 512, 256, 128)
    bk_ff = _pick(d_ff, 1408, 1024, 512, 256, 128)
    tq = _pick(seq, 256, 128)
    bm_x = _pick(seq, 512, 256, 128)
    bm_mem = _pick(mem.shape[0] * mem.shape[1], 512, 256, 128)

    w_qk = _columns(w_in, QK_ORDER).astype(BF16)
    w_vt = jnp.swapaxes(_columns(w_in, V_ORDER), 1, 2).astype(BF16)
    gate_lane_head = jnp.arange(2 * GATE_LANES) % GATE_LANES // GATE_PARTS
    wf_rep = jnp.pad(_columns(w_in, ("fc",))[:, :, gate_lane_head],
                     ((0, 0), (0, 0), (0, LANES - 2 * GATE_LANES))).astype(BF16)
    bf_rep = jnp.pad(b_f[:, gate_lane_head], ((0, 0), (0, LANES - 2 * GATE_LANES))).reshape(depth, 1, LANES)
    w_out_b, w_cq_b, w_ckv_b, w_co_b = (w.astype(BF16) for w in (w_out, w_cq, w_ckv, w_co))
    w_gate_b, w_up_b, w_down_b = (w.astype(BF16) for w in (w_gate, w_up, w_down))
    lam_q = jnp.concatenate([lam_q1, lam_q2], axis=1).reshape(depth, 1, LANES)
    lam_k = jnp.concatenate([lam_k1, lam_k2], axis=1).reshape(depth, 1, LANES)

    colscale = jnp.concatenate([
        jnp.full((w,), s, F32) for w, s in ((SB_W, HEAD_DIM ** -0.5), (SB_W, 1.0),
                                            (FOX_W, HEAD_DIM ** -0.5 * LOG2E), (FOX_W, 1.0),
                                            (A_W, DIFF_QK_DIM ** -0.5 * LOG2E), (A_W, 1.0))]).reshape(1, QK_COLS)

    x = x.reshape(m, d)
    mem2 = mem.reshape(-1, d)
    cos_t, sin_t = rope_tables(positions, bm=bm_row)
    h = prenorm(x, g_mix_pre[0], bm=bm_row)

    for l in range(depth):
        lam_init = 0.8 - 0.6 * math.exp(-0.3 * l)
        qk = in_proj_qk(h, w_qk[l], colscale, cos_t, sin_t, bm=bm_mm, bn=bn_in)
        vt = in_proj_vt(h, w_vt[l], bm=bm_mm, bn=bn_in, tk=tq)
        fk, fq = forget_features(h, wf_rep[l], bf_rep[l], batch=batch, blk=tq)
        oa = diff_attention(qk, vt, lam_q[l], lam_k[l], g_diff_sub[l].reshape(1, LANES), lam_init, batch=batch, tq=tq, lag=2)
        ob = stick_breaking_attention(qk, vt, batch=batch, tq=tq, lag=1)
        oc = forgetting_attention(qk, vt, fk, fq, batch=batch, tq=tq, lag=2)
        x, h = mix_out(oa, ob, oc, x, w_out_b[l], g_sb_out[l], g_fox_out[l], g_mix_post[l], g_x_pre[l], bm=bm_row // 2)
        kv = norm_matmul(mem2, g_mem[l], w_ckv_b[l], bm=bm_mem)
        x, h = cross_attention(h, x, kv, w_cq_b[l], w_co_b[l], g_x_post[l], g_ffn_pre[l], batch=batch, bm=bm_x)
        a = ffn_up(h, w_gate_b[l], w_up_b[l], bm=bm_mm, bn=bn_ff)
        g_next = g_mix_pre[l + 1] if l + 1 < depth else None
        x, *rest = ffn_down(a, w_down_b[l], x, g_ffn_post[l], g_next, bm=bm_row, bk=bk_ff)
        h = rest[0] if rest else None
    return x.reshape(batch, seq, d)
```

```python
import functools
import math

import jax
import jax.numpy as jnp
from jax import lax
from jax.experimental import pallas as pl
from jax.experimental.pallas import tpu as pltpu

F32 = jnp.float32
BF16 = jnp.bfloat16

EPS = 1e-6
NEG_INF = -1e30
ROPE_THETA = 10000.0
LOG2E = math.log2(math.e)

LANES = 128
HEAD_DIM = 128
N_DIFF, N_SB, N_FOX = 4, 6, 6
DIFF_QK_DIM = HEAD_DIM // 2
A_W, SB_W, FOX_W = N_DIFF * HEAD_DIM, N_SB * HEAD_DIM, N_FOX * HEAD_DIM
QK_COLS = 2 * (A_W + SB_W + FOX_W)
V_ROWS = A_W + SB_W + FOX_W
N_CROSS_HEADS = 4
CROSS_W = N_CROSS_HEADS * HEAD_DIM
GATE_PARTS = 3
GATE_LANES = GATE_PARTS * N_FOX

VMEM_LIMIT = 56 * 1024 * 1024

_IN_OFFSETS = {}
_off = 0
for _name, _width in (("qa", A_W), ("ka", A_W), ("va", A_W), ("qb", SB_W), ("kb", SB_W), ("vb", SB_W),
                      ("qc", FOX_W), ("kc", FOX_W), ("vc", FOX_W), ("fc", N_FOX)):
    _IN_OFFSETS[_name] = (_off, _off + _width)
    _off += _width
QK_ORDER = ("qb", "kb", "qc", "kc", "qa", "ka")
V_ORDER = ("vb", "vc", "va")
QB_COL, QC_COL, QA_COL = 0, 2, (4 * SB_W) // A_W
VB_ROW, VC_ROW, VA_ROW = 0, 1, (2 * SB_W) // A_W


def _params(*semantics):
    return pltpu.CompilerParams(dimension_semantics=semantics, vmem_limit_bytes=VMEM_LIMIT)


def _rms(xf, g):
    return xf * lax.rsqrt(jnp.mean(xf * xf, axis=-1, keepdims=True) + EPS) * g


def _dot(a, b):
    return jnp.dot(a, b, preferred_element_type=F32)


def _dot_nt(a, b):
    return lax.dot_general(a, b, (((1,), (1,)), ((), ())), preferred_element_type=F32)


def _bf16_parts(x):
    p0 = x.astype(BF16)
    r1 = x - p0.astype(F32)
    p1 = r1.astype(BF16)
    p2 = (r1 - p1.astype(F32)).astype(BF16)
    return p0, p1, p2


def _prenorm_kernel(x_ref, g_ref, h_ref):
    h_ref[...] = _rms(x_ref[...], g_ref[...]).astype(BF16)


def prenorm(x, g, *, bm):
    m, d = x.shape
    return pl.pallas_call(
        _prenorm_kernel,
        grid=(m // bm,),
        in_specs=[pl.BlockSpec((bm, d), lambda i: (i, 0)), pl.BlockSpec((1, d), lambda i: (0, 0))],
        out_specs=pl.BlockSpec((bm, d), lambda i: (i, 0)),
        out_shape=jax.ShapeDtypeStruct((m, d), BF16),
        compiler_params=_params("parallel"),
        name="prenorm",
    )(x, g.reshape(1, d))


def _rope_table_kernel(pos_ref, invf_ref, sign_ref, cos_ref, sin_ref):
    ang = pos_ref[...].astype(F32) * invf_ref[...]
    cos_ref[...] = jnp.cos(ang)
    sin_ref[...] = jnp.sin(ang) * sign_ref[...]


def rope_tables(positions, *, bm):
    m = positions.size
    half = DIFF_QK_DIM // 2
    inv_freq = ROPE_THETA ** (-jnp.arange(half, dtype=F32) / half)
    invf = jnp.tile(inv_freq, LANES // half).reshape(1, LANES)
    sign = jnp.where((jnp.arange(LANES) % DIFF_QK_DIM) < half, -1.0, 1.0).astype(F32).reshape(1, LANES)
    row = pl.BlockSpec((bm, LANES), lambda i: (i, 0))
    const = pl.BlockSpec((1, LANES), lambda i: (0, 0))
    return pl.pallas_call(
        _rope_table_kernel,
        grid=(m // bm,),
        in_specs=[pl.BlockSpec((bm, 1), lambda i: (i, 0)), const, const],
        out_specs=[row, row],
        out_shape=[jax.ShapeDtypeStruct((m, LANES), F32)] * 2,
        compiler_params=_params("parallel"),
        name="rope_tables",
    )(positions.reshape(m, 1), invf, sign)


def _in_proj_qk_kernel(h_ref, w_ref, cs_ref, cos_ref, sin_ref, o_ref, *, first_rope_block):
    j = pl.program_id(1)
    acc = _dot(h_ref[...], w_ref[...]) * cs_ref[...]
    bm, bn = acc.shape

    @pl.when(j >= first_rope_block)
    def _():
        c, s = cos_ref[...], sin_ref[...]
        first_half = (lax.broadcasted_iota(jnp.int32, (bm, LANES), 1) % DIFF_QK_DIM) < DIFF_QK_DIM // 2
        for t in range(bn // LANES):
            a = acc[:, t * LANES:(t + 1) * LANES]
            partner = jnp.where(first_half, pltpu.roll(a, LANES - DIFF_QK_DIM // 2, 1),
                                pltpu.roll(a, DIFF_QK_DIM // 2, 1))
            o_ref[:, t * LANES:(t + 1) * LANES] = (a * c + partner * s).astype(BF16)

    @pl.when(j < first_rope_block)
    def _():
        o_ref[...] = acc.astype(BF16)


def in_proj_qk(h, w, colscale, cos_t, sin_t, *, bm, bn):
    m, d = h.shape
    n = w.shape[1]
    rope_cols = 2 * A_W
    assert rope_cols % bn == 0 and n % bn == 0 and m % bm == 0
    row_tab = pl.BlockSpec((bm, LANES), lambda i, j: (i, 0))
    return pl.pallas_call(
        functools.partial(_in_proj_qk_kernel, first_rope_block=(n - rope_cols) // bn),
        grid=(m // bm, n // bn),
        in_specs=[pl.BlockSpec((bm, d), lambda i, j: (i, 0)),
                  pl.BlockSpec((d, bn), lambda i, j: (0, j)),
                  pl.BlockSpec((1, bn), lambda i, j: (0, j)),
                  row_tab, row_tab],
        out_specs=pl.BlockSpec((bm, bn), lambda i, j: (i, j)),
        out_shape=jax.ShapeDtypeStruct((m, n), BF16),
        compiler_params=_params("parallel", "arbitrary"),
        name="in_proj_qk",
    )(h, w, colscale, cos_t, sin_t)


def _in_proj_vt_kernel(h_ref, wt_ref, o_ref, *, tk):
    res = _dot_nt(wt_ref[...], h_ref[...]).astype(BF16)
    for c in range(o_ref.shape[0]):
        o_ref[c] = res[:, c * tk:(c + 1) * tk]


def in_proj_vt(h, wt, *, bm, bn, tk):
    m, d = h.shape
    n = wt.shape[0]
    return pl.pallas_call(
        functools.partial(_in_proj_vt_kernel, tk=tk),
        grid=(m // bm, n // bn),
        in_specs=[pl.BlockSpec((bm, d), lambda i, j: (i, 0)),
                  pl.BlockSpec((bn, d), lambda i, j: (j, 0))],
        out_specs=pl.BlockSpec((bm // tk, bn, tk), lambda i, j: (i, j, 0)),
        out_shape=jax.ShapeDtypeStruct((m // tk, n, tk), BF16),
        compiler_params=_params("parallel", "arbitrary"),
        name="in_proj_vt",
    )(h, wt)


def _forget_features_kernel(h_ref, wf_ref, bf_ref, part_ref, fk_ref, fq_ref, *, blk):
    s = h_ref.shape[0]
    lane = lax.broadcasted_iota(jnp.int32, (blk, LANES), 1)
    part = jnp.broadcast_to(part_ref[...], (blk, LANES))
    lower = (lax.broadcasted_iota(jnp.int32, (blk, blk), 0)
             >= lax.broadcasted_iota(jnp.int32, (blk, blk), 1)).astype(BF16)
    one = jnp.ones((blk, LANES), F32)
    zero = jnp.zeros((blk, LANES), F32)
    carry = jnp.zeros((1, LANES), F32)
    for t in range(s // blk):
        rows = slice(t * blk, (t + 1) * blk)
        fc = _dot(h_ref[rows, :], wf_ref[...]) + bf_ref[...]
        log_f = (jnp.minimum(fc, 0.0) - jnp.log1p(jnp.exp(-jnp.abs(fc)))) * LOG2E
        x0, x1, x2 = _bf16_parts(log_f)
        cum = _dot(lower, x0) + _dot(lower, x1) + _dot(lower, x2) + carry
        carry = cum[blk - 1:blk, :]
        c0, c1, c2 = (c.astype(F32) for c in _bf16_parts(cum))
        parts = jnp.where(part == 0, c0, jnp.where(part == 1, c1, c2))
        fk = jnp.where(lane < GATE_LANES, parts, jnp.where(lane < 2 * GATE_LANES, one, zero))
        fq = jnp.where(lane < GATE_LANES, -one, jnp.where(lane < 2 * GATE_LANES, parts, zero))
        fk_ref[rows, :] = fk.astype(BF16)
        fq_ref[rows, :] = fq.astype(BF16)


def forget_features(h, wf_rep, bf_rep, *, batch, blk):
    m, d = h.shape
    s = m // batch
    part = (jnp.arange(LANES, dtype=jnp.int32) % GATE_PARTS).reshape(1, LANES)
    rows = pl.BlockSpec((s, LANES), lambda b: (b, 0))
    vec = pl.BlockSpec((1, LANES), lambda b: (0, 0))
    return pl.pallas_call(
        functools.partial(_forget_features_kernel, blk=blk),
        grid=(batch,),
        in_specs=[pl.BlockSpec((s, d), lambda b: (b, 0)),
                  pl.BlockSpec((d, LANES), lambda b: (0, 0)), vec, vec],
        out_specs=[rows, rows],
        out_shape=[jax.ShapeDtypeStruct((m, LANES), BF16)] * 2,
        compiler_params=_params("parallel"),
        name="forget_features",
    )(h, wf_rep, bf_rep, part)


def _head(ref, h):
    return ref[:, h * HEAD_DIM:(h + 1) * HEAD_DIM]


def _key_block(ref, kb, tk, h):
    return ref[pl.ds(pl.multiple_of(kb * tk, tk), tk), h * HEAD_DIM:(h + 1) * HEAD_DIM]


def _value_block_t(ref, kb, h):
    return ref[kb, h * HEAD_DIM:(h + 1) * HEAD_DIM, :]


def _run_pipeline(i, n_heads, stages, lag):
    n = len(stages)
    lead = [(n - 1 - k) * lag for k in range(n)]
    assert lead[0] <= n_heads

    def positions(cur, nxt, cur_masked, first, last):
        for t in range(first, last):
            for k in range(n):
                idx = t + lead[k]
                if 0 <= idx < n_heads:
                    stages[k](cur, idx, cur_masked)
                elif idx >= n_heads and nxt is not None:
                    stages[k](nxt, idx - n_heads, False)

    positions(i, None, True, -lead[0], 0)
    positions(i, jnp.maximum(i - 1, 0), True, 0, n_heads)

    def body(j, carry):
        cur = i - j
        positions(cur, jnp.maximum(cur - 1, 0), False, 0, n_heads)
        return carry

    lax.fori_loop(1, i + 1, body, 0)


def _softmax_stages(scores, values_t, keep, m_ref, l_ref, acc_ref, s_buf, mn_buf, al_buf, ps_buf, pv_buf):
    def stage_scores(kb, h, masked):
        s_buf[h] = scores(kb, h)

    def stage_probabilities(kb, h, masked):
        s = s_buf[h]
        if masked:
            s = jnp.where(keep, s, NEG_INF)
        m_prev = m_ref[h]
        m_new = jnp.maximum(m_prev, jnp.max(s, axis=0, keepdims=True))
        p = jnp.exp2(s - m_new)
        mn_buf[h] = m_new
        al_buf[h] = jnp.exp2(m_prev - m_new)
        ps_buf[h] = jnp.sum(p, axis=0, keepdims=True)
        pv_buf[h] = _dot(values_t(kb, h), p.astype(BF16))

    def stage_commit(kb, h, masked):
        alpha = al_buf[h]
        m_ref[h] = mn_buf[h]
        l_ref[h] = alpha * l_ref[h] + ps_buf[h]
        acc_ref[h] = alpha * acc_ref[h] + pv_buf[h]

    return stage_scores, stage_probabilities, stage_commit


def _softmax_scratch(n_heads, tk, nq):
    stat = pltpu.VMEM((n_heads, 1, nq), F32)
    wide = pltpu.VMEM((n_heads, HEAD_DIM, nq), F32)
    return [stat, stat, wide,
            pltpu.VMEM((n_heads, tk, nq), F32), stat, stat, stat, wide]


def _init_softmax_stats(m_ref, l_ref, acc_ref):
    m_ref[...] = jnp.full(m_ref.shape, NEG_INF, F32)
    l_ref[...] = jnp.zeros(l_ref.shape, F32)
    acc_ref[...] = jnp.zeros(acc_ref.shape, F32)


def _group_specs(s, tq, width, q_col, v_row):
    nq = s // tq
    q_spec = pl.BlockSpec((tq, width), lambda b, i: (b * nq + i, q_col))
    k_spec = pl.BlockSpec((s, width), lambda b, i: (b, q_col + 1))
    vt_spec = pl.BlockSpec((nq, width, tq), lambda b, i: (b, v_row, 0))
    o_spec = pl.BlockSpec((tq, width), lambda b, i: (b * nq + i, 0))
    return q_spec, k_spec, vt_spec, o_spec


def _diff_attn_kernel(lq_ref, lk_ref, g_ref, q_ref, k_ref, vt_ref, o_ref, qq_ref, m_ref, l_ref, acc_ref,
                      *bufs, tq, lam_init, lag):
    i = pl.program_id(1)
    low = lax.broadcasted_iota(jnp.int32, (HEAD_DIM, tq), 0) < DIFF_QK_DIM
    for h in range(N_DIFF):
        q_t = _head(q_ref, h).astype(F32).T
        qq_ref[h, :, 0:tq] = jnp.where(low, q_t, 0.0).astype(BF16)
        qq_ref[h, :, tq:2 * tq] = jnp.where(low, 0.0, q_t).astype(BF16)
    _init_softmax_stats(m_ref, l_ref, acc_ref)

    key = lax.broadcasted_iota(jnp.int32, (tq, 2 * tq), 0)
    qry = lax.broadcasted_iota(jnp.int32, (tq, 2 * tq), 1)
    keep = jnp.where(qry >= tq, qry - tq, qry) >= key

    def scores(kb, h):
        return _dot(_key_block(k_ref, kb, tq, h), qq_ref[h])

    stages = _softmax_stages(scores, lambda kb, h: _value_block_t(vt_ref, kb, h), keep,
                             m_ref, l_ref, acc_ref, *bufs)
    _run_pipeline(i, N_DIFF, stages, lag)

    prod = lq_ref[...] * lk_ref[...]
    first = lax.broadcasted_iota(jnp.int32, prod.shape, 1) < DIFF_QK_DIM
    e1 = jnp.exp(jnp.sum(jnp.where(first, prod, 0.0), axis=1, keepdims=True))
    e2 = jnp.exp(jnp.sum(jnp.where(first, 0.0, prod), axis=1, keepdims=True))
    lam = (e1 - e2) + lam_init
    for h in range(N_DIFF):
        o_t = (acc_ref[h, :, 0:tq] / l_ref[h, :, 0:tq]
               - lam * (acc_ref[h, :, tq:2 * tq] / l_ref[h, :, tq:2 * tq]))
        o_ref[:, h * HEAD_DIM:(h + 1) * HEAD_DIM] = (_rms(o_t.T, g_ref[...]) * (1.0 - lam_init)).astype(BF16)


def diff_attention(qk, vt, lam_q, lam_k, g_sub, lam_init, *, batch, tq, lag):
    m = qk.shape[0]
    s = m // batch
    vec = pl.BlockSpec((1, LANES), lambda b, i: (0, 0))
    q_spec, k_spec, vt_spec, o_spec = _group_specs(s, tq, A_W, QA_COL, VA_ROW)
    return pl.pallas_call(
        functools.partial(_diff_attn_kernel, tq=tq, lam_init=lam_init, lag=lag),
        grid=(batch, s // tq),
        in_specs=[vec, vec, vec, q_spec, k_spec, vt_spec],
        out_specs=o_spec,
        out_shape=jax.ShapeDtypeStruct((m, A_W), BF16),
        scratch_shapes=[pltpu.VMEM((N_DIFF, HEAD_DIM, 2 * tq), BF16)] + _softmax_scratch(N_DIFF, tq, 2 * tq),
        compiler_params=_params("parallel", "arbitrary"),
        name="diff_attention",
    )(lam_q, lam_k, g_sub, qk, qk, vt)


def _sb_attn_kernel(q_ref, k_ref, vt_ref, o_ref, qn_ref, r_ref, acc_ref,
                    zn_buf, after_buf, cs_buf, pv_buf, *, tq, lag):
    i = pl.program_id(1)
    for h in range(N_SB):
        qn_ref[h] = (-_head(q_ref, h).astype(F32)).T.astype(BF16)
    r_ref[...] = jnp.zeros(r_ref.shape, F32)
    acc_ref[...] = jnp.zeros(acc_ref.shape, F32)
    ss = lax.broadcasted_iota(jnp.int32, (tq, 2 * tq), 0)
    jj = lax.broadcasted_iota(jnp.int32, (tq, 2 * tq), 1)
    tri = (jnp.where(jj >= tq, jj - tq, jj) >= ss).astype(BF16)

    strict = lax.broadcasted_iota(jnp.int32, (tq, tq), 0) < lax.broadcasted_iota(jnp.int32, (tq, tq), 1)

    def scores(kb, h, masked):
        zn_buf[h] = _dot(_key_block(k_ref, kb, tq, h), qn_ref[h])

    def suffix_sums(kb, h, masked):
        zn = zn_buf[h]
        sp = jnp.log(1.0 + jnp.exp2(jnp.abs(zn) * -LOG2E))
        log_remain = jnp.minimum(zn, 0.0) - sp
        if masked:
            log_remain = jnp.where(strict, log_remain, 0.0)
        hi = log_remain.astype(BF16)
        lo = (log_remain - hi.astype(F32)).astype(BF16)
        after_buf[h] = _dot(tri, jnp.concatenate([hi, lo], axis=0))
        cs_buf[h] = jnp.sum(log_remain, axis=0, keepdims=True)

    def weights(kb, h, masked):
        w = jnp.exp((after_buf[h] + r_ref[h]) - zn_buf[h])
        if masked:
            w = jnp.where(strict, w, 0.0)
        pv_buf[h] = _dot(_value_block_t(vt_ref, kb, h), w.astype(BF16))
        r_ref[h] += cs_buf[h]

    def accumulate(kb, h, masked):
        acc_ref[h] += pv_buf[h]

    _run_pipeline(i, N_SB, (scores, suffix_sums, weights, accumulate), lag)
    for h in range(N_SB):
        o_ref[:, h * HEAD_DIM:(h + 1) * HEAD_DIM] = acc_ref[h].T.astype(BF16)


def stick_breaking_attention(qk, vt, *, batch, tq, lag):
    m = qk.shape[0]
    s = m // batch
    q_spec, k_spec, vt_spec, o_spec = _group_specs(s, tq, SB_W, QB_COL, VB_ROW)
    stat = pltpu.VMEM((N_SB, 1, tq), F32)
    wide = pltpu.VMEM((N_SB, HEAD_DIM, tq), F32)
    tile = pltpu.VMEM((N_SB, tq, tq), F32)
    return pl.pallas_call(
        functools.partial(_sb_attn_kernel, tq=tq, lag=lag),
        grid=(batch, s // tq),
        in_specs=[q_spec, k_spec, vt_spec],
        out_specs=o_spec,
        out_shape=jax.ShapeDtypeStruct((m, SB_W), BF16),
        scratch_shapes=[pltpu.VMEM((N_SB, HEAD_DIM, tq), BF16), stat, wide,
                        tile, tile, stat, wide],
        compiler_params=_params("parallel", "arbitrary"),
        name="stick_breaking_attention",
    )(qk, qk, vt)


def _fox_attn_kernel(fq_ref, fk_ref, q_ref, k_ref, vt_ref, o_ref, qa_ref, m_ref, l_ref, acc_ref, *bufs, tq, lag):
    i = pl.program_id(1)
    feat = lax.broadcasted_iota(jnp.int32, (LANES, tq), 0)
    fq_t = fq_ref[...].astype(F32).T
    for h in range(N_FOX):
        lo = GATE_PARTS * h
        mine = (((feat >= lo) & (feat < lo + GATE_PARTS))
                | ((feat >= GATE_LANES + lo) & (feat < GATE_LANES + lo + GATE_PARTS)))
        qa_ref[h, 0:HEAD_DIM, :] = _head(q_ref, h).astype(F32).T.astype(BF16)
        qa_ref[h, HEAD_DIM:2 * HEAD_DIM, :] = jnp.where(mine, fq_t, 0.0).astype(BF16)
    _init_softmax_stats(m_ref, l_ref, acc_ref)

    keep = lax.broadcasted_iota(jnp.int32, (tq, tq), 1) >= lax.broadcasted_iota(jnp.int32, (tq, tq), 0)

    def scores(kb, h):
        fk = fk_ref[pl.ds(pl.multiple_of(kb * tq, tq), tq), :]
        return _dot(jnp.concatenate([_key_block(k_ref, kb, tq, h), fk], axis=1), qa_ref[h])

    stages = _softmax_stages(scores, lambda kb, h: _value_block_t(vt_ref, kb, h), keep,
                             m_ref, l_ref, acc_ref, *bufs)
    _run_pipeline(i, N_FOX, stages, lag)
    for h in range(N_FOX):
        o_ref[:, h * HEAD_DIM:(h + 1) * HEAD_DIM] = (acc_ref[h] / l_ref[h]).T.astype(BF16)


def forgetting_attention(qk, vt, fk, fq, *, batch, tq, lag):
    m = qk.shape[0]
    s = m // batch
    nq = s // tq
    q_spec, k_spec, vt_spec, o_spec = _group_specs(s, tq, FOX_W, QC_COL, VC_ROW)
    return pl.pallas_call(
        functools.partial(_fox_attn_kernel, tq=tq, lag=lag),
        grid=(batch, nq),
        in_specs=[pl.BlockSpec((tq, LANES), lambda b, i: (b * nq + i, 0)),
                  pl.BlockSpec((s, LANES), lambda b, i: (b, 0)),
                  q_spec, k_spec, vt_spec],
        out_specs=o_spec,
        out_shape=jax.ShapeDtypeStruct((m, FOX_W), BF16),
        scratch_shapes=[pltpu.VMEM((N_FOX, 2 * HEAD_DIM, tq), BF16)] + _softmax_scratch(N_FOX, tq, tq),
        compiler_params=_params("parallel", "arbitrary"),
        name="forgetting_attention",
    )(fq, fk, qk, qk, vt)


def _residual_tail(y, x, g_post, g_next, x_out_ref, h_out_ref):
    x_new = x + _rms(y, g_post)
    x_out_ref[...] = x_new
    if h_out_ref is not None:
        h_out_ref[...] = _rms(x_new, g_next).astype(BF16)


def _mix_out_kernel(oa_ref, ob_ref, oc_ref, x_ref, w_ref, gb_ref, gc_ref, gp_ref, gn_ref, xo_ref, ho_ref):
    ob = _rms(ob_ref[...].astype(F32), gb_ref[...]).astype(BF16)
    oc = _rms(oc_ref[...].astype(F32), gc_ref[...]).astype(BF16)
    y = (_dot(oa_ref[...], w_ref[0:A_W, :]) + _dot(ob, w_ref[A_W:A_W + SB_W, :])
         + _dot(oc, w_ref[A_W + SB_W:, :]))
    _residual_tail(y, x_ref[...], gp_ref[...], gn_ref[...], xo_ref, ho_ref)


def mix_out(oa, ob, oc, x, w, g_sb, g_fox, g_post, g_next, *, bm):
    m, d = x.shape
    rows = lambda width: pl.BlockSpec((bm, width), lambda i: (i, 0))
    const = lambda width: pl.BlockSpec((1, width), lambda i: (0, 0))
    return pl.pallas_call(
        _mix_out_kernel,
        grid=(m // bm,),
        in_specs=[rows(A_W), rows(SB_W), rows(FOX_W), rows(d),
                  pl.BlockSpec(w.shape, lambda i: (0, 0)),
                  const(SB_W), const(FOX_W), const(d), const(d)],
        out_specs=[rows(d), rows(d)],
        out_shape=[jax.ShapeDtypeStruct((m, d), F32), jax.ShapeDtypeStruct((m, d), BF16)],
        compiler_params=_params("parallel"),
        name="mix_out",
    )(oa, ob, oc, x, w, g_sb.reshape(1, -1), g_fox.reshape(1, -1), g_post.reshape(1, -1), g_next.reshape(1, -1))


def _norm_matmul_kernel(x_ref, g_ref, w_ref, o_ref):
    o_ref[...] = _dot(_rms(x_ref[...], g_ref[...]).astype(BF16), w_ref[...]).astype(BF16)


def norm_matmul(x, g, w, *, bm):
    m, d = x.shape
    n = w.shape[1]
    return pl.pallas_call(
        _norm_matmul_kernel,
        grid=(m // bm,),
        in_specs=[pl.BlockSpec((bm, d), lambda i: (i, 0)), pl.BlockSpec((1, d), lambda i: (0, 0)),
                  pl.BlockSpec((d, n), lambda i: (0, 0))],
        out_specs=pl.BlockSpec((bm, n), lambda i: (i, 0)),
        out_shape=jax.ShapeDtypeStruct((m, n), BF16),
        compiler_params=_params("parallel"),
        name="memory_kv",
    )(x, g.reshape(1, d), w)


def _cross_attn_kernel(h_ref, x_ref, kv_ref, wq_ref, wo_ref, gp_ref, gn_ref, xo_ref, ho_ref):
    q = (_dot(h_ref[...], wq_ref[...]) * HEAD_DIM ** -0.5).astype(BF16)
    heads = []
    for hd in range(N_CROSS_HEADS):
        k = kv_ref[:, hd * HEAD_DIM:(hd + 1) * HEAD_DIM]
        v = kv_ref[:, CROSS_W + hd * HEAD_DIM:CROSS_W + (hd + 1) * HEAD_DIM]
        s = _dot_nt(q[:, hd * HEAD_DIM:(hd + 1) * HEAD_DIM], k)
        p = jnp.exp(s - jnp.max(s, axis=1, keepdims=True))
        o = _dot(p.astype(BF16), v) / jnp.sum(p, axis=1, keepdims=True)
        heads.append(o.astype(BF16))
    y = _dot(jnp.concatenate(heads, axis=1), wo_ref[...])
    _residual_tail(y, x_ref[...], gp_ref[...], gn_ref[...], xo_ref, ho_ref)


def cross_attention(h, x, kv, wq, wo, g_post, g_next, *, batch, bm):
    m, d = x.shape
    s = m // batch
    mem_len = kv.shape[0] // batch
    nb = s // bm
    rows = lambda width: pl.BlockSpec((bm, width), lambda b, i: (b * nb + i, 0))
    const = lambda shape: pl.BlockSpec(shape, lambda b, i: (0, 0))
    return pl.pallas_call(
        _cross_attn_kernel,
        grid=(batch, nb),
        in_specs=[rows(d), rows(d),
                  pl.BlockSpec((mem_len, 2 * CROSS_W), lambda b, i: (b, 0)),
                  const(wq.shape), const(wo.shape), const((1, d)), const((1, d))],
        out_specs=[rows(d), rows(d)],
        out_shape=[jax.ShapeDtypeStruct((m, d), F32), jax.ShapeDtypeStruct((m, d), BF16)],
        compiler_params=_params("parallel", "parallel"),
        name="cross_attention",
    )(h, x, kv, wq, wo, g_post.reshape(1, d), g_next.reshape(1, d))


def _ffn_up_kernel(h_ref, wg_ref, wu_ref, o_ref):
    h = h_ref[...]
    gate = _dot(h, wg_ref[...])
    up = _dot(h, wu_ref[...])
    o_ref[...] = (gate * jax.nn.sigmoid(gate) * up).astype(BF16)


def ffn_up(h, wg, wu, *, bm, bn):
    m, d = h.shape
    n = wg.shape[1]
    wspec = pl.BlockSpec((d, bn), lambda i, j: (0, j))
    return pl.pallas_call(
        _ffn_up_kernel,
        grid=(m // bm, n // bn),
        in_specs=[pl.BlockSpec((bm, d), lambda i, j: (i, 0)), wspec, wspec],
        out_specs=pl.BlockSpec((bm, bn), lambda i, j: (i, j)),
        out_shape=jax.ShapeDtypeStruct((m, n), BF16),
        compiler_params=_params("parallel", "arbitrary"),
        name="ffn_up",
    )(h, wg, wu)


def _ffn_down_kernel(*refs, emit_h):
    if emit_h:
        a_ref, w_ref, x_ref, gp_ref, gn_ref, xo_ref, ho_ref, acc_ref = refs
    else:
        a_ref, w_ref, x_ref, gp_ref, xo_ref, acc_ref = refs
        gn_ref = ho_ref = None
    kk = pl.program_id(1)

    @pl.when(kk == 0)
    def _():
        acc_ref[...] = jnp.zeros(acc_ref.shape, F32)

    acc_ref[...] += _dot(a_ref[...], w_ref[...])

    @pl.when(kk == pl.num_programs(1) - 1)
    def _():
        g_next = gn_ref[...] if emit_h else None
        _residual_tail(acc_ref[...], x_ref[...], gp_ref[...], g_next, xo_ref, ho_ref)


def ffn_down(a, w, x, g_post, g_next, *, bm, bk):
    m, d = x.shape
    kdim = a.shape[1]
    emit_h = g_next is not None
    rows = pl.BlockSpec((bm, d), lambda i, k: (i, 0))
    const = pl.BlockSpec((1, d), lambda i, k: (0, 0))
    gains = [g_post.reshape(1, d)] + ([g_next.reshape(1, d)] if emit_h else [])
    out_shape = [jax.ShapeDtypeStruct((m, d), F32)] + ([jax.ShapeDtypeStruct((m, d), BF16)] if emit_h else [])
    return pl.pallas_call(
        functools.partial(_ffn_down_kernel, emit_h=emit_h),
        grid=(m // bm, kdim // bk),
        in_specs=[pl.BlockSpec((bm, bk), lambda i, k: (i, k)),
                  pl.BlockSpec((bk, d), lambda i, k: (k, 0)),
                  rows] + [const] * len(gains),
        out_specs=[rows] * len(out_shape),
        out_shape=out_shape,
        scratch_shapes=[pltpu.VMEM((bm, d), F32)],
        compiler_params=_params("parallel", "arbitrary"),
        name="ffn_down",
    )(a, w, x, *gains)


def _pick(n, *candidates):
    for c in candidates:
        if n % c == 0:
            return c
    return n


def _columns(w, names):
    return jnp.concatenate([w[..., _IN_OFFSETS[n][0]:_IN_OFFSETS[n][1]] for n in names], axis=-1)


def kernel(x, mem, positions, g_mix_pre, g_mix_post, w_in, b_f, lam_q1, lam_k1, lam_q2, lam_k2, g_diff_sub, g_sb_out, g_fox_out, w_out, g_x_pre, g_x_post, g_mem, w_cq, w_ckv, w_co, g_ffn_pre, g_ffn_post, w_gate, w_up, w_down):
    batch, seq, d = x.shape
    depth = w_in.shape[0]
    m = batch * seq
    d_ff = w_gate.shape[2]
    assert w_in.shape[2] == _IN_OFFSETS["fc"][1] and d == V_ROWS

    bm_row = _pick(m, 512, 256, 128)
    bm_mm = _pick(m, 1024, 512, 256, 128)
    bm_big = _pick(m, 2048, 1024, 512, 256, 128)
    lag_diff, lag_sb, lag_fox = 2, 2, 3
    bn_in = 2 * A_W
    bn_ff = _pick(d_ff, 512, 256, 128)
    bk_ff = _pick(d_ff, 1408, 1024, 512, 256, 128)
    tq = _pick(seq, 256, 128)
    bm_x = _pick(seq, 512, 256, 128)
    bm_mem = _pick(mem.shape[0] * mem.shape[1], 512, 256, 128)

    w_qk = _columns(w_in, QK_ORDER).astype(BF16)
    w_vt = jnp.swapaxes(_columns(w_in, V_ORDER), 1, 2).astype(BF16)
    gate_lane_head = jnp.arange(2 * GATE_LANES) % GATE_LANES // GATE_PARTS
    wf_rep = jnp.pad(_columns(w_in, ("fc",))[:, :, gate_lane_head],
                     ((0, 0), (0, 0), (0, LANES - 2 * GATE_LANES))).astype(BF16)
    bf_rep = jnp.pad(b_f[:, gate_lane_head], ((0, 0), (0, LANES - 2 * GATE_LANES))).reshape(depth, 1, LANES)
    w_out_b, w_cq_b, w_ckv_b, w_co_b = (w.astype(BF16) for w in (w_out, w_cq, w_ckv, w_co))
    w_gate_b, w_up_b, w_down_b = (w.astype(BF16) for w in (w_gate, w_up, w_down))
    lam_q = jnp.concatenate([lam_q1, lam_q2], axis=1).reshape(depth, 1, LANES)
    lam_k = jnp.concatenate([lam_k1, lam_k2], axis=1).reshape(depth, 1, LANES)

    colscale = jnp.concatenate([
        jnp.full((w,), s, F32) for w, s in ((SB_W, HEAD_DIM ** -0.5), (SB_W, 1.0),
                                            (FOX_W, HEAD_DIM ** -0.5 * LOG2E), (FOX_W, 1.0),
                                            (A_W, DIFF_QK_DIM ** -0.5 * LOG2E), (A_W, 1.0))]).reshape(1, QK_COLS)

    x = x.reshape(m, d)
    mem2 = mem.reshape(-1, d)
    cos_t, sin_t = rope_tables(positions, bm=bm_row)
    h = prenorm(x, g_mix_pre[0], bm=bm_row)

    for l in range(depth):
        lam_init = 0.8 - 0.6 * math.exp(-0.3 * l)
        qk = in_proj_qk(h, w_qk[l], colscale, cos_t, sin_t, bm=bm_big, bn=bn_in)
        vt = in_proj_vt(h, w_vt[l], bm=bm_mm, bn=bn_in, tk=tq)
        fk, fq = forget_features(h, wf_rep[l], bf_rep[l], batch=batch, blk=tq)
        oa = diff_attention(qk, vt, lam_q[l], lam_k[l], g_diff_sub[l].reshape(1, LANES), lam_init, batch=batch, tq=tq,
                            lag=lag_diff)
        ob = stick_breaking_attention(qk, vt, batch=batch, tq=tq, lag=lag_sb)
        oc = forgetting_attention(qk, vt, fk, fq, batch=batch, tq=tq, lag=lag_fox)
        x, h = mix_out(oa, ob, oc, x, w_out_b[l], g_sb_out[l], g_fox_out[l], g_mix_post[l], g_x_pre[l], bm=bm_row // 2)
        kv = norm_matmul(mem2, g_mem[l], w_ckv_b[l], bm=bm_mem)
        x, h = cross_attention(h, x, kv, w_cq_b[l], w_co_b[l], g_x_post[l], g_ffn_pre[l], batch=batch, bm=bm_x)
        a = ffn_up(h, w_gate_b[l], w_up_b[l], bm=bm_big, bn=bn_ff)
        g_next = g_mix_pre[l + 1] if l + 1 < depth else None
        x, *rest = ffn_down(a, w_down_b[l], x, g_ffn_post[l], g_next, bm=bm_row, bk=bk_ff)
        h = rest[0] if rest else None
    return x.reshape(batch, seq, d)
```

```python
import functools
import math

import jax
import jax.numpy as jnp
from jax import lax
from jax.experimental import pallas as pl
from jax.experimental.pallas import tpu as pltpu

F32 = jnp.float32
BF16 = jnp.bfloat16

EPS = 1e-6
NEG_INF = -1e30
ROPE_THETA = 10000.0
LOG2E = math.log2(math.e)

LANES = 128
HEAD_DIM = 128
N_DIFF, N_SB, N_FOX = 4, 6, 6
DIFF_QK_DIM = HEAD_DIM // 2
A_W, SB_W, FOX_W = N_DIFF * HEAD_DIM, N_SB * HEAD_DIM, N_FOX * HEAD_DIM
V_ROWS = A_W + SB_W + FOX_W
N_CROSS_HEADS = 4
CROSS_W = N_CROSS_HEADS * HEAD_DIM
GATE_PARTS = 3
GATE_LANES = GATE_PARTS * N_FOX

VMEM_LIMIT = 56 * 1024 * 1024

_IN_OFFSETS = {}
_off = 0
for _name, _width in (("qa", A_W), ("ka", A_W), ("va", A_W), ("qb", SB_W), ("kb", SB_W), ("vb", SB_W),
                      ("qc", FOX_W), ("kc", FOX_W), ("vc", FOX_W), ("fc", N_FOX)):
    _IN_OFFSETS[_name] = (_off, _off + _width)
    _off += _width
QK_ORDER = ("qb", "kb", "qc", "kc")
QK_ROPE_ORDER = ("qa", "ka")
V_ORDER = ("vb", "vc", "va")
QB_COL, QC_COL, QA_COL = 0, 2, 0
VB_ROW, VC_ROW, VA_ROW = 0, 1, (2 * SB_W) // A_W


def _params(*semantics):
    return pltpu.CompilerParams(dimension_semantics=semantics, vmem_limit_bytes=VMEM_LIMIT)


def _rms(xf, g):
    return xf * lax.rsqrt(jnp.mean(xf * xf, axis=-1, keepdims=True) + EPS) * g


def _dot(a, b):
    return jnp.dot(a, b, preferred_element_type=F32)


def _dot_nt(a, b):
    return lax.dot_general(a, b, (((1,), (1,)), ((), ())), preferred_element_type=F32)


def _bf16_parts(x):
    p0 = x.astype(BF16)
    r1 = x - p0.astype(F32)
    p1 = r1.astype(BF16)
    p2 = (r1 - p1.astype(F32)).astype(BF16)
    return p0, p1, p2


def _prenorm_kernel(x_ref, g_ref, h_ref):
    h_ref[...] = _rms(x_ref[...], g_ref[...]).astype(BF16)


def prenorm(x, g, *, bm):
    m, d = x.shape
    return pl.pallas_call(
        _prenorm_kernel,
        grid=(m // bm,),
        in_specs=[pl.BlockSpec((bm, d), lambda i: (i, 0)), pl.BlockSpec((1, d), lambda i: (0, 0))],
        out_specs=pl.BlockSpec((bm, d), lambda i: (i, 0)),
        out_shape=jax.ShapeDtypeStruct((m, d), BF16),
        compiler_params=_params("parallel"),
        name="prenorm",
    )(x, g.reshape(1, d))


def _rope_table_kernel(pos_ref, invf_ref, sign_ref, cos_ref, sin_ref):
    ang = pos_ref[...].astype(F32) * invf_ref[...]
    cos_ref[...] = jnp.cos(ang)
    sin_ref[...] = jnp.sin(ang) * sign_ref[...]


def rope_tables(positions, *, bm):
    m = positions.size
    half = DIFF_QK_DIM // 2
    inv_freq = ROPE_THETA ** (-jnp.arange(half, dtype=F32) / half)
    invf = jnp.tile(inv_freq, LANES // half).reshape(1, LANES)
    sign = jnp.where((jnp.arange(LANES) % DIFF_QK_DIM) < half, -1.0, 1.0).astype(F32).reshape(1, LANES)
    row = pl.BlockSpec((bm, LANES), lambda i: (i, 0))
    const = pl.BlockSpec((1, LANES), lambda i: (0, 0))
    return pl.pallas_call(
        _rope_table_kernel,
        grid=(m // bm,),
        in_specs=[pl.BlockSpec((bm, 1), lambda i: (i, 0)), const, const],
        out_specs=[row, row],
        out_shape=[jax.ShapeDtypeStruct((m, LANES), F32)] * 2,
        compiler_params=_params("parallel"),
        name="rope_tables",
    )(positions.reshape(m, 1), invf, sign)


def _in_proj_kernel(h_ref, w_ref, cs_ref, o_ref):
    o_ref[...] = (_dot(h_ref[...], w_ref[...]) * cs_ref[...]).astype(BF16)


def _in_proj_rope_kernel(h_ref, w_ref, cs_ref, cos_ref, sin_ref, o_ref):
    acc = _dot(h_ref[...], w_ref[...]) * cs_ref[...]
    bm, bn = acc.shape
    c, s = cos_ref[...], sin_ref[...]
    first_half = (lax.broadcasted_iota(jnp.int32, (bm, LANES), 1) % DIFF_QK_DIM) < DIFF_QK_DIM // 2
    for t in range(bn // LANES):
        a = acc[:, t * LANES:(t + 1) * LANES]
        partner = jnp.where(first_half, pltpu.roll(a, LANES - DIFF_QK_DIM // 2, 1),
                            pltpu.roll(a, DIFF_QK_DIM // 2, 1))
        o_ref[:, t * LANES:(t + 1) * LANES] = (a * c + partner * s).astype(BF16)


def in_proj(h, w, colscale, rope_tables_or_none, *, bm, bn):
    m, d = h.shape
    n = w.shape[1]
    assert n % bn == 0 and m % bm == 0
    rope = rope_tables_or_none is not None
    row_tab = pl.BlockSpec((bm, LANES), lambda i, j: (i, 0))
    return pl.pallas_call(
        _in_proj_rope_kernel if rope else _in_proj_kernel,
        grid=(m // bm, n // bn),
        in_specs=[pl.BlockSpec((bm, d), lambda i, j: (i, 0)),
                  pl.BlockSpec((d, bn), lambda i, j: (0, j)),
                  pl.BlockSpec((1, bn), lambda i, j: (0, j))] + ([row_tab, row_tab] if rope else []),
        out_specs=pl.BlockSpec((bm, bn), lambda i, j: (i, j)),
        out_shape=jax.ShapeDtypeStruct((m, n), BF16),
        compiler_params=_params("parallel", "arbitrary"),
        name="in_proj_rope" if rope else "in_proj",
    )(h, w, colscale, *(rope_tables_or_none or ()))


def _in_proj_vt_kernel(h_ref, wt_ref, o_ref, *, tk):
    res = _dot_nt(wt_ref[...], h_ref[...]).astype(BF16)
    for c in range(o_ref.shape[0]):
        o_ref[c] = res[:, c * tk:(c + 1) * tk]


def in_proj_vt(h, wt, *, bm, bn, tk):
    m, d = h.shape
    n = wt.shape[0]
    return pl.pallas_call(
        functools.partial(_in_proj_vt_kernel, tk=tk),
        grid=(m // bm, n // bn),
        in_specs=[pl.BlockSpec((bm, d), lambda i, j: (i, 0)),
                  pl.BlockSpec((bn, d), lambda i, j: (j, 0))],
        out_specs=pl.BlockSpec((bm // tk, bn, tk), lambda i, j: (i, j, 0)),
        out_shape=jax.ShapeDtypeStruct((m // tk, n, tk), BF16),
        compiler_params=_params("parallel", "arbitrary"),
        name="in_proj_vt",
    )(h, wt)


def _forget_features_kernel(h_ref, wf_ref, bf_ref, part_ref, fk_ref, fq_ref, *, blk):
    s = h_ref.shape[0]
    lane = lax.broadcasted_iota(jnp.int32, (blk, LANES), 1)
    part = jnp.broadcast_to(part_ref[...], (blk, LANES))
    lower = (lax.broadcasted_iota(jnp.int32, (blk, blk), 0)
             >= lax.broadcasted_iota(jnp.int32, (blk, blk), 1)).astype(BF16)
    one = jnp.ones((blk, LANES), F32)
    zero = jnp.zeros((blk, LANES), F32)
    carry = jnp.zeros((1, LANES), F32)
    for t in range(s // blk):
        rows = slice(t * blk, (t + 1) * blk)
        fc = _dot(h_ref[rows, :], wf_ref[...]) + bf_ref[...]
        log_f = (jnp.minimum(fc, 0.0) - jnp.log1p(jnp.exp(-jnp.abs(fc)))) * LOG2E
        x0, x1, x2 = _bf16_parts(log_f)
        cum = _dot(lower, x0) + _dot(lower, x1) + _dot(lower, x2) + carry
        carry = cum[blk - 1:blk, :]
        c0, c1, c2 = (c.astype(F32) for c in _bf16_parts(cum))
        parts = jnp.where(part == 0, c0, jnp.where(part == 1, c1, c2))
        fk = jnp.where(lane < GATE_LANES, parts, jnp.where(lane < 2 * GATE_LANES, one, zero))
        fq = jnp.where(lane < GATE_LANES, -one, jnp.where(lane < 2 * GATE_LANES, parts, zero))
        fk_ref[rows, :] = fk.astype(BF16)
        fq_ref[rows, :] = fq.astype(BF16)


def forget_features(h, wf_rep, bf_rep, *, batch, blk):
    m, d = h.shape
    s = m // batch
    part = (jnp.arange(LANES, dtype=jnp.int32) % GATE_PARTS).reshape(1, LANES)
    rows = pl.BlockSpec((s, LANES), lambda b: (b, 0))
    vec = pl.BlockSpec((1, LANES), lambda b: (0, 0))
    return pl.pallas_call(
        functools.partial(_forget_features_kernel, blk=blk),
        grid=(batch,),
        in_specs=[pl.BlockSpec((s, d), lambda b: (b, 0)),
                  pl.BlockSpec((d, LANES), lambda b: (0, 0)), vec, vec],
        out_specs=[rows, rows],
        out_shape=[jax.ShapeDtypeStruct((m, LANES), BF16)] * 2,
        compiler_params=_params("parallel"),
        name="forget_features",
    )(h, wf_rep, bf_rep, part)


def _head(ref, h):
    return ref[:, h * HEAD_DIM:(h + 1) * HEAD_DIM]


def _key_block(ref, kb, tk, h):
    return ref[pl.ds(pl.multiple_of(kb * tk, tk), tk), h * HEAD_DIM:(h + 1) * HEAD_DIM]


def _value_block_t(ref, kb, h):
    return ref[kb, h * HEAD_DIM:(h + 1) * HEAD_DIM, :]


def _run_pipeline(i, n_heads, stages, lag, prepare):
    n = len(stages)
    lead = [(n - 1 - k) * lag for k in range(n)]
    assert lead[0] <= n_heads
    unprepared = set(range(n_heads))

    def positions(cur, nxt, cur_masked, first, last):
        for t in range(first, last):
            for k in range(n):
                idx = t + lead[k]
                if 0 <= idx < n_heads:
                    if k == 0 and idx in unprepared:
                        unprepared.discard(idx)
                        prepare(idx)
                    stages[k](cur, idx, cur_masked)
                elif idx >= n_heads and nxt is not None:
                    stages[k](nxt, idx - n_heads, False)

    positions(i, None, True, -lead[0], 0)
    positions(i, jnp.maximum(i - 1, 0), True, 0, n_heads)

    def body(j, carry):
        cur = i - j
        positions(cur, jnp.maximum(cur - 1, 0), False, 0, n_heads)
        return carry

    lax.fori_loop(1, i + 1, body, 0)


def _softmax_stages(scores, values_t, keep, m_ref, l_ref, acc_ref, s_buf, mn_buf, al_buf, ps_buf, pv_buf):
    def stage_scores(kb, h, masked):
        s_buf[h] = scores(kb, h)

    def stage_probabilities(kb, h, masked):
        s = s_buf[h]
        if masked:
            s = jnp.where(keep, s, NEG_INF)
        m_prev = m_ref[h]
        m_new = jnp.maximum(m_prev, jnp.max(s, axis=0, keepdims=True))
        p = jnp.exp2(s - m_new)
        mn_buf[h] = m_new
        al_buf[h] = jnp.exp2(m_prev - m_new)
        ps_buf[h] = jnp.sum(p, axis=0, keepdims=True)
        pv_buf[h] = _dot(values_t(kb, h), p.astype(BF16))

    def stage_commit(kb, h, masked):
        alpha = al_buf[h]
        m_ref[h] = mn_buf[h]
        l_ref[h] = alpha * l_ref[h] + ps_buf[h]
        acc_ref[h] = alpha * acc_ref[h] + pv_buf[h]

    return stage_scores, stage_probabilities, stage_commit


def _softmax_scratch(n_heads, tk, nq):
    stat = pltpu.VMEM((n_heads, 1, nq), F32)
    wide = pltpu.VMEM((n_heads, HEAD_DIM, nq), F32)
    return [stat, stat, wide,
            pltpu.VMEM((n_heads, tk, nq), F32), stat, stat, stat, wide]


def _init_softmax_stats(m_ref, l_ref, acc_ref):
    m_ref[...] = jnp.full(m_ref.shape, NEG_INF, F32)
    l_ref[...] = jnp.zeros(l_ref.shape, F32)
    acc_ref[...] = jnp.zeros(acc_ref.shape, F32)


def _group_specs(s, tq, width, q_col, v_row):
    nq = s // tq
    q_spec = pl.BlockSpec((tq, width), lambda b, i: (b * nq + i, q_col))
    k_spec = pl.BlockSpec((s, width), lambda b, i: (b, q_col + 1))
    vt_spec = pl.BlockSpec((nq, width, tq), lambda b, i: (b, v_row, 0))
    o_spec = pl.BlockSpec((tq, width), lambda b, i: (b * nq + i, 0))
    return q_spec, k_spec, vt_spec, o_spec


def _diff_attn_kernel(lq_ref, lk_ref, g_ref, q_ref, k_ref, vt_ref, o_ref, qq_ref, m_ref, l_ref, acc_ref,
                      *bufs, tq, lam_init, lag):
    i = pl.program_id(1)
    low = lax.broadcasted_iota(jnp.int32, (HEAD_DIM, tq), 0) < DIFF_QK_DIM

    def prepare(h):
        q_t = _head(q_ref, h).astype(F32).T
        qq_ref[h, :, 0:tq] = jnp.where(low, q_t, 0.0).astype(BF16)
        qq_ref[h, :, tq:2 * tq] = jnp.where(low, 0.0, q_t).astype(BF16)

    _init_softmax_stats(m_ref, l_ref, acc_ref)

    key = lax.broadcasted_iota(jnp.int32, (tq, 2 * tq), 0)
    qry = lax.broadcasted_iota(jnp.int32, (tq, 2 * tq), 1)
    keep = jnp.where(qry >= tq, qry - tq, qry) >= key

    def scores(kb, h):
        return _dot(_key_block(k_ref, kb, tq, h), qq_ref[h])

    stages = _softmax_stages(scores, lambda kb, h: _value_block_t(vt_ref, kb, h), keep,
                             m_ref, l_ref, acc_ref, *bufs)
    _run_pipeline(i, N_DIFF, stages, lag, prepare)

    prod = lq_ref[...] * lk_ref[...]
    first = lax.broadcasted_iota(jnp.int32, prod.shape, 1) < DIFF_QK_DIM
    e1 = jnp.exp(jnp.sum(jnp.where(first, prod, 0.0), axis=1, keepdims=True))
    e2 = jnp.exp(jnp.sum(jnp.where(first, 0.0, prod), axis=1, keepdims=True))
    lam = (e1 - e2) + lam_init
    for h in range(N_DIFF):
        o_t = (acc_ref[h, :, 0:tq] / l_ref[h, :, 0:tq]
               - lam * (acc_ref[h, :, tq:2 * tq] / l_ref[h, :, tq:2 * tq]))
        o_ref[:, h * HEAD_DIM:(h + 1) * HEAD_DIM] = (_rms(o_t.T, g_ref[...]) * (1.0 - lam_init)).astype(BF16)


def diff_attention(qk, vt, lam_q, lam_k, g_sub, lam_init, *, batch, tq, lag):
    m = qk.shape[0]
    s = m // batch
    vec = pl.BlockSpec((1, LANES), lambda b, i: (0, 0))
    q_spec, k_spec, vt_spec, o_spec = _group_specs(s, tq, A_W, QA_COL, VA_ROW)
    return pl.pallas_call(
        functools.partial(_diff_attn_kernel, tq=tq, lam_init=lam_init, lag=lag),
        grid=(batch, s // tq),
        in_specs=[vec, vec, vec, q_spec, k_spec, vt_spec],
        out_specs=o_spec,
        out_shape=jax.ShapeDtypeStruct((m, A_W), BF16),
        scratch_shapes=[pltpu.VMEM((N_DIFF, HEAD_DIM, 2 * tq), BF16)] + _softmax_scratch(N_DIFF, tq, 2 * tq),
        compiler_params=_params("parallel", "arbitrary"),
        name="diff_attention",
    )(lam_q, lam_k, g_sub, qk, qk, vt)


def _sb_attn_kernel(q_ref, k_ref, vt_ref, o_ref, qn_ref, r_ref, acc_ref,
                    zn_buf, after_buf, cs_buf, pv_buf, *, tq, lag):
    i = pl.program_id(1)
    def prepare(h):
        qn_ref[h] = (-_head(q_ref, h).astype(F32)).T.astype(BF16)

    r_ref[...] = jnp.zeros(r_ref.shape, F32)
    acc_ref[...] = jnp.zeros(acc_ref.shape, F32)
    ss = lax.broadcasted_iota(jnp.int32, (tq, 2 * tq), 0)
    jj = lax.broadcasted_iota(jnp.int32, (tq, 2 * tq), 1)
    tri = (jnp.where(jj >= tq, jj - tq, jj) >= ss).astype(BF16)

    strict = lax.broadcasted_iota(jnp.int32, (tq, tq), 0) < lax.broadcasted_iota(jnp.int32, (tq, tq), 1)

    def scores(kb, h, masked):
        zn_buf[h] = _dot(_key_block(k_ref, kb, tq, h), qn_ref[h])

    def suffix_sums(kb, h, masked):
        zn = zn_buf[h]
        sp = jnp.log(1.0 + jnp.exp2(jnp.abs(zn) * -LOG2E))
        log_remain = jnp.minimum(zn, 0.0) - sp
        if masked:
            log_remain = jnp.where(strict, log_remain, 0.0)
        hi = log_remain.astype(BF16)
        lo = (log_remain - hi.astype(F32)).astype(BF16)
        after_buf[h] = _dot(tri, jnp.concatenate([hi, lo], axis=0))
        cs_buf[h] = jnp.sum(log_remain, axis=0, keepdims=True)

    def weights(kb, h, masked):
        w = jnp.exp((after_buf[h] + r_ref[h]) - zn_buf[h])
        if masked:
            w = jnp.where(strict, w, 0.0)
        pv_buf[h] = _dot(_value_block_t(vt_ref, kb, h), w.astype(BF16))
        r_ref[h] += cs_buf[h]

    def accumulate(kb, h, masked):
        acc_ref[h] += pv_buf[h]

    _run_pipeline(i, N_SB, (scores, suffix_sums, weights, accumulate), lag, prepare)
    for h in range(N_SB):
        o_ref[:, h * HEAD_DIM:(h + 1) * HEAD_DIM] = acc_ref[h].T.astype(BF16)


def stick_breaking_attention(qk, vt, *, batch, tq, lag):
    m = qk.shape[0]
    s = m // batch
    q_spec, k_spec, vt_spec, o_spec = _group_specs(s, tq, SB_W, QB_COL, VB_ROW)
    stat = pltpu.VMEM((N_SB, 1, tq), F32)
    wide = pltpu.VMEM((N_SB, HEAD_DIM, tq), F32)
    tile = pltpu.VMEM((N_SB, tq, tq), F32)
    return pl.pallas_call(
        functools.partial(_sb_attn_kernel, tq=tq, lag=lag),
        grid=(batch, s // tq),
        in_specs=[q_spec, k_spec, vt_spec],
        out_specs=o_spec,
        out_shape=jax.ShapeDtypeStruct((m, SB_W), BF16),
        scratch_shapes=[pltpu.VMEM((N_SB, HEAD_DIM, tq), BF16), stat, wide,
                        tile, tile, stat, wide],
        compiler_params=_params("parallel", "arbitrary"),
        name="stick_breaking_attention",
    )(qk, qk, vt)


def _fox_attn_kernel(fq_ref, fk_ref, q_ref, k_ref, vt_ref, o_ref, qa_ref, m_ref, l_ref, acc_ref, *bufs, tq, lag):
    i = pl.program_id(1)
    feat = lax.broadcasted_iota(jnp.int32, (LANES, tq), 0)
    fq_t = fq_ref[...].astype(F32).T

    def prepare(h):
        lo = GATE_PARTS * h
        mine = (((feat >= lo) & (feat < lo + GATE_PARTS))
                | ((feat >= GATE_LANES + lo) & (feat < GATE_LANES + lo + GATE_PARTS)))
        qa_ref[h, 0:HEAD_DIM, :] = _head(q_ref, h).astype(F32).T.astype(BF16)
        qa_ref[h, HEAD_DIM:2 * HEAD_DIM, :] = jnp.where(mine, fq_t, 0.0).astype(BF16)

    _init_softmax_stats(m_ref, l_ref, acc_ref)

    keep = lax.broadcasted_iota(jnp.int32, (tq, tq), 1) >= lax.broadcasted_iota(jnp.int32, (tq, tq), 0)

    def scores(kb, h):
        fk = fk_ref[pl.ds(pl.multiple_of(kb * tq, tq), tq), :]
        return _dot(jnp.concatenate([_key_block(k_ref, kb, tq, h), fk], axis=1), qa_ref[h])

    stages = _softmax_stages(scores, lambda kb, h: _value_block_t(vt_ref, kb, h), keep,
                             m_ref, l_ref, acc_ref, *bufs)
    _run_pipeline(i, N_FOX, stages, lag, prepare)
    for h in range(N_FOX):
        o_ref[:, h * HEAD_DIM:(h + 1) * HEAD_DIM] = (acc_ref[h] / l_ref[h]).T.astype(BF16)


def forgetting_attention(qk, vt, fk, fq, *, batch, tq, lag):
    m = qk.shape[0]
    s = m // batch
    nq = s // tq
    q_spec, k_spec, vt_spec, o_spec = _group_specs(s, tq, FOX_W, QC_COL, VC_ROW)
    return pl.pallas_call(
        functools.partial(_fox_attn_kernel, tq=tq, lag=lag),
        grid=(batch, nq),
        in_specs=[pl.BlockSpec((tq, LANES), lambda b, i: (b * nq + i, 0)),
                  pl.BlockSpec((s, LANES), lambda b, i: (b, 0)),
                  q_spec, k_spec, vt_spec],
        out_specs=o_spec,
        out_shape=jax.ShapeDtypeStruct((m, FOX_W), BF16),
        scratch_shapes=[pltpu.VMEM((N_FOX, 2 * HEAD_DIM, tq), BF16)] + _softmax_scratch(N_FOX, tq, tq),
        compiler_params=_params("parallel", "arbitrary"),
        name="forgetting_attention",
    )(fq, fk, qk, qk, vt)


def _residual_tail(y, x, g_post, g_next, x_out_ref, h_out_ref):
    x_new = x + _rms(y, g_post)
    x_out_ref[...] = x_new
    if h_out_ref is not None:
        h_out_ref[...] = _rms(x_new, g_next).astype(BF16)


def _mix_out_kernel(oa_ref, ob_ref, oc_ref, x_ref, w_ref, gb_ref, gc_ref, gp_ref, gn_ref, xo_ref, ho_ref):
    ob = _rms(ob_ref[...].astype(F32), gb_ref[...]).astype(BF16)
    oc = _rms(oc_ref[...].astype(F32), gc_ref[...]).astype(BF16)
    y = (_dot(oa_ref[...], w_ref[0:A_W, :]) + _dot(ob, w_ref[A_W:A_W + SB_W, :])
         + _dot(oc, w_ref[A_W + SB_W:, :]))
    _residual_tail(y, x_ref[...], gp_ref[...], gn_ref[...], xo_ref, ho_ref)


def mix_out(oa, ob, oc, x, w, g_sb, g_fox, g_post, g_next, *, bm):
    m, d = x.shape
    rows = lambda width: pl.BlockSpec((bm, width), lambda i: (i, 0))
    const = lambda width: pl.BlockSpec((1, width), lambda i: (0, 0))
    return pl.pallas_call(
        _mix_out_kernel,
        grid=(m // bm,),
        in_specs=[rows(A_W), rows(SB_W), rows(FOX_W), rows(d),
                  pl.BlockSpec(w.shape, lambda i: (0, 0)),
                  const(SB_W), const(FOX_W), const(d), const(d)],
        out_specs=[rows(d), rows(d)],
        out_shape=[jax.ShapeDtypeStruct((m, d), F32), jax.ShapeDtypeStruct((m, d), BF16)],
        compiler_params=_params("parallel"),
        name="mix_out",
    )(oa, ob, oc, x, w, g_sb.reshape(1, -1), g_fox.reshape(1, -1), g_post.reshape(1, -1), g_next.reshape(1, -1))


def _norm_matmul_kernel(x_ref, g_ref, w_ref, o_ref):
    o_ref[...] = _dot(_rms(x_ref[...], g_ref[...]).astype(BF16), w_ref[...]).astype(BF16)


def norm_matmul(x, g, w, *, bm):
    m, d = x.shape
    n = w.shape[1]
    return pl.pallas_call(
        _norm_matmul_kernel,
        grid=(m // bm,),
        in_specs=[pl.BlockSpec((bm, d), lambda i: (i, 0)), pl.BlockSpec((1, d), lambda i: (0, 0)),
                  pl.BlockSpec((d, n), lambda i: (0, 0))],
        out_specs=pl.BlockSpec((bm, n), lambda i: (i, 0)),
        out_shape=jax.ShapeDtypeStruct((m, n), BF16),
        compiler_params=_params("parallel"),
        name="memory_kv",
    )(x, g.reshape(1, d), w)


def _cross_attn_kernel(h_ref, x_ref, kv_ref, wq_ref, wo_ref, gp_ref, gn_ref, xo_ref, ho_ref):
    q = (_dot(h_ref[...], wq_ref[...]) * HEAD_DIM ** -0.5).astype(BF16)
    heads = []
    for hd in range(N_CROSS_HEADS):
        k = kv_ref[:, hd * HEAD_DIM:(hd + 1) * HEAD_DIM]
        v = kv_ref[:, CROSS_W + hd * HEAD_DIM:CROSS_W + (hd + 1) * HEAD_DIM]
        s = _dot_nt(q[:, hd * HEAD_DIM:(hd + 1) * HEAD_DIM], k)
        p = jnp.exp(s - jnp.max(s, axis=1, keepdims=True))
        o = _dot(p.astype(BF16), v) / jnp.sum(p, axis=1, keepdims=True)
        heads.append(o.astype(BF16))
    y = _dot(jnp.concatenate(heads, axis=1), wo_ref[...])
    _residual_tail(y, x_ref[...], gp_ref[...], gn_ref[...], xo_ref, ho_ref)


def cross_attention(h, x, kv, wq, wo, g_post, g_next, *, batch, bm):
    m, d = x.shape
    s = m // batch
    mem_len = kv.shape[0] // batch
    nb = s // bm
    rows = lambda width: pl.BlockSpec((bm, width), lambda b, i: (b * nb + i, 0))
    const = lambda shape: pl.BlockSpec(shape, lambda b, i: (0, 0))
    return pl.pallas_call(
        _cross_attn_kernel,
        grid=(batch, nb),
        in_specs=[rows(d), rows(d),
                  pl.BlockSpec((mem_len, 2 * CROSS_W), lambda b, i: (b, 0)),
                  const(wq.shape), const(wo.shape), const((1, d)), const((1, d))],
        out_specs=[rows(d), rows(d)],
        out_shape=[jax.ShapeDtypeStruct((m, d), F32), jax.ShapeDtypeStruct((m, d), BF16)],
        compiler_params=_params("parallel", "parallel"),
        name="cross_attention",
    )(h, x, kv, wq, wo, g_post.reshape(1, d), g_next.reshape(1, d))


def _ffn_up_kernel(h_ref, wg_ref, wu_ref, o_ref):
    h = h_ref[...]
    gate = _dot(h, wg_ref[...])
    up = _dot(h, wu_ref[...])
    o_ref[...] = (gate * jax.nn.sigmoid(gate) * up).astype(BF16)


def ffn_up(h, wg, wu, *, bm, bn):
    m, d = h.shape
    n = wg.shape[1]
    wspec = pl.BlockSpec((d, bn), lambda i, j: (0, j))
    return pl.pallas_call(
        _ffn_up_kernel,
        grid=(m // bm, n // bn),
        in_specs=[pl.BlockSpec((bm, d), lambda i, j: (i, 0)), wspec, wspec],
        out_specs=pl.BlockSpec((bm, bn), lambda i, j: (i, j)),
        out_shape=jax.ShapeDtypeStruct((m, n), BF16),
        compiler_params=_params("parallel", "arbitrary"),
        name="ffn_up",
    )(h, wg, wu)


def _ffn_down_kernel(*refs, emit_h):
    if emit_h:
        a_ref, w_ref, x_ref, gp_ref, gn_ref, xo_ref, ho_ref, acc_ref = refs
    else:
        a_ref, w_ref, x_ref, gp_ref, xo_ref, acc_ref = refs
        gn_ref = ho_ref = None
    kk = pl.program_id(1)

    @pl.when(kk == 0)
    def _():
        acc_ref[...] = jnp.zeros(acc_ref.shape, F32)

    acc_ref[...] += _dot(a_ref[...], w_ref[...])

    @pl.when(kk == pl.num_programs(1) - 1)
    def _():
        g_next = gn_ref[...] if emit_h else None
        _residual_tail(acc_ref[...], x_ref[...], gp_ref[...], g_next, xo_ref, ho_ref)


def ffn_down(a, w, x, g_post, g_next, *, bm, bk):
    m, d = x.shape
    kdim = a.shape[1]
    emit_h = g_next is not None
    rows = pl.BlockSpec((bm, d), lambda i, k: (i, 0))
    const = pl.BlockSpec((1, d), lambda i, k: (0, 0))
    gains = [g_post.reshape(1, d)] + ([g_next.reshape(1, d)] if emit_h else [])
    out_shape = [jax.ShapeDtypeStruct((m, d), F32)] + ([jax.ShapeDtypeStruct((m, d), BF16)] if emit_h else [])
    return pl.pallas_call(
        functools.partial(_ffn_down_kernel, emit_h=emit_h),
        grid=(m // bm, kdim // bk),
        in_specs=[pl.BlockSpec((bm, bk), lambda i, k: (i, k)),
                  pl.BlockSpec((bk, d), lambda i, k: (k, 0)),
                  rows] + [const] * len(gains),
        out_specs=[rows] * len(out_shape),
        out_shape=out_shape,
        scratch_shapes=[pltpu.VMEM((bm, d), F32)],
        compiler_params=_params("parallel", "arbitrary"),
        name="ffn_down",
    )(a, w, x, *gains)


def _pick(n, *candidates):
    for c in candidates:
        if n % c == 0:
            return c
    return n


def _columns(w, names):
    return jnp.concatenate([w[..., _IN_OFFSETS[n][0]:_IN_OFFSETS[n][1]] for n in names], axis=-1)


def kernel(x, mem, positions, g_mix_pre, g_mix_post, w_in, b_f, lam_q1, lam_k1, lam_q2, lam_k2, g_diff_sub, g_sb_out, g_fox_out, w_out, g_x_pre, g_x_post, g_mem, w_cq, w_ckv, w_co, g_ffn_pre, g_ffn_post, w_gate, w_up, w_down):
    batch, seq, d = x.shape
    depth = w_in.shape[0]
    m = batch * seq
    d_ff = w_gate.shape[2]
    assert w_in.shape[2] == _IN_OFFSETS["fc"][1] and d == V_ROWS

    bm_row = _pick(m, 512, 256, 128)
    bm_mm = _pick(m, 1024, 512, 256, 128)
    bm_big = _pick(m, 2048, 1024, 512, 256, 128)
    lag_diff, lag_sb, lag_fox = 2, 2, 3
    bn_in = 2 * A_W
    bn_ff = _pick(d_ff, 512, 256, 128)
    bk_ff = _pick(d_ff, 1408, 1024, 512, 256, 128)
    tq = _pick(seq, 256, 128)
    bm_x = _pick(seq, 512, 256, 128)
    bm_mem = _pick(mem.shape[0] * mem.shape[1], 512, 256, 128)

    w_qk = _columns(w_in, QK_ORDER).astype(BF16)
    w_qk_rope = _columns(w_in, QK_ROPE_ORDER).astype(BF16)
    w_vt = jnp.swapaxes(_columns(w_in, V_ORDER), 1, 2).astype(BF16)
    gate_lane_head = jnp.arange(2 * GATE_LANES) % GATE_LANES // GATE_PARTS
    wf_rep = jnp.pad(_columns(w_in, ("fc",))[:, :, gate_lane_head],
                     ((0, 0), (0, 0), (0, LANES - 2 * GATE_LANES))).astype(BF16)
    bf_rep = jnp.pad(b_f[:, gate_lane_head], ((0, 0), (0, LANES - 2 * GATE_LANES))).reshape(depth, 1, LANES)
    w_out_b, w_cq_b, w_ckv_b, w_co_b = (w.astype(BF16) for w in (w_out, w_cq, w_ckv, w_co))
    w_gate_b, w_up_b, w_down_b = (w.astype(BF16) for w in (w_gate, w_up, w_down))
    lam_q = jnp.concatenate([lam_q1, lam_q2], axis=1).reshape(depth, 1, LANES)
    lam_k = jnp.concatenate([lam_k1, lam_k2], axis=1).reshape(depth, 1, LANES)

    def scales(*groups):
        return jnp.concatenate([jnp.full((w,), v, F32) for w, v in groups]).reshape(1, -1)

    colscale = scales((SB_W, HEAD_DIM ** -0.5), (SB_W, 1.0), (FOX_W, HEAD_DIM ** -0.5 * LOG2E), (FOX_W, 1.0))
    colscale_rope = scales((A_W, DIFF_QK_DIM ** -0.5 * LOG2E), (A_W, 1.0))

    x = x.reshape(m, d)
    mem2 = mem.reshape(-1, d)
    cos_t, sin_t = rope_tables(positions, bm=bm_row)
    h = prenorm(x, g_mix_pre[0], bm=bm_row)

    for l in range(depth):
        lam_init = 0.8 - 0.6 * math.exp(-0.3 * l)
        qk = in_proj(h, w_qk[l], colscale, None, bm=bm_big, bn=bn_in)
        qk_rope = in_proj(h, w_qk_rope[l], colscale_rope, (cos_t, sin_t), bm=bm_big, bn=bn_in)
        vt = in_proj_vt(h, w_vt[l], bm=bm_mm, bn=bn_in, tk=tq)
        fk, fq = forget_features(h, wf_rep[l], bf_rep[l], batch=batch, blk=tq)
        oa = diff_attention(qk_rope, vt, lam_q[l], lam_k[l], g_diff_sub[l].reshape(1, LANES), lam_init, batch=batch, tq=tq,
                            lag=lag_diff)
        ob = stick_breaking_attention(qk, vt, batch=batch, tq=tq, lag=lag_sb)
        oc = forgetting_attention(qk, vt, fk, fq, batch=batch, tq=tq, lag=lag_fox)
        x, h = mix_out(oa, ob, oc, x, w_out_b[l], g_sb_out[l], g_fox_out[l], g_mix_post[l], g_x_pre[l], bm=bm_row // 2)
        kv = norm_matmul(mem2, g_mem[l], w_ckv_b[l], bm=bm_mem)
        x, h = cross_attention(h, x, kv, w_cq_b[l], w_co_b[l], g_x_post[l], g_ffn_pre[l], batch=batch, bm=bm_x)
        a = ffn_up(h, w_gate_b[l], w_up_b[l], bm=bm_mm, bn=bn_ff)
        g_next = g_mix_pre[l + 1] if l + 1 < depth else None
        x, *rest = ffn_down(a, w_down_b[l], x, g_ffn_post[l], g_next, bm=bm_row, bk=bk_ff)
        h = rest[0] if rest else None
    return x.reshape(batch, seq, d)
```

```python
import functools
import math

import jax
import jax.numpy as jnp
from jax import lax
from jax.experimental import pallas as pl
from jax.experimental.pallas import tpu as pltpu

F32 = jnp.float32
BF16 = jnp.bfloat16

EPS = 1e-6
NEG_INF = -1e30
ROPE_THETA = 10000.0
LOG2E = math.log2(math.e)

LANES = 128
HEAD_DIM = 128
N_DIFF, N_SB, N_FOX = 4, 6, 6
DIFF_QK_DIM = HEAD_DIM // 2
A_W, SB_W, FOX_W = N_DIFF * HEAD_DIM, N_SB * HEAD_DIM, N_FOX * HEAD_DIM
V_ROWS = A_W + SB_W + FOX_W
N_CROSS_HEADS = 4
CROSS_W = N_CROSS_HEADS * HEAD_DIM
GATE_PARTS = 3
GATE_LANES = GATE_PARTS * N_FOX

VMEM_LIMIT = 56 * 1024 * 1024

_IN_OFFSETS = {}
_off = 0
for _name, _width in (("qa", A_W), ("ka", A_W), ("va", A_W), ("qb", SB_W), ("kb", SB_W), ("vb", SB_W),
                      ("qc", FOX_W), ("kc", FOX_W), ("vc", FOX_W), ("fc", N_FOX)):
    _IN_OFFSETS[_name] = (_off, _off + _width)
    _off += _width
QK_ORDER = ("qb", "kb", "qc", "kc")
QK_ROPE_ORDER = ("qa", "ka")
V_ORDER = ("vb", "vc", "va")
QB_COL, QC_COL, QA_COL = 0, 2, 0
VB_ROW, VC_ROW, VA_ROW = 0, 1, (2 * SB_W) // A_W


def _params(*semantics):
    return pltpu.CompilerParams(dimension_semantics=semantics, vmem_limit_bytes=VMEM_LIMIT)


def _rms(xf, g):
    return xf * lax.rsqrt(jnp.mean(xf * xf, axis=-1, keepdims=True) + EPS) * g


def _dot(a, b):
    return jnp.dot(a, b, preferred_element_type=F32)


def _dot_nt(a, b):
    return lax.dot_general(a, b, (((1,), (1,)), ((), ())), preferred_element_type=F32)


def _bf16_parts(x):
    p0 = x.astype(BF16)
    r1 = x - p0.astype(F32)
    p1 = r1.astype(BF16)
    p2 = (r1 - p1.astype(F32)).astype(BF16)
    return p0, p1, p2


def _prenorm_kernel(x_ref, g_ref, h_ref):
    h_ref[...] = _rms(x_ref[...], g_ref[...]).astype(BF16)


def prenorm(x, g, *, bm):
    m, d = x.shape
    return pl.pallas_call(
        _prenorm_kernel,
        grid=(m // bm,),
        in_specs=[pl.BlockSpec((bm, d), lambda i: (i, 0)), pl.BlockSpec((1, d), lambda i: (0, 0))],
        out_specs=pl.BlockSpec((bm, d), lambda i: (i, 0)),
        out_shape=jax.ShapeDtypeStruct((m, d), BF16),
        compiler_params=_params("parallel"),
        name="prenorm",
    )(x, g.reshape(1, d))


def _rope_table_kernel(pos_ref, invf_ref, sign_ref, cos_ref, sin_ref):
    ang = pos_ref[...].astype(F32) * invf_ref[...]
    cos_ref[...] = jnp.cos(ang)
    sin_ref[...] = jnp.sin(ang) * sign_ref[...]


def rope_tables(positions, *, bm):
    m = positions.size
    half = DIFF_QK_DIM // 2
    inv_freq = ROPE_THETA ** (-jnp.arange(half, dtype=F32) / half)
    invf = jnp.tile(inv_freq, LANES // half).reshape(1, LANES)
    sign = jnp.where((jnp.arange(LANES) % DIFF_QK_DIM) < half, -1.0, 1.0).astype(F32).reshape(1, LANES)
    row = pl.BlockSpec((bm, LANES), lambda i: (i, 0))
    const = pl.BlockSpec((1, LANES), lambda i: (0, 0))
    return pl.pallas_call(
        _rope_table_kernel,
        grid=(m // bm,),
        in_specs=[pl.BlockSpec((bm, 1), lambda i: (i, 0)), const, const],
        out_specs=[row, row],
        out_shape=[jax.ShapeDtypeStruct((m, LANES), F32)] * 2,
        compiler_params=_params("parallel"),
        name="rope_tables",
    )(positions.reshape(m, 1), invf, sign)


def _in_proj_kernel(h_ref, w_ref, cs_ref, o_ref):
    o_ref[...] = (_dot(h_ref[...], w_ref[...]) * cs_ref[...]).astype(BF16)


def _in_proj_rope_kernel(h_ref, w_ref, cs_ref, cos_ref, sin_ref, o_ref):
    acc = _dot(h_ref[...], w_ref[...]) * cs_ref[...]
    bm, bn = acc.shape
    c, s = cos_ref[...], sin_ref[...]
    first_half = (lax.broadcasted_iota(jnp.int32, (bm, LANES), 1) % DIFF_QK_DIM) < DIFF_QK_DIM // 2
    for t in range(bn // LANES):
        a = acc[:, t * LANES:(t + 1) * LANES]
        partner = jnp.where(first_half, pltpu.roll(a, LANES - DIFF_QK_DIM // 2, 1),
                            pltpu.roll(a, DIFF_QK_DIM // 2, 1))
        o_ref[:, t * LANES:(t + 1) * LANES] = (a * c + partner * s).astype(BF16)


def in_proj(h, w, colscale, rope_tables_or_none, *, bm, bn):
    m, d = h.shape
    n = w.shape[1]
    assert n % bn == 0 and m % bm == 0
    rope = rope_tables_or_none is not None
    row_tab = pl.BlockSpec((bm, LANES), lambda i, j: (i, 0))
    return pl.pallas_call(
        _in_proj_rope_kernel if rope else _in_proj_kernel,
        grid=(m // bm, n // bn),
        in_specs=[pl.BlockSpec((bm, d), lambda i, j: (i, 0)),
                  pl.BlockSpec((d, bn), lambda i, j: (0, j)),
                  pl.BlockSpec((1, bn), lambda i, j: (0, j))] + ([row_tab, row_tab] if rope else []),
        out_specs=pl.BlockSpec((bm, bn), lambda i, j: (i, j)),
        out_shape=jax.ShapeDtypeStruct((m, n), BF16),
        compiler_params=_params("parallel", "arbitrary"),
        name="in_proj_rope" if rope else "in_proj",
    )(h, w, colscale, *(rope_tables_or_none or ()))


def _in_proj_vt_kernel(h_ref, wt_ref, o_ref, *, tk):
    res = _dot_nt(wt_ref[...], h_ref[...]).astype(BF16)
    for c in range(o_ref.shape[0]):
        o_ref[c] = res[:, c * tk:(c + 1) * tk]


def in_proj_vt(h, wt, *, bm, bn, tk):
    m, d = h.shape
    n = wt.shape[0]
    return pl.pallas_call(
        functools.partial(_in_proj_vt_kernel, tk=tk),
        grid=(m // bm, n // bn),
        in_specs=[pl.BlockSpec((bm, d), lambda i, j: (i, 0)),
                  pl.BlockSpec((bn, d), lambda i, j: (j, 0))],
        out_specs=pl.BlockSpec((bm // tk, bn, tk), lambda i, j: (i, j, 0)),
        out_shape=jax.ShapeDtypeStruct((m // tk, n, tk), BF16),
        compiler_params=_params("parallel", "arbitrary"),
        name="in_proj_vt",
    )(h, wt)


def _forget_features_kernel(h_ref, wf_ref, bf_ref, part_ref, fk_ref, fq_ref, *, blk):
    s = h_ref.shape[0]
    lane = lax.broadcasted_iota(jnp.int32, (blk, LANES), 1)
    part = jnp.broadcast_to(part_ref[...], (blk, LANES))
    lower = (lax.broadcasted_iota(jnp.int32, (blk, blk), 0)
             >= lax.broadcasted_iota(jnp.int32, (blk, blk), 1)).astype(BF16)
    one = jnp.ones((blk, LANES), F32)
    zero = jnp.zeros((blk, LANES), F32)
    carry = jnp.zeros((1, LANES), F32)
    for t in range(s // blk):
        rows = slice(t * blk, (t + 1) * blk)
        fc = _dot(h_ref[rows, :], wf_ref[...]) + bf_ref[...]
        log_f = (jnp.minimum(fc, 0.0) - jnp.log1p(jnp.exp(-jnp.abs(fc)))) * LOG2E
        x0, x1, x2 = _bf16_parts(log_f)
        cum = _dot(lower, x0) + _dot(lower, x1) + _dot(lower, x2) + carry
        carry = cum[blk - 1:blk, :]
        c0, c1, c2 = (c.astype(F32) for c in _bf16_parts(cum))
        parts = jnp.where(part == 0, c0, jnp.where(part == 1, c1, c2))
        fk = jnp.where(lane < GATE_LANES, parts, jnp.where(lane < 2 * GATE_LANES, one, zero))
        fq = jnp.where(lane < GATE_LANES, -one, jnp.where(lane < 2 * GATE_LANES, parts, zero))
        fk_ref[rows, :] = fk.astype(BF16)
        fq_ref[rows, :] = fq.astype(BF16)


def forget_features(h, wf_rep, bf_rep, *, batch, blk):
    m, d = h.shape
    s = m // batch
    part = (jnp.arange(LANES, dtype=jnp.int32) % GATE_PARTS).reshape(1, LANES)
    rows = pl.BlockSpec((s, LANES), lambda b: (b, 0))
    vec = pl.BlockSpec((1, LANES), lambda b: (0, 0))
    return pl.pallas_call(
        functools.partial(_forget_features_kernel, blk=blk),
        grid=(batch,),
        in_specs=[pl.BlockSpec((s, d), lambda b: (b, 0)),
                  pl.BlockSpec((d, LANES), lambda b: (0, 0)), vec, vec],
        out_specs=[rows, rows],
        out_shape=[jax.ShapeDtypeStruct((m, LANES), BF16)] * 2,
        compiler_params=_params("parallel"),
        name="forget_features",
    )(h, wf_rep, bf_rep, part)


def _head(ref, h):
    return ref[:, h * HEAD_DIM:(h + 1) * HEAD_DIM]


def _key_block(ref, kb, tk, h):
    return ref[pl.ds(pl.multiple_of(kb * tk, tk), tk), h * HEAD_DIM:(h + 1) * HEAD_DIM]


def _value_block_t(ref, kb, h):
    return ref[kb, h * HEAD_DIM:(h + 1) * HEAD_DIM, :]


def _run_pipeline(i, n_heads, stages, lag, prepare):
    n = len(stages)
    lead = [(n - 1 - k) * lag for k in range(n)]
    assert lead[0] <= n_heads
    unprepared = set(range(n_heads))

    def positions(cur, nxt, cur_masked, first, last):
        for t in range(first, last):
            for k in range(n):
                idx = t + lead[k]
                if 0 <= idx < n_heads:
                    if k == 0 and idx in unprepared:
                        unprepared.discard(idx)
                        prepare(idx)
                    stages[k](cur, idx, cur_masked)
                elif idx >= n_heads and nxt is not None:
                    stages[k](nxt, idx - n_heads, False)

    positions(i, None, True, -lead[0], 0)
    positions(i, jnp.maximum(i - 1, 0), True, 0, n_heads)

    def body(j, carry):
        cur = i - j
        positions(cur, jnp.maximum(cur - 1, 0), False, 0, n_heads)
        return carry

    lax.fori_loop(1, i + 1, body, 0)


def _softmax_stages(scores, values_t, keep, m_ref, l_ref, acc_ref, s_buf, mn_buf, al_buf, ps_buf, pv_buf):
    def stage_scores(kb, h, masked):
        s_buf[h] = scores(kb, h)

    def stage_probabilities(kb, h, masked):
        s = s_buf[h]
        if masked:
            s = jnp.where(keep, s, NEG_INF)
        m_prev = m_ref[h]
        m_new = jnp.maximum(m_prev, jnp.max(s, axis=0, keepdims=True))
        p = jnp.exp2(s - m_new)
        mn_buf[h] = m_new
        al_buf[h] = jnp.exp2(m_prev - m_new)
        ps_buf[h] = jnp.sum(p, axis=0, keepdims=True)
        pv_buf[h] = _dot(values_t(kb, h), p.astype(BF16))

    def stage_commit(kb, h, masked):
        alpha = al_buf[h]
        m_ref[h] = mn_buf[h]
        l_ref[h] = alpha * l_ref[h] + ps_buf[h]
        acc_ref[h] = alpha * acc_ref[h] + pv_buf[h]

    return stage_scores, stage_probabilities, stage_commit


def _softmax_scratch(n_heads, tk, nq):
    stat = pltpu.VMEM((n_heads, 1, nq), F32)
    wide = pltpu.VMEM((n_heads, HEAD_DIM, nq), F32)
    return [stat, stat, wide,
            pltpu.VMEM((n_heads, tk, nq), F32), stat, stat, stat, wide]


def _init_softmax_stats(m_ref, l_ref, acc_ref):
    m_ref[...] = jnp.full(m_ref.shape, NEG_INF, F32)
    l_ref[...] = jnp.zeros(l_ref.shape, F32)
    acc_ref[...] = jnp.zeros(acc_ref.shape, F32)


def _group_specs(s, tq, width, q_col, v_row):
    nq = s // tq
    q_spec = pl.BlockSpec((tq, width), lambda b, i: (b * nq + i, q_col))
    k_spec = pl.BlockSpec((s, width), lambda b, i: (b, q_col + 1))
    vt_spec = pl.BlockSpec((nq, width, tq), lambda b, i: (b, v_row, 0))
    o_spec = pl.BlockSpec((tq, width), lambda b, i: (b * nq + i, 0))
    return q_spec, k_spec, vt_spec, o_spec


def _diff_attn_kernel(lq_ref, lk_ref, g_ref, q_ref, k_ref, vt_ref, o_ref, qq_ref, m_ref, l_ref, acc_ref,
                      *bufs, tq, lam_init, lag):
    i = pl.program_id(1)
    low = lax.broadcasted_iota(jnp.int32, (HEAD_DIM, tq), 0) < DIFF_QK_DIM

    def prepare(h):
        q_t = _head(q_ref, h).astype(F32).T
        qq_ref[h, :, 0:tq] = jnp.where(low, q_t, 0.0).astype(BF16)
        qq_ref[h, :, tq:2 * tq] = jnp.where(low, 0.0, q_t).astype(BF16)

    _init_softmax_stats(m_ref, l_ref, acc_ref)

    key = lax.broadcasted_iota(jnp.int32, (tq, 2 * tq), 0)
    qry = lax.broadcasted_iota(jnp.int32, (tq, 2 * tq), 1)
    keep = jnp.where(qry >= tq, qry - tq, qry) >= key

    def scores(kb, h):
        return _dot(_key_block(k_ref, kb, tq, h), qq_ref[h])

    stages = _softmax_stages(scores, lambda kb, h: _value_block_t(vt_ref, kb, h), keep,
                             m_ref, l_ref, acc_ref, *bufs)
    _run_pipeline(i, N_DIFF, stages, lag, prepare)

    prod = lq_ref[...] * lk_ref[...]
    first = lax.broadcasted_iota(jnp.int32, prod.shape, 1) < DIFF_QK_DIM
    e1 = jnp.exp(jnp.sum(jnp.where(first, prod, 0.0), axis=1, keepdims=True))
    e2 = jnp.exp(jnp.sum(jnp.where(first, 0.0, prod), axis=1, keepdims=True))
    lam = (e1 - e2) + lam_init
    for h in range(N_DIFF):
        o_t = (acc_ref[h, :, 0:tq] / l_ref[h, :, 0:tq]
               - lam * (acc_ref[h, :, tq:2 * tq] / l_ref[h, :, tq:2 * tq]))
        o_ref[:, h * HEAD_DIM:(h + 1) * HEAD_DIM] = (_rms(o_t.T, g_ref[...]) * (1.0 - lam_init)).astype(BF16)


def diff_attention(qk, vt, lam_q, lam_k, g_sub, lam_init, *, batch, tq, lag):
    m = qk.shape[0]
    s = m // batch
    vec = pl.BlockSpec((1, LANES), lambda b, i: (0, 0))
    q_spec, k_spec, vt_spec, o_spec = _group_specs(s, tq, A_W, QA_COL, VA_ROW)
    return pl.pallas_call(
        functools.partial(_diff_attn_kernel, tq=tq, lam_init=lam_init, lag=lag),
        grid=(batch, s // tq),
        in_specs=[vec, vec, vec, q_spec, k_spec, vt_spec],
        out_specs=o_spec,
        out_shape=jax.ShapeDtypeStruct((m, A_W), BF16),
        scratch_shapes=[pltpu.VMEM((N_DIFF, HEAD_DIM, 2 * tq), BF16)] + _softmax_scratch(N_DIFF, tq, 2 * tq),
        compiler_params=_params("parallel", "arbitrary"),
        name="diff_attention",
    )(lam_q, lam_k, g_sub, qk, qk, vt)


def _sb_attn_kernel(q_ref, k_ref, vt_ref, o_ref, qn_ref, r_ref, acc_ref,
                    zn_buf, after_buf, cs_buf, pv_buf, *, tq, lag):
    i = pl.program_id(1)
    def prepare(h):
        qn_ref[h] = (-_head(q_ref, h).astype(F32)).T.astype(BF16)

    r_ref[...] = jnp.zeros(r_ref.shape, F32)
    acc_ref[...] = jnp.zeros(acc_ref.shape, F32)
    ss = lax.broadcasted_iota(jnp.int32, (tq, 2 * tq), 0)
    jj = lax.broadcasted_iota(jnp.int32, (tq, 2 * tq), 1)
    tri = (jnp.where(jj >= tq, jj - tq, jj) >= ss).astype(BF16)

    strict = lax.broadcasted_iota(jnp.int32, (tq, tq), 0) < lax.broadcasted_iota(jnp.int32, (tq, tq), 1)

    def scores(kb, h, masked):
        zn_buf[h] = _dot(_key_block(k_ref, kb, tq, h), qn_ref[h])

    def suffix_sums(kb, h, masked):
        zn = zn_buf[h]
        sp = jnp.log(1.0 + jnp.exp2(jnp.abs(zn) * -LOG2E))
        log_remain = jnp.minimum(zn, 0.0) - sp
        if masked:
            log_remain = jnp.where(strict, log_remain, 0.0)
        hi = log_remain.astype(BF16)
        lo = (log_remain - hi.astype(F32)).astype(BF16)
        after_buf[h] = _dot(tri, jnp.concatenate([hi, lo], axis=0))
        cs_buf[h] = jnp.sum(log_remain, axis=0, keepdims=True)

    def weights(kb, h, masked):
        w = jnp.exp((after_buf[h] + r_ref[h]) - zn_buf[h])
        if masked:
            w = jnp.where(strict, w, 0.0)
        pv_buf[h] = _dot(_value_block_t(vt_ref, kb, h), w.astype(BF16))
        r_ref[h] += cs_buf[h]

    def accumulate(kb, h, masked):
        acc_ref[h] += pv_buf[h]

    _run_pipeline(i, N_SB, (scores, suffix_sums, weights, accumulate), lag, prepare)
    for h in range(N_SB):
        o_ref[:, h * HEAD_DIM:(h + 1) * HEAD_DIM] = acc_ref[h].T.astype(BF16)


def stick_breaking_attention(qk, vt, *, batch, tq, lag):
    m = qk.shape[0]
    s = m // batch
    q_spec, k_spec, vt_spec, o_spec = _group_specs(s, tq, SB_W, QB_COL, VB_ROW)
    stat = pltpu.VMEM((N_SB, 1, tq), F32)
    wide = pltpu.VMEM((N_SB, HEAD_DIM, tq), F32)
    tile = pltpu.VMEM((N_SB, tq, tq), F32)
    return pl.pallas_call(
        functools.partial(_sb_attn_kernel, tq=tq, lag=lag),
        grid=(batch, s // tq),
        in_specs=[q_spec, k_spec, vt_spec],
        out_specs=o_spec,
        out_shape=jax.ShapeDtypeStruct((m, SB_W), BF16),
        scratch_shapes=[pltpu.VMEM((N_SB, HEAD_DIM, tq), BF16), stat, wide,
                        tile, tile, stat, wide],
        compiler_params=_params("parallel", "arbitrary"),
        name="stick_breaking_attention",
    )(qk, qk, vt)


def _fox_attn_kernel(fq_ref, fk_ref, q_ref, k_ref, vt_ref, o_ref, qa_ref, m_ref, l_ref, acc_ref, *bufs, tq, lag):
    i = pl.program_id(1)
    feat = lax.broadcasted_iota(jnp.int32, (LANES, tq), 0)
    fq_t = fq_ref[...].astype(F32).T

    def prepare(h):
        lo = GATE_PARTS * h
        mine = (((feat >= lo) & (feat < lo + GATE_PARTS))
                | ((feat >= GATE_LANES + lo) & (feat < GATE_LANES + lo + GATE_PARTS)))
        qa_ref[h, 0:HEAD_DIM, :] = _head(q_ref, h).astype(F32).T.astype(BF16)
        qa_ref[h, HEAD_DIM:2 * HEAD_DIM, :] = jnp.where(mine, fq_t, 0.0).astype(BF16)

    _init_softmax_stats(m_ref, l_ref, acc_ref)

    keep = lax.broadcasted_iota(jnp.int32, (tq, tq), 1) >= lax.broadcasted_iota(jnp.int32, (tq, tq), 0)

    def scores(kb, h):
        fk = fk_ref[pl.ds(pl.multiple_of(kb * tq, tq), tq), :]
        return _dot(jnp.concatenate([_key_block(k_ref, kb, tq, h), fk], axis=1), qa_ref[h])

    stages = _softmax_stages(scores, lambda kb, h: _value_block_t(vt_ref, kb, h), keep,
                             m_ref, l_ref, acc_ref, *bufs)
    _run_pipeline(i, N_FOX, stages, lag, prepare)
    for h in range(N_FOX):
        o_ref[:, h * HEAD_DIM:(h + 1) * HEAD_DIM] = (acc_ref[h] / l_ref[h]).T.astype(BF16)


def forgetting_attention(qk, vt, fk, fq, *, batch, tq, lag):
    m = qk.shape[0]
    s = m // batch
    nq = s // tq
    q_spec, k_spec, vt_spec, o_spec = _group_specs(s, tq, FOX_W, QC_COL, VC_ROW)
    return pl.pallas_call(
        functools.partial(_fox_attn_kernel, tq=tq, lag=lag),
        grid=(batch, nq),
        in_specs=[pl.BlockSpec((tq, LANES), lambda b, i: (b * nq + i, 0)),
                  pl.BlockSpec((s, LANES), lambda b, i: (b, 0)),
                  q_spec, k_spec, vt_spec],
        out_specs=o_spec,
        out_shape=jax.ShapeDtypeStruct((m, FOX_W), BF16),
        scratch_shapes=[pltpu.VMEM((N_FOX, 2 * HEAD_DIM, tq), BF16)] + _softmax_scratch(N_FOX, tq, tq),
        compiler_params=_params("parallel", "arbitrary"),
        name="forgetting_attention",
    )(fq, fk, qk, qk, vt)


def _residual_tail(y, x, g_post, g_next, x_out_ref, h_out_ref):
    x_new = x + _rms(y, g_post)
    x_out_ref[...] = x_new
    if h_out_ref is not None:
        h_out_ref[...] = _rms(x_new, g_next).astype(BF16)


def _mix_out_kernel(oa_ref, ob_ref, oc_ref, x_ref, w_ref, gb_ref, gc_ref, gp_ref, gn_ref, xo_ref, ho_ref):
    ob = _rms(ob_ref[...].astype(F32), gb_ref[...]).astype(BF16)
    oc = _rms(oc_ref[...].astype(F32), gc_ref[...]).astype(BF16)
    y = (_dot(oa_ref[...], w_ref[0:A_W, :]) + _dot(ob, w_ref[A_W:A_W + SB_W, :])
         + _dot(oc, w_ref[A_W + SB_W:, :]))
    _residual_tail(y, x_ref[...], gp_ref[...], gn_ref[...], xo_ref, ho_ref)


def mix_out(oa, ob, oc, x, w, g_sb, g_fox, g_post, g_next, *, bm):
    m, d = x.shape
    rows = lambda width: pl.BlockSpec((bm, width), lambda i: (i, 0))
    const = lambda width: pl.BlockSpec((1, width), lambda i: (0, 0))
    return pl.pallas_call(
        _mix_out_kernel,
        grid=(m // bm,),
        in_specs=[rows(A_W), rows(SB_W), rows(FOX_W), rows(d),
                  pl.BlockSpec(w.shape, lambda i: (0, 0)),
                  const(SB_W), const(FOX_W), const(d), const(d)],
        out_specs=[rows(d), rows(d)],
        out_shape=[jax.ShapeDtypeStruct((m, d), F32), jax.ShapeDtypeStruct((m, d), BF16)],
        compiler_params=_params("parallel"),
        name="mix_out",
    )(oa, ob, oc, x, w, g_sb.reshape(1, -1), g_fox.reshape(1, -1), g_post.reshape(1, -1), g_next.reshape(1, -1))


def _norm_matmul_kernel(x_ref, g_ref, w_ref, o_ref):
    o_ref[...] = _dot(_rms(x_ref[...], g_ref[...]).astype(BF16), w_ref[...]).astype(BF16)


def norm_matmul(x, g, w, *, bm):
    m, d = x.shape
    n = w.shape[1]
    return pl.pallas_call(
        _norm_matmul_kernel,
        grid=(m // bm,),
        in_specs=[pl.BlockSpec((bm, d), lambda i: (i, 0)), pl.BlockSpec((1, d), lambda i: (0, 0)),
                  pl.BlockSpec((d, n), lambda i: (0, 0))],
        out_specs=pl.BlockSpec((bm, n), lambda i: (i, 0)),
        out_shape=jax.ShapeDtypeStruct((m, n), BF16),
        compiler_params=_params("parallel"),
        name="memory_kv",
    )(x, g.reshape(1, d), w)


def _cross_attn_kernel(h_ref, x_ref, kv_ref, wq_ref, wo_ref, gp_ref, gn_ref, xo_ref, ho_ref):
    q = (_dot(h_ref[...], wq_ref[...]) * HEAD_DIM ** -0.5).astype(BF16)
    heads = []
    for hd in range(N_CROSS_HEADS):
        k = kv_ref[:, hd * HEAD_DIM:(hd + 1) * HEAD_DIM]
        v = kv_ref[:, CROSS_W + hd * HEAD_DIM:CROSS_W + (hd + 1) * HEAD_DIM]
        s = _dot_nt(q[:, hd * HEAD_DIM:(hd + 1) * HEAD_DIM], k)
        p = jnp.exp(s - jnp.max(s, axis=1, keepdims=True))
        o = _dot(p.astype(BF16), v) / jnp.sum(p, axis=1, keepdims=True)
        heads.append(o.astype(BF16))
    y = _dot(jnp.concatenate(heads, axis=1), wo_ref[...])
    _residual_tail(y, x_ref[...], gp_ref[...], gn_ref[...], xo_ref, ho_ref)


def cross_attention(h, x, kv, wq, wo, g_post, g_next, *, batch, bm):
    m, d = x.shape
    s = m // batch
    mem_len = kv.shape[0] // batch
    nb = s // bm
    rows = lambda width: pl.BlockSpec((bm, width), lambda b, i: (b * nb + i, 0))
    const = lambda shape: pl.BlockSpec(shape, lambda b, i: (0, 0))
    return pl.pallas_call(
        _cross_attn_kernel,
        grid=(batch, nb),
        in_specs=[rows(d), rows(d),
                  pl.BlockSpec((mem_len, 2 * CROSS_W), lambda b, i: (b, 0)),
                  const(wq.shape), const(wo.shape), const((1, d)), const((1, d))],
        out_specs=[rows(d), rows(d)],
        out_shape=[jax.ShapeDtypeStruct((m, d), F32), jax.ShapeDtypeStruct((m, d), BF16)],
        compiler_params=_params("parallel", "parallel"),
        name="cross_attention",
    )(h, x, kv, wq, wo, g_post.reshape(1, d), g_next.reshape(1, d))


def _ffn_up_kernel(h_ref, wg_ref, wu_ref, o_ref):
    h = h_ref[...]
    gate = _dot(h, wg_ref[...])
    up = _dot(h, wu_ref[...])
    o_ref[...] = (gate * jax.nn.sigmoid(gate) * up).astype(BF16)


def ffn_up(h, wg, wu, *, bm, bn):
    m, d = h.shape
    n = wg.shape[1]
    wspec = pl.BlockSpec((d, bn), lambda i, j: (0, j))
    return pl.pallas_call(
        _ffn_up_kernel,
        grid=(m // bm, n // bn),
        in_specs=[pl.BlockSpec((bm, d), lambda i, j: (i, 0)), wspec, wspec],
        out_specs=pl.BlockSpec((bm, bn), lambda i, j: (i, j)),
        out_shape=jax.ShapeDtypeStruct((m, n), BF16),
        compiler_params=_params("parallel", "arbitrary"),
        name="ffn_up",
    )(h, wg, wu)


def _ffn_down_kernel(*refs, emit_h):
    if emit_h:
        a_ref, w_ref, x_ref, gp_ref, gn_ref, xo_ref, ho_ref = refs
    else:
        a_ref, w_ref, x_ref, gp_ref, xo_ref = refs
        gn_ref = ho_ref = None
    kk = pl.program_id(1)
    last = pl.num_programs(1) - 1
    part = _dot(a_ref[...], w_ref[...])

    @pl.when(kk == 0)
    def _():
        xo_ref[...] = part

    @pl.when((kk > 0) & (kk < last))
    def _():
        xo_ref[...] += part

    @pl.when(kk == last)
    def _():
        g_next = gn_ref[...] if emit_h else None
        _residual_tail(xo_ref[...] + part, x_ref[...], gp_ref[...], g_next, xo_ref, ho_ref)


def ffn_down(a, w, x, g_post, g_next, *, bm, bk):
    m, d = x.shape
    kdim = a.shape[1]
    emit_h = g_next is not None
    assert kdim // bk >= 2
    rows = pl.BlockSpec((bm, d), lambda i, k: (i, 0))
    const = pl.BlockSpec((1, d), lambda i, k: (0, 0))
    gains = [g_post.reshape(1, d)] + ([g_next.reshape(1, d)] if emit_h else [])
    out_shape = [jax.ShapeDtypeStruct((m, d), F32)] + ([jax.ShapeDtypeStruct((m, d), BF16)] if emit_h else [])
    return pl.pallas_call(
        functools.partial(_ffn_down_kernel, emit_h=emit_h),
        grid=(m // bm, kdim // bk),
        in_specs=[pl.BlockSpec((bm, bk), lambda i, k: (i, k)),
                  pl.BlockSpec((bk, d), lambda i, k: (k, 0)),
                  rows] + [const] * len(gains),
        out_specs=[rows] * len(out_shape),
        out_shape=out_shape,
        compiler_params=_params("parallel", "arbitrary"),
        name="ffn_down",
    )(a, w, x, *gains)


def _pick(n, *candidates):
    for c in candidates:
        if n % c == 0:
            return c
    return n


def _columns(w, names):
    return jnp.concatenate([w[..., _IN_OFFSETS[n][0]:_IN_OFFSETS[n][1]] for n in names], axis=-1)


def kernel(x, mem, positions, g_mix_pre, g_mix_post, w_in, b_f, lam_q1, lam_k1, lam_q2, lam_k2, g_diff_sub, g_sb_out, g_fox_out, w_out, g_x_pre, g_x_post, g_mem, w_cq, w_ckv, w_co, g_ffn_pre, g_ffn_post, w_gate, w_up, w_down):
    batch, seq, d = x.shape
    depth = w_in.shape[0]
    m = batch * seq
    d_ff = w_gate.shape[2]
    assert w_in.shape[2] == _IN_OFFSETS["fc"][1] and d == V_ROWS

    bm_row = _pick(m, 512, 256, 128)
    bm_mm = _pick(m, 1024, 512, 256, 128)
    bm_big = _pick(m, 2048, 1024, 512, 256, 128)
    lag_diff, lag_sb, lag_fox = 2, 2, 3
    bn_in = 2 * A_W
    bn_ff = _pick(d_ff, 512, 256, 128)
    bk_ff = _pick(d_ff, 2816, 1408, 1024, 512, 256, 128)
    tq = _pick(seq, 256, 128)
    bm_x = _pick(seq, 512, 256, 128)
    bm_mem = _pick(mem.shape[0] * mem.shape[1], 512, 256, 128)

    w_qk = _columns(w_in, QK_ORDER).astype(BF16)
    w_qk_rope = _columns(w_in, QK_ROPE_ORDER).astype(BF16)
    w_vt = jnp.swapaxes(_columns(w_in, V_ORDER), 1, 2).astype(BF16)
    gate_lane_head = jnp.arange(2 * GATE_LANES) % GATE_LANES // GATE_PARTS
    wf_rep = jnp.pad(_columns(w_in, ("fc",))[:, :, gate_lane_head],
                     ((0, 0), (0, 0), (0, LANES - 2 * GATE_LANES))).astype(BF16)
    bf_rep = jnp.pad(b_f[:, gate_lane_head], ((0, 0), (0, LANES - 2 * GATE_LANES))).reshape(depth, 1, LANES)
    w_out_b, w_cq_b, w_ckv_b, w_co_b = (w.astype(BF16) for w in (w_out, w_cq, w_ckv, w_co))
    w_gate_b, w_up_b, w_down_b = (w.astype(BF16) for w in (w_gate, w_up, w_down))
    lam_q = jnp.concatenate([lam_q1, lam_q2], axis=1).reshape(depth, 1, LANES)
    lam_k = jnp.concatenate([lam_k1, lam_k2], axis=1).reshape(depth, 1, LANES)

    def scales(*groups):
        return jnp.concatenate([jnp.full((w,), v, F32) for w, v in groups]).reshape(1, -1)

    colscale = scales((SB_W, HEAD_DIM ** -0.5), (SB_W, 1.0), (FOX_W, HEAD_DIM ** -0.5 * LOG2E), (FOX_W, 1.0))
    colscale_rope = scales((A_W, DIFF_QK_DIM ** -0.5 * LOG2E), (A_W, 1.0))

    x = x.reshape(m, d)
    mem2 = mem.reshape(-1, d)
    cos_t, sin_t = rope_tables(positions, bm=bm_row)
    h = prenorm(x, g_mix_pre[0], bm=bm_row)

    for l in range(depth):
        lam_init = 0.8 - 0.6 * math.exp(-0.3 * l)
        qk = in_proj(h, w_qk[l], colscale, None, bm=bm_big, bn=bn_in)
        qk_rope = in_proj(h, w_qk_rope[l], colscale_rope, (cos_t, sin_t), bm=bm_big, bn=bn_in)
        vt = in_proj_vt(h, w_vt[l], bm=bm_mm, bn=bn_in, tk=tq)
        fk, fq = forget_features(h, wf_rep[l], bf_rep[l], batch=batch, blk=tq)
        oa = diff_attention(qk_rope, vt, lam_q[l], lam_k[l], g_diff_sub[l].reshape(1, LANES), lam_init, batch=batch, tq=tq,
                            lag=lag_diff)
        ob = stick_breaking_attention(qk, vt, batch=batch, tq=tq, lag=lag_sb)
        oc = forgetting_attention(qk, vt, fk, fq, batch=batch, tq=tq, lag=lag_fox)
        x, h = mix_out(oa, ob, oc, x, w_out_b[l], g_sb_out[l], g_fox_out[l], g_mix_post[l], g_x_pre[l], bm=bm_row // 2)
        kv = norm_matmul(mem2, g_mem[l], w_ckv_b[l], bm=bm_mem)
        x, h = cross_attention(h, x, kv, w_cq_b[l], w_co_b[l], g_x_post[l], g_ffn_pre[l], batch=batch, bm=bm_x)
        a = ffn_up(h, w_gate_b[l], w_up_b[l], bm=bm_mm, bn=bn_ff)
        g_next = g_mix_pre[l + 1] if l + 1 < depth else None
        x, *rest = ffn_down(a, w_down_b[l], x, g_ffn_post[l], g_next, bm=bm_row, bk=bk_ff)
        h = rest[0] if rest else None
    return x.reshape(batch, seq, d)
```

```python
import functools
import math

import jax
import jax.numpy as jnp
from jax import lax
from jax.experimental import pallas as pl
from jax.experimental.pallas import tpu as pltpu

F32 = jnp.float32
BF16 = jnp.bfloat16

EPS = 1e-6
NEG_INF = -1e30
ROPE_THETA = 10000.0
LOG2E = math.log2(math.e)

LANES = 128
HEAD_DIM = 128
N_DIFF, N_SB, N_FOX = 4, 6, 6
DIFF_QK_DIM = HEAD_DIM // 2
A_W, SB_W, FOX_W = N_DIFF * HEAD_DIM, N_SB * HEAD_DIM, N_FOX * HEAD_DIM
V_ROWS = A_W + SB_W + FOX_W
N_CROSS_HEADS = 4
CROSS_W = N_CROSS_HEADS * HEAD_DIM
GATE_PARTS = 3
GATE_LANES = GATE_PARTS * N_FOX

VMEM_LIMIT = 56 * 1024 * 1024

_IN_OFFSETS = {}
_off = 0
for _name, _width in (("qa", A_W), ("ka", A_W), ("va", A_W), ("qb", SB_W), ("kb", SB_W), ("vb", SB_W),
                      ("qc", FOX_W), ("kc", FOX_W), ("vc", FOX_W), ("fc", N_FOX)):
    _IN_OFFSETS[_name] = (_off, _off + _width)
    _off += _width
QK_ORDER = ("qb", "kb", "qc", "kc")
QK_ROPE_ORDER = ("qa", "ka")
V_ORDER = ("vb", "vc", "va")
QB_COL, QC_COL, QA_COL = 0, 2, 0
VB_ROW, VC_ROW, VA_ROW = 0, 1, (2 * SB_W) // A_W


def _params(*semantics):
    return pltpu.CompilerParams(dimension_semantics=semantics, vmem_limit_bytes=VMEM_LIMIT)


def _rms(xf, g):
    return xf * lax.rsqrt(jnp.mean(xf * xf, axis=-1, keepdims=True) + EPS) * g


def _dot(a, b):
    return jnp.dot(a, b, preferred_element_type=F32)


def _dot_nt(a, b):
    return lax.dot_general(a, b, (((1,), (1,)), ((), ())), preferred_element_type=F32)


def _bf16_parts(x):
    p0 = x.astype(BF16)
    r1 = x - p0.astype(F32)
    p1 = r1.astype(BF16)
    p2 = (r1 - p1.astype(F32)).astype(BF16)
    return p0, p1, p2


def _prenorm_kernel(x_ref, g_ref, h_ref):
    h_ref[...] = _rms(x_ref[...], g_ref[...]).astype(BF16)


def prenorm(x, g, *, bm):
    m, d = x.shape
    return pl.pallas_call(
        _prenorm_kernel,
        grid=(m // bm,),
        in_specs=[pl.BlockSpec((bm, d), lambda i: (i, 0)), pl.BlockSpec((1, d), lambda i: (0, 0))],
        out_specs=pl.BlockSpec((bm, d), lambda i: (i, 0)),
        out_shape=jax.ShapeDtypeStruct((m, d), BF16),
        compiler_params=_params("parallel"),
        name="prenorm",
    )(x, g.reshape(1, d))


def _rope_table_kernel(pos_ref, invf_ref, sign_ref, cos_ref, sin_ref):
    ang = pos_ref[...].astype(F32) * invf_ref[...]
    cos_ref[...] = jnp.cos(ang)
    sin_ref[...] = jnp.sin(ang) * sign_ref[...]


def rope_tables(positions, *, bm):
    m = positions.size
    half = DIFF_QK_DIM // 2
    inv_freq = ROPE_THETA ** (-jnp.arange(half, dtype=F32) / half)
    invf = jnp.tile(inv_freq, LANES // half).reshape(1, LANES)
    sign = jnp.where((jnp.arange(LANES) % DIFF_QK_DIM) < half, -1.0, 1.0).astype(F32).reshape(1, LANES)
    row = pl.BlockSpec((bm, LANES), lambda i: (i, 0))
    const = pl.BlockSpec((1, LANES), lambda i: (0, 0))
    return pl.pallas_call(
        _rope_table_kernel,
        grid=(m // bm,),
        in_specs=[pl.BlockSpec((bm, 1), lambda i: (i, 0)), const, const],
        out_specs=[row, row],
        out_shape=[jax.ShapeDtypeStruct((m, LANES), F32)] * 2,
        compiler_params=_params("parallel"),
        name="rope_tables",
    )(positions.reshape(m, 1), invf, sign)


def _in_proj_kernel(h_ref, w_ref, cs_ref, o_ref):
    o_ref[...] = (_dot(h_ref[...], w_ref[...]) * cs_ref[...]).astype(BF16)


def _in_proj_rope_kernel(h_ref, w_ref, cs_ref, cos_ref, sin_ref, o_ref):
    acc = _dot(h_ref[...], w_ref[...]) * cs_ref[...]
    bm, bn = acc.shape
    c, s = cos_ref[...], sin_ref[...]
    first_half = (lax.broadcasted_iota(jnp.int32, (bm, LANES), 1) % DIFF_QK_DIM) < DIFF_QK_DIM // 2
    for t in range(bn // LANES):
        a = acc[:, t * LANES:(t + 1) * LANES]
        partner = jnp.where(first_half, pltpu.roll(a, LANES - DIFF_QK_DIM // 2, 1),
                            pltpu.roll(a, DIFF_QK_DIM // 2, 1))
        o_ref[:, t * LANES:(t + 1) * LANES] = (a * c + partner * s).astype(BF16)


def in_proj(h, w, colscale, rope_tables_or_none, *, bm, bn):
    m, d = h.shape
    n = w.shape[1]
    assert n % bn == 0 and m % bm == 0
    rope = rope_tables_or_none is not None
    row_tab = pl.BlockSpec((bm, LANES), lambda i, j: (i, 0))
    return pl.pallas_call(
        _in_proj_rope_kernel if rope else _in_proj_kernel,
        grid=(m // bm, n // bn),
        in_specs=[pl.BlockSpec((bm, d), lambda i, j: (i, 0)),
                  pl.BlockSpec((d, bn), lambda i, j: (0, j)),
                  pl.BlockSpec((1, bn), lambda i, j: (0, j))] + ([row_tab, row_tab] if rope else []),
        out_specs=pl.BlockSpec((bm, bn), lambda i, j: (i, j)),
        out_shape=jax.ShapeDtypeStruct((m, n), BF16),
        compiler_params=_params("parallel", "arbitrary"),
        name="in_proj_rope" if rope else "in_proj",
    )(h, w, colscale, *(rope_tables_or_none or ()))


def _in_proj_vt_kernel(h_ref, wt_ref, o_ref, *, tk):
    res = _dot_nt(wt_ref[...], h_ref[...]).astype(BF16)
    for c in range(o_ref.shape[0]):
        o_ref[c] = res[:, c * tk:(c + 1) * tk]


def in_proj_vt(h, wt, *, bm, bn, tk):
    m, d = h.shape
    n = wt.shape[0]
    return pl.pallas_call(
        functools.partial(_in_proj_vt_kernel, tk=tk),
        grid=(m // bm, n // bn),
        in_specs=[pl.BlockSpec((bm, d), lambda i, j: (i, 0)),
                  pl.BlockSpec((bn, d), lambda i, j: (j, 0))],
        out_specs=pl.BlockSpec((bm // tk, bn, tk), lambda i, j: (i, j, 0)),
        out_shape=jax.ShapeDtypeStruct((m // tk, n, tk), BF16),
        compiler_params=_params("parallel", "arbitrary"),
        name="in_proj_vt",
    )(h, wt)


def _forget_features_kernel(h_ref, wf_ref, bf_ref, part_ref, fk_ref, fq_ref, *, blk):
    s = h_ref.shape[0]
    lane = lax.broadcasted_iota(jnp.int32, (blk, LANES), 1)
    part = jnp.broadcast_to(part_ref[...], (blk, LANES))
    lower = (lax.broadcasted_iota(jnp.int32, (blk, blk), 0)
             >= lax.broadcasted_iota(jnp.int32, (blk, blk), 1)).astype(BF16)
    one = jnp.ones((blk, LANES), F32)
    zero = jnp.zeros((blk, LANES), F32)
    carry = jnp.zeros((1, LANES), F32)
    for t in range(s // blk):
        rows = slice(t * blk, (t + 1) * blk)
        fc = _dot(h_ref[rows, :], wf_ref[...]) + bf_ref[...]
        log_f = (jnp.minimum(fc, 0.0) - jnp.log1p(jnp.exp(-jnp.abs(fc)))) * LOG2E
        x0, x1, x2 = _bf16_parts(log_f)
        cum = _dot(lower, x0) + _dot(lower, x1) + _dot(lower, x2) + carry
        carry = cum[blk - 1:blk, :]
        c0, c1, c2 = (c.astype(F32) for c in _bf16_parts(cum))
        parts = jnp.where(part == 0, c0, jnp.where(part == 1, c1, c2))
        fk = jnp.where(lane < GATE_LANES, parts, jnp.where(lane < 2 * GATE_LANES, one, zero))
        fq = jnp.where(lane < GATE_LANES, -one, jnp.where(lane < 2 * GATE_LANES, parts, zero))
        fk_ref[rows, :] = fk.astype(BF16)
        fq_ref[rows, :] = fq.astype(BF16)


def forget_features(h, wf_rep, bf_rep, *, batch, blk):
    m, d = h.shape
    s = m // batch
    part = (jnp.arange(LANES, dtype=jnp.int32) % GATE_PARTS).reshape(1, LANES)
    rows = pl.BlockSpec((s, LANES), lambda b: (b, 0))
    vec = pl.BlockSpec((1, LANES), lambda b: (0, 0))
    return pl.pallas_call(
        functools.partial(_forget_features_kernel, blk=blk),
        grid=(batch,),
        in_specs=[pl.BlockSpec((s, d), lambda b: (b, 0)),
                  pl.BlockSpec((d, LANES), lambda b: (0, 0)), vec, vec],
        out_specs=[rows, rows],
        out_shape=[jax.ShapeDtypeStruct((m, LANES), BF16)] * 2,
        compiler_params=_params("parallel"),
        name="forget_features",
    )(h, wf_rep, bf_rep, part)


def _head(ref, h):
    return ref[:, h * HEAD_DIM:(h + 1) * HEAD_DIM]


def _key_block(ref, kb, tk, h):
    return ref[pl.ds(pl.multiple_of(kb * tk, tk), tk), h * HEAD_DIM:(h + 1) * HEAD_DIM]


def _value_block_t(ref, kb, h):
    return ref[kb, h * HEAD_DIM:(h + 1) * HEAD_DIM, :]


def _run_pipeline(i, n_heads, stages, lag, prepare):
    n = len(stages)
    lead = [(n - 1 - k) * lag for k in range(n)]
    assert lead[0] <= n_heads
    unprepared = set(range(n_heads))

    def positions(cur, nxt, cur_masked, first, last):
        for t in range(first, last):
            for k in range(n):
                idx = t + lead[k]
                if 0 <= idx < n_heads:
                    if k == 0 and idx in unprepared:
                        unprepared.discard(idx)
                        prepare(idx)
                    stages[k](cur, idx, cur_masked)
                elif idx >= n_heads and nxt is not None:
                    stages[k](nxt, idx - n_heads, False)

    positions(i, None, True, -lead[0], 0)
    positions(i, jnp.maximum(i - 1, 0), True, 0, n_heads)

    def body(j, carry):
        cur = i - j
        positions(cur, jnp.maximum(cur - 1, 0), False, 0, n_heads)
        return carry

    lax.fori_loop(1, i + 1, body, 0)


def _softmax_stages(scores, values_t, keep, m_ref, l_ref, acc_ref, s_buf, mn_buf, al_buf, ps_buf, pv_buf):
    def stage_scores(kb, h, masked):
        s_buf[h] = scores(kb, h)

    def stage_probabilities(kb, h, masked):
        s = s_buf[h]
        if masked:
            s = jnp.where(keep, s, NEG_INF)
        m_prev = m_ref[h]
        m_new = jnp.maximum(m_prev, jnp.max(s, axis=0, keepdims=True))
        p = jnp.exp2(s - m_new)
        mn_buf[h] = m_new
        al_buf[h] = jnp.exp2(m_prev - m_new)
        ps_buf[h] = jnp.sum(p, axis=0, keepdims=True)
        pv_buf[h] = _dot(values_t(kb, h), p.astype(BF16))

    def stage_commit(kb, h, masked):
        alpha = al_buf[h]
        m_ref[h] = mn_buf[h]
        l_ref[h] = alpha * l_ref[h] + ps_buf[h]
        acc_ref[h] = alpha * acc_ref[h] + pv_buf[h]

    return stage_scores, stage_probabilities, stage_commit


def _softmax_scratch(n_heads, tk, nq):
    stat = pltpu.VMEM((n_heads, 1, nq), F32)
    wide = pltpu.VMEM((n_heads, HEAD_DIM, nq), F32)
    return [stat, stat, wide,
            pltpu.VMEM((n_heads, tk, nq), F32), stat, stat, stat, wide]


def _init_softmax_stats(m_ref, l_ref, acc_ref):
    m_ref[...] = jnp.full(m_ref.shape, NEG_INF, F32)
    l_ref[...] = jnp.zeros(l_ref.shape, F32)
    acc_ref[...] = jnp.zeros(acc_ref.shape, F32)


def _group_specs(s, tq, width, q_col, v_row):
    nq = s // tq
    q_spec = pl.BlockSpec((tq, width), lambda b, i: (b * nq + i, q_col))
    k_spec = pl.BlockSpec((s, width), lambda b, i: (b, q_col + 1))
    vt_spec = pl.BlockSpec((nq, width, tq), lambda b, i: (b, v_row, 0))
    o_spec = pl.BlockSpec((tq, width), lambda b, i: (b * nq + i, 0))
    return q_spec, k_spec, vt_spec, o_spec


def _diff_attn_kernel(lq_ref, lk_ref, g_ref, q_ref, k_ref, vt_ref, o_ref, qq_ref, m_ref, l_ref, acc_ref,
                      *bufs, tq, lam_init, lag):
    i = pl.program_id(1)
    low = lax.broadcasted_iota(jnp.int32, (HEAD_DIM, tq), 0) < DIFF_QK_DIM

    def prepare(h):
        q_t = _head(q_ref, h).astype(F32).T
        qq_ref[h, :, 0:tq] = jnp.where(low, q_t, 0.0).astype(BF16)
        qq_ref[h, :, tq:2 * tq] = jnp.where(low, 0.0, q_t).astype(BF16)

    _init_softmax_stats(m_ref, l_ref, acc_ref)

    key = lax.broadcasted_iota(jnp.int32, (tq, 2 * tq), 0)
    qry = lax.broadcasted_iota(jnp.int32, (tq, 2 * tq), 1)
    keep = jnp.where(qry >= tq, qry - tq, qry) >= key

    def scores(kb, h):
        return _dot(_key_block(k_ref, kb, tq, h), qq_ref[h])

    stages = _softmax_stages(scores, lambda kb, h: _value_block_t(vt_ref, kb, h), keep,
                             m_ref, l_ref, acc_ref, *bufs)
    _run_pipeline(i, N_DIFF, stages, lag, prepare)

    prod = lq_ref[...] * lk_ref[...]
    first = lax.broadcasted_iota(jnp.int32, prod.shape, 1) < DIFF_QK_DIM
    e1 = jnp.exp(jnp.sum(jnp.where(first, prod, 0.0), axis=1, keepdims=True))
    e2 = jnp.exp(jnp.sum(jnp.where(first, 0.0, prod), axis=1, keepdims=True))
    lam = (e1 - e2) + lam_init
    for h in range(N_DIFF):
        o_t = (acc_ref[h, :, 0:tq] / l_ref[h, :, 0:tq]
               - lam * (acc_ref[h, :, tq:2 * tq] / l_ref[h, :, tq:2 * tq]))
        o_ref[:, h * HEAD_DIM:(h + 1) * HEAD_DIM] = (_rms(o_t.T, g_ref[...]) * (1.0 - lam_init)).astype(BF16)


def diff_attention(qk, vt, lam_q, lam_k, g_sub, lam_init, *, batch, tq, lag):
    m = qk.shape[0]
    s = m // batch
    vec = pl.BlockSpec((1, LANES), lambda b, i: (0, 0))
    q_spec, k_spec, vt_spec, o_spec = _group_specs(s, tq, A_W, QA_COL, VA_ROW)
    return pl.pallas_call(
        functools.partial(_diff_attn_kernel, tq=tq, lam_init=lam_init, lag=lag),
        grid=(batch, s // tq),
        in_specs=[vec, vec, vec, q_spec, k_spec, vt_spec],
        out_specs=o_spec,
        out_shape=jax.ShapeDtypeStruct((m, A_W), BF16),
        scratch_shapes=[pltpu.VMEM((N_DIFF, HEAD_DIM, 2 * tq), BF16)] + _softmax_scratch(N_DIFF, tq, 2 * tq),
        compiler_params=_params("parallel", "arbitrary"),
        name="diff_attention",
    )(lam_q, lam_k, g_sub, qk, qk, vt)


def _sb_attn_kernel(q_ref, k_ref, vt_ref, o_ref, qn_ref, r_ref, acc_ref,
                    zn_buf, after_buf, cs_buf, pv_buf, *, tq, lag):
    i = pl.program_id(1)
    def prepare(h):
        qn_ref[h] = (-_head(q_ref, h).astype(F32)).T.astype(BF16)

    r_ref[...] = jnp.zeros(r_ref.shape, F32)
    acc_ref[...] = jnp.zeros(acc_ref.shape, F32)
    ss = lax.broadcasted_iota(jnp.int32, (tq, 2 * tq), 0)
    jj = lax.broadcasted_iota(jnp.int32, (tq, 2 * tq), 1)
    tri = (jnp.where(jj >= tq, jj - tq, jj) >= ss).astype(BF16)

    strict = lax.broadcasted_iota(jnp.int32, (tq, tq), 0) < lax.broadcasted_iota(jnp.int32, (tq, tq), 1)

    def scores(kb, h, masked):
        zn_buf[h] = _dot(_key_block(k_ref, kb, tq, h), qn_ref[h])

    def suffix_sums(kb, h, masked):
        zn = zn_buf[h]
        sp = jnp.log(1.0 + jnp.exp2(jnp.abs(zn) * -LOG2E))
        log_remain = jnp.minimum(zn, 0.0) - sp
        if masked:
            log_remain = jnp.where(strict, log_remain, 0.0)
        hi = log_remain.astype(BF16)
        lo = (log_remain - hi.astype(F32)).astype(BF16)
        after_buf[h] = _dot(tri, jnp.concatenate([hi, lo], axis=0))
        cs_buf[h] = jnp.sum(log_remain, axis=0, keepdims=True)

    def weights(kb, h, masked):
        w = jnp.exp((after_buf[h] + r_ref[h]) - zn_buf[h])
        if masked:
            w = jnp.where(strict, w, 0.0)
        pv_buf[h] = _dot(_value_block_t(vt_ref, kb, h), w.astype(BF16))
        r_ref[h] += cs_buf[h]

    def accumulate(kb, h, masked):
        acc_ref[h] += pv_buf[h]

    _run_pipeline(i, N_SB, (scores, suffix_sums, weights, accumulate), lag, prepare)
    for h in range(N_SB):
        o_ref[:, h * HEAD_DIM:(h + 1) * HEAD_DIM] = acc_ref[h].T.astype(BF16)


def stick_breaking_attention(qk, vt, *, batch, tq, lag):
    m = qk.shape[0]
    s = m // batch
    q_spec, k_spec, vt_spec, o_spec = _group_specs(s, tq, SB_W, QB_COL, VB_ROW)
    stat = pltpu.VMEM((N_SB, 1, tq), F32)
    wide = pltpu.VMEM((N_SB, HEAD_DIM, tq), F32)
    tile = pltpu.VMEM((N_SB, tq, tq), F32)
    return pl.pallas_call(
        functools.partial(_sb_attn_kernel, tq=tq, lag=lag),
        grid=(batch, s // tq),
        in_specs=[q_spec, k_spec, vt_spec],
        out_specs=o_spec,
        out_shape=jax.ShapeDtypeStruct((m, SB_W), BF16),
        scratch_shapes=[pltpu.VMEM((N_SB, HEAD_DIM, tq), BF16), stat, wide,
                        tile, tile, stat, wide],
        compiler_params=_params("parallel", "arbitrary"),
        name="stick_breaking_attention",
    )(qk, qk, vt)


def _fox_attn_kernel(fq_ref, fk_ref, q_ref, k_ref, vt_ref, o_ref, qa_ref, m_ref, l_ref, acc_ref, *bufs, tq, lag):
    i = pl.program_id(1)
    feat = lax.broadcasted_iota(jnp.int32, (LANES, tq), 0)
    fq_t = fq_ref[...].astype(F32).T

    def prepare(h):
        lo = GATE_PARTS * h
        mine = (((feat >= lo) & (feat < lo + GATE_PARTS))
                | ((feat >= GATE_LANES + lo) & (feat < GATE_LANES + lo + GATE_PARTS)))
        qa_ref[h, 0:HEAD_DIM, :] = _head(q_ref, h).astype(F32).T.astype(BF16)
        qa_ref[h, HEAD_DIM:2 * HEAD_DIM, :] = jnp.where(mine, fq_t, 0.0).astype(BF16)

    _init_softmax_stats(m_ref, l_ref, acc_ref)

    keep = lax.broadcasted_iota(jnp.int32, (tq, tq), 1) >= lax.broadcasted_iota(jnp.int32, (tq, tq), 0)

    def scores(kb, h):
        fk = fk_ref[pl.ds(pl.multiple_of(kb * tq, tq), tq), :]
        return _dot(jnp.concatenate([_key_block(k_ref, kb, tq, h), fk], axis=1), qa_ref[h])

    stages = _softmax_stages(scores, lambda kb, h: _value_block_t(vt_ref, kb, h), keep,
                             m_ref, l_ref, acc_ref, *bufs)
    _run_pipeline(i, N_FOX, stages, lag, prepare)
    for h in range(N_FOX):
        o_ref[:, h * HEAD_DIM:(h + 1) * HEAD_DIM] = (acc_ref[h] / l_ref[h]).T.astype(BF16)


def forgetting_attention(qk, vt, fk, fq, *, batch, tq, lag):
    m = qk.shape[0]
    s = m // batch
    nq = s // tq
    q_spec, k_spec, vt_spec, o_spec = _group_specs(s, tq, FOX_W, QC_COL, VC_ROW)
    return pl.pallas_call(
        functools.partial(_fox_attn_kernel, tq=tq, lag=lag),
        grid=(batch, nq),
        in_specs=[pl.BlockSpec((tq, LANES), lambda b, i: (b * nq + i, 0)),
                  pl.BlockSpec((s, LANES), lambda b, i: (b, 0)),
                  q_spec, k_spec, vt_spec],
        out_specs=o_spec,
        out_shape=jax.ShapeDtypeStruct((m, FOX_W), BF16),
        scratch_shapes=[pltpu.VMEM((N_FOX, 2 * HEAD_DIM, tq), BF16)] + _softmax_scratch(N_FOX, tq, tq),
        compiler_params=_params("parallel", "arbitrary"),
        name="forgetting_attention",
    )(fq, fk, qk, qk, vt)


def _residual_tail(y, x, g_post, g_next, x_out_ref, h_out_ref):
    x_new = x + _rms(y, g_post)
    x_out_ref[...] = x_new
    if h_out_ref is not None:
        h_out_ref[...] = _rms(x_new, g_next).astype(BF16)


def _mix_out_kernel(oa_ref, ob_ref, oc_ref, x_ref, w_ref, gb_ref, gc_ref, gp_ref, gn_ref, xo_ref, ho_ref):
    ob = _rms(ob_ref[...].astype(F32), gb_ref[...]).astype(BF16)
    oc = _rms(oc_ref[...].astype(F32), gc_ref[...]).astype(BF16)
    y = (_dot(oa_ref[...], w_ref[0:A_W, :]) + _dot(ob, w_ref[A_W:A_W + SB_W, :])
         + _dot(oc, w_ref[A_W + SB_W:, :]))
    _residual_tail(y, x_ref[...], gp_ref[...], gn_ref[...], xo_ref, ho_ref)


def mix_out(oa, ob, oc, x, w, g_sb, g_fox, g_post, g_next, *, bm):
    m, d = x.shape
    rows = lambda width: pl.BlockSpec((bm, width), lambda i: (i, 0))
    const = lambda width: pl.BlockSpec((1, width), lambda i: (0, 0))
    return pl.pallas_call(
        _mix_out_kernel,
        grid=(m // bm,),
        in_specs=[rows(A_W), rows(SB_W), rows(FOX_W), rows(d),
                  pl.BlockSpec(w.shape, lambda i: (0, 0)),
                  const(SB_W), const(FOX_W), const(d), const(d)],
        out_specs=[rows(d), rows(d)],
        out_shape=[jax.ShapeDtypeStruct((m, d), F32), jax.ShapeDtypeStruct((m, d), BF16)],
        compiler_params=_params("parallel"),
        name="mix_out",
    )(oa, ob, oc, x, w, g_sb.reshape(1, -1), g_fox.reshape(1, -1), g_post.reshape(1, -1), g_next.reshape(1, -1))


def _norm_matmul_kernel(x_ref, g_ref, w_ref, o_ref):
    o_ref[...] = _dot(_rms(x_ref[...], g_ref[...]).astype(BF16), w_ref[...]).astype(BF16)


def norm_matmul(x, g, w, *, bm):
    m, d = x.shape
    n = w.shape[1]
    return pl.pallas_call(
        _norm_matmul_kernel,
        grid=(m // bm,),
        in_specs=[pl.BlockSpec((bm, d), lambda i: (i, 0)), pl.BlockSpec((1, d), lambda i: (0, 0)),
                  pl.BlockSpec((d, n), lambda i: (0, 0))],
        out_specs=pl.BlockSpec((bm, n), lambda i: (i, 0)),
        out_shape=jax.ShapeDtypeStruct((m, n), BF16),
        compiler_params=_params("parallel"),
        name="memory_kv",
    )(x, g.reshape(1, d), w)


def _cross_attn_kernel(h_ref, x_ref, kv_ref, wq_ref, wo_ref, gp_ref, gn_ref, xo_ref, ho_ref):
    q = (_dot(h_ref[...], wq_ref[...]) * HEAD_DIM ** -0.5).astype(BF16)
    heads = []
    for hd in range(N_CROSS_HEADS):
        k = kv_ref[:, hd * HEAD_DIM:(hd + 1) * HEAD_DIM]
        v = kv_ref[:, CROSS_W + hd * HEAD_DIM:CROSS_W + (hd + 1) * HEAD_DIM]
        s = _dot_nt(q[:, hd * HEAD_DIM:(hd + 1) * HEAD_DIM], k)
        p = jnp.exp(s - jnp.max(s, axis=1, keepdims=True))
        o = _dot(p.astype(BF16), v) / jnp.sum(p, axis=1, keepdims=True)
        heads.append(o.astype(BF16))
    y = _dot(jnp.concatenate(heads, axis=1), wo_ref[...])
    _residual_tail(y, x_ref[...], gp_ref[...], gn_ref[...], xo_ref, ho_ref)


def cross_attention(h, x, kv, wq, wo, g_post, g_next, *, batch, bm):
    m, d = x.shape
    s = m // batch
    mem_len = kv.shape[0] // batch
    nb = s // bm
    rows = lambda width: pl.BlockSpec((bm, width), lambda b, i: (b * nb + i, 0))
    const = lambda shape: pl.BlockSpec(shape, lambda b, i: (0, 0))
    return pl.pallas_call(
        _cross_attn_kernel,
        grid=(batch, nb),
        in_specs=[rows(d), rows(d),
                  pl.BlockSpec((mem_len, 2 * CROSS_W), lambda b, i: (b, 0)),
                  const(wq.shape), const(wo.shape), const((1, d)), const((1, d))],
        out_specs=[rows(d), rows(d)],
        out_shape=[jax.ShapeDtypeStruct((m, d), F32), jax.ShapeDtypeStruct((m, d), BF16)],
        compiler_params=_params("parallel", "parallel"),
        name="cross_attention",
    )(h, x, kv, wq, wo, g_post.reshape(1, d), g_next.reshape(1, d))


def _ffn_up_kernel(h_ref, wg_ref, wu_ref, o_ref):
    h = h_ref[...]
    gate = _dot(h, wg_ref[...])
    up = _dot(h, wu_ref[...])
    o_ref[...] = (gate * jax.nn.sigmoid(gate) * up).astype(BF16)


def ffn_up(h, wg, wu, *, bm, bn):
    m, d = h.shape
    n = wg.shape[1]
    wspec = pl.BlockSpec((d, bn), lambda i, j: (0, j))
    return pl.pallas_call(
        _ffn_up_kernel,
        grid=(m // bm, n // bn),
        in_specs=[pl.BlockSpec((bm, d), lambda i, j: (i, 0)), wspec, wspec],
        out_specs=pl.BlockSpec((bm, bn), lambda i, j: (i, j)),
        out_shape=jax.ShapeDtypeStruct((m, n), BF16),
        compiler_params=_params("parallel", "arbitrary"),
        name="ffn_up",
    )(h, wg, wu)


def _ffn_down_kernel(*refs, emit_h):
    if emit_h:
        a_ref, w_ref, x_ref, gp_ref, gn_ref, xo_ref, ho_ref = refs
    else:
        a_ref, w_ref, x_ref, gp_ref, xo_ref = refs
        gn_ref = ho_ref = None
    kk = pl.program_id(1)
    last = pl.num_programs(1) - 1
    part = _dot(a_ref[...], w_ref[...])

    @pl.when(kk == 0)
    def _():
        xo_ref[...] = part

    @pl.when((kk > 0) & (kk < last))
    def _():
        xo_ref[...] += part

    @pl.when(kk == last)
    def _():
        g_next = gn_ref[...] if emit_h else None
        _residual_tail(xo_ref[...] + part, x_ref[...], gp_ref[...], g_next, xo_ref, ho_ref)


def ffn_down(a, w, x, g_post, g_next, *, bm, bk):
    m, d = x.shape
    kdim = a.shape[1]
    emit_h = g_next is not None
    assert kdim // bk >= 2
    rows = pl.BlockSpec((bm, d), lambda i, k: (i, 0))
    const = pl.BlockSpec((1, d), lambda i, k: (0, 0))
    gains = [g_post.reshape(1, d)] + ([g_next.reshape(1, d)] if emit_h else [])
    out_shape = [jax.ShapeDtypeStruct((m, d), F32)] + ([jax.ShapeDtypeStruct((m, d), BF16)] if emit_h else [])
    return pl.pallas_call(
        functools.partial(_ffn_down_kernel, emit_h=emit_h),
        grid=(m // bm, kdim // bk),
        in_specs=[pl.BlockSpec((bm, bk), lambda i, k: (i, k)),
                  pl.BlockSpec((bk, d), lambda i, k: (k, 0)),
                  rows] + [const] * len(gains),
        out_specs=[rows] * len(out_shape),
        out_shape=out_shape,
        compiler_params=_params("parallel", "arbitrary"),
        name="ffn_down",
    )(a, w, x, *gains)


def _pick(n, *candidates):
    for c in candidates:
        if n % c == 0:
            return c
    return n


def _columns(w, names):
    return jnp.concatenate([w[..., _IN_OFFSETS[n][0]:_IN_OFFSETS[n][1]] for n in names], axis=-1)


def kernel(x, mem, positions, g_mix_pre, g_mix_post, w_in, b_f, lam_q1, lam_k1, lam_q2, lam_k2, g_diff_sub, g_sb_out, g_fox_out, w_out, g_x_pre, g_x_post, g_mem, w_cq, w_ckv, w_co, g_ffn_pre, g_ffn_post, w_gate, w_up, w_down):
    batch, seq, d = x.shape
    depth = w_in.shape[0]
    m = batch * seq
    d_ff = w_gate.shape[2]
    assert w_in.shape[2] == _IN_OFFSETS["fc"][1] and d == V_ROWS

    bm_row = _pick(m, 512, 256, 128)
    bm_mm = _pick(m, 1024, 512, 256, 128)
    bm_big = _pick(m, 2048, 1024, 512, 256, 128)
    lag_diff, lag_sb, lag_fox = 2, 2, 3
    bn_in = 2 * A_W
    bn_ff = _pick(d_ff, 512, 256, 128)
    bk_ff = _pick(d_ff, 2816, 1408, 1024, 512, 256, 128)
    tq = _pick(seq, 256, 128)
    bm_x = _pick(seq, 512, 256, 128)
    bm_mem = _pick(mem.shape[0] * mem.shape[1], 512, 256, 128)

    w_qk = _columns(w_in, QK_ORDER).astype(BF16)
    w_qk_rope = _columns(w_in, QK_ROPE_ORDER).astype(BF16)
    w_vt = jnp.swapaxes(_columns(w_in, V_ORDER), 1, 2).astype(BF16)
    gate_lane_head = jnp.arange(2 * GATE_LANES) % GATE_LANES // GATE_PARTS
    wf_rep = jnp.pad(_columns(w_in, ("fc",))[:, :, gate_lane_head],
                     ((0, 0), (0, 0), (0, LANES - 2 * GATE_LANES))).astype(BF16)
    bf_rep = jnp.pad(b_f[:, gate_lane_head], ((0, 0), (0, LANES - 2 * GATE_LANES))).reshape(depth, 1, LANES)
    w_out_b, w_cq_b, w_ckv_b, w_co_b = (w.astype(BF16) for w in (w_out, w_cq, w_ckv, w_co))
    w_gate_b, w_up_b, w_down_b = (w.astype(BF16) for w in (w_gate, w_up, w_down))
    lam_q = jnp.concatenate([lam_q1, lam_q2], axis=1).reshape(depth, 1, LANES)
    lam_k = jnp.concatenate([lam_k1, lam_k2], axis=1).reshape(depth, 1, LANES)

    def scales(*groups):
        return jnp.concatenate([jnp.full((w,), v, F32) for w, v in groups]).reshape(1, -1)

    colscale = scales((SB_W, HEAD_DIM ** -0.5), (SB_W, 1.0), (FOX_W, HEAD_DIM ** -0.5 * LOG2E), (FOX_W, 1.0))
    colscale_rope = scales((A_W, DIFF_QK_DIM ** -0.5 * LOG2E), (A_W, 1.0))

    x = x.reshape(m, d)
    mem2 = mem.reshape(-1, d)
    cos_t, sin_t = rope_tables(positions, bm=bm_row)
    h = prenorm(x, g_mix_pre[0], bm=bm_row)

    for l in range(depth):
        lam_init = 0.8 - 0.6 * math.exp(-0.3 * l)
        qk = in_proj(h, w_qk[l], colscale, None, bm=bm_big, bn=bn_in)
        qk_rope = in_proj(h, w_qk_rope[l], colscale_rope, (cos_t, sin_t), bm=bm_big, bn=bn_in)
        vt = in_proj_vt(h, w_vt[l], bm=bm_big, bn=bn_in, tk=tq)
        fk, fq = forget_features(h, wf_rep[l], bf_rep[l], batch=batch, blk=tq)
        oa = diff_attention(qk_rope, vt, lam_q[l], lam_k[l], g_diff_sub[l].reshape(1, LANES), lam_init, batch=batch, tq=tq,
                            lag=lag_diff)
        ob = stick_breaking_attention(qk, vt, batch=batch, tq=tq, lag=lag_sb)
        oc = forgetting_attention(qk, vt, fk, fq, batch=batch, tq=tq, lag=lag_fox)
        x, h = mix_out(oa, ob, oc, x, w_out_b[l], g_sb_out[l], g_fox_out[l], g_mix_post[l], g_x_pre[l], bm=bm_row)
        kv = norm_matmul(mem2, g_mem[l], w_ckv_b[l], bm=bm_mem)
        x, h = cross_attention(h, x, kv, w_cq_b[l], w_co_b[l], g_x_post[l], g_ffn_pre[l], batch=batch, bm=bm_x)
        a = ffn_up(h, w_gate_b[l], w_up_b[l], bm=bm_mm, bn=bn_ff)
        g_next = g_mix_pre[l + 1] if l + 1 < depth else None
        x, *rest = ffn_down(a, w_down_b[l], x, g_ffn_post[l], g_next, bm=bm_row, bk=bk_ff)
        h = rest[0] if rest else None
    return x.reshape(batch, seq, d)
```

```python
import functools
import math

import jax
import jax.numpy as jnp
from jax import lax
from jax.experimental import pallas as pl
from jax.experimental.pallas import tpu as pltpu

F32 = jnp.float32
BF16 = jnp.bfloat16

EPS = 1e-6
NEG_INF = -1e30
ROPE_THETA = 10000.0
LOG2E = math.log2(math.e)

LANES = 128
HEAD_DIM = 128
N_DIFF, N_SB, N_FOX = 4, 6, 6
DIFF_QK_DIM = HEAD_DIM // 2
A_W, SB_W, FOX_W = N_DIFF * HEAD_DIM, N_SB * HEAD_DIM, N_FOX * HEAD_DIM
V_ROWS = A_W + SB_W + FOX_W
N_CROSS_HEADS = 4
CROSS_W = N_CROSS_HEADS * HEAD_DIM
GATE_PARTS = 3
GATE_LANES = GATE_PARTS * N_FOX

VMEM_LIMIT = 56 * 1024 * 1024

_IN_OFFSETS = {}
_off = 0
for _name, _width in (("qa", A_W), ("ka", A_W), ("va", A_W), ("qb", SB_W), ("kb", SB_W), ("vb", SB_W),
                      ("qc", FOX_W), ("kc", FOX_W), ("vc", FOX_W), ("fc", N_FOX)):
    _IN_OFFSETS[_name] = (_off, _off + _width)
    _off += _width
QK_ORDER = ("qb", "kb", "qc", "kc")
V_ORDER = ("vb", "vc", "va")
QB_COL, QC_COL, QA_COL = 0, 2, 0
VB_ROW, VC_ROW, VA_ROW = 0, 1, (2 * SB_W) // A_W


def _params(*semantics):
    return pltpu.CompilerParams(dimension_semantics=semantics, vmem_limit_bytes=VMEM_LIMIT)


def _rms(xf, g):
    return xf * lax.rsqrt(jnp.mean(xf * xf, axis=-1, keepdims=True) + EPS) * g


def _dot(a, b):
    return jnp.dot(a, b, preferred_element_type=F32)


def _dot_nt(a, b):
    return lax.dot_general(a, b, (((1,), (1,)), ((), ())), preferred_element_type=F32)


def _bf16_parts(x):
    p0 = x.astype(BF16)
    r1 = x - p0.astype(F32)
    p1 = r1.astype(BF16)
    p2 = (r1 - p1.astype(F32)).astype(BF16)
    return p0, p1, p2


def _prenorm_kernel(x_ref, g_ref, h_ref):
    h_ref[...] = _rms(x_ref[...], g_ref[...]).astype(BF16)


def prenorm(x, g, *, bm):
    m, d = x.shape
    return pl.pallas_call(
        _prenorm_kernel,
        grid=(m // bm,),
        in_specs=[pl.BlockSpec((bm, d), lambda i: (i, 0)), pl.BlockSpec((1, d), lambda i: (0, 0))],
        out_specs=pl.BlockSpec((bm, d), lambda i: (i, 0)),
        out_shape=jax.ShapeDtypeStruct((m, d), BF16),
        compiler_params=_params("parallel"),
        name="prenorm",
    )(x, g.reshape(1, d))


def _rope_table_kernel(pos_ref, invf_ref, sign_ref, cos_ref, sin_ref):
    ang = pos_ref[...].astype(F32) * invf_ref[...]
    cos_ref[...] = jnp.cos(ang)
    sin_ref[...] = jnp.sin(ang) * sign_ref[...]


def rope_tables(positions, *, bm):
    m = positions.size
    half = DIFF_QK_DIM // 2
    inv_freq = ROPE_THETA ** (-jnp.arange(half, dtype=F32) / half)
    invf = jnp.tile(inv_freq, LANES // half).reshape(1, LANES)
    sign = jnp.where((jnp.arange(LANES) % DIFF_QK_DIM) < half, -1.0, 1.0).astype(F32).reshape(1, LANES)
    row = pl.BlockSpec((bm, LANES), lambda i: (i, 0))
    const = pl.BlockSpec((1, LANES), lambda i: (0, 0))
    return pl.pallas_call(
        _rope_table_kernel,
        grid=(m // bm,),
        in_specs=[pl.BlockSpec((bm, 1), lambda i: (i, 0)), const, const],
        out_specs=[row, row],
        out_shape=[jax.ShapeDtypeStruct((m, LANES), F32)] * 2,
        compiler_params=_params("parallel"),
        name="rope_tables",
    )(positions.reshape(m, 1), invf, sign)


def _in_proj_kernel(h_ref, w_ref, cs_ref, o_ref):
    o_ref[...] = (_dot(h_ref[...], w_ref[...].astype(BF16)) * cs_ref[...]).astype(BF16)


def _in_proj_rope_kernel(h_ref, w_ref, cs_ref, cos_ref, sin_ref, o_ref):
    acc = _dot(h_ref[...], w_ref[...].astype(BF16)) * cs_ref[...]
    bm, bn = acc.shape
    c, s = cos_ref[...], sin_ref[...]
    first_half = (lax.broadcasted_iota(jnp.int32, (bm, LANES), 1) % DIFF_QK_DIM) < DIFF_QK_DIM // 2
    for t in range(bn // LANES):
        a = acc[:, t * LANES:(t + 1) * LANES]
        partner = jnp.where(first_half, pltpu.roll(a, LANES - DIFF_QK_DIM // 2, 1),
                            pltpu.roll(a, DIFF_QK_DIM // 2, 1))
        o_ref[:, t * LANES:(t + 1) * LANES] = (a * c + partner * s).astype(BF16)


def in_proj(h, w_all, layer, col_blocks, colscale, rope_tables_or_none, *, bm, bn):
    m, d = h.shape
    n = bn * len(col_blocks)
    assert m % bm == 0 and colscale.shape == (1, n)
    rope = rope_tables_or_none is not None

    def w_block(i, j):
        block = col_blocks[0]
        for t in range(1, len(col_blocks)):
            block = block + (col_blocks[t] - col_blocks[t - 1]) * (j >= t).astype(jnp.int32)
        return (layer, 0, block)

    row_tab = pl.BlockSpec((bm, LANES), lambda i, j: (i, 0))
    return pl.pallas_call(
        _in_proj_rope_kernel if rope else _in_proj_kernel,
        grid=(m // bm, len(col_blocks)),
        in_specs=[pl.BlockSpec((bm, d), lambda i, j: (i, 0)),
                  pl.BlockSpec((None, d, bn), w_block),
                  pl.BlockSpec((1, bn), lambda i, j: (0, j))] + ([row_tab, row_tab] if rope else []),
        out_specs=pl.BlockSpec((bm, bn), lambda i, j: (i, j)),
        out_shape=jax.ShapeDtypeStruct((m, n), BF16),
        compiler_params=_params("parallel", "arbitrary"),
        name="in_proj_rope" if rope else "in_proj",
    )(h, w_all, colscale, *(rope_tables_or_none or ()))


def _in_proj_vt_kernel(h_ref, wt_ref, o_ref, *, tk):
    res = _dot_nt(wt_ref[...], h_ref[...]).astype(BF16)
    for c in range(o_ref.shape[0]):
        o_ref[c] = res[:, c * tk:(c + 1) * tk]


def in_proj_vt(h, wt, *, bm, bn, tk):
    m, d = h.shape
    n = wt.shape[0]
    return pl.pallas_call(
        functools.partial(_in_proj_vt_kernel, tk=tk),
        grid=(m // bm, n // bn),
        in_specs=[pl.BlockSpec((bm, d), lambda i, j: (i, 0)),
                  pl.BlockSpec((bn, d), lambda i, j: (j, 0))],
        out_specs=pl.BlockSpec((bm // tk, bn, tk), lambda i, j: (i, j, 0)),
        out_shape=jax.ShapeDtypeStruct((m // tk, n, tk), BF16),
        compiler_params=_params("parallel", "arbitrary"),
        name="in_proj_vt",
    )(h, wt)


def _forget_features_kernel(h_ref, wf_ref, bf_ref, part_ref, fk_ref, fq_ref, *, blk):
    s = h_ref.shape[0]
    lane = lax.broadcasted_iota(jnp.int32, (blk, LANES), 1)
    part = jnp.broadcast_to(part_ref[...], (blk, LANES))
    lower = (lax.broadcasted_iota(jnp.int32, (blk, blk), 0)
             >= lax.broadcasted_iota(jnp.int32, (blk, blk), 1)).astype(BF16)
    one = jnp.ones((blk, LANES), F32)
    zero = jnp.zeros((blk, LANES), F32)
    carry = jnp.zeros((1, LANES), F32)
    for t in range(s // blk):
        rows = slice(t * blk, (t + 1) * blk)
        fc = _dot(h_ref[rows, :], wf_ref[...]) + bf_ref[...]
        log_f = (jnp.minimum(fc, 0.0) - jnp.log1p(jnp.exp(-jnp.abs(fc)))) * LOG2E
        x0, x1, x2 = _bf16_parts(log_f)
        cum = _dot(lower, x0) + _dot(lower, x1) + _dot(lower, x2) + carry
        carry = cum[blk - 1:blk, :]
        c0, c1, c2 = (c.astype(F32) for c in _bf16_parts(cum))
        parts = jnp.where(part == 0, c0, jnp.where(part == 1, c1, c2))
        fk = jnp.where(lane < GATE_LANES, parts, jnp.where(lane < 2 * GATE_LANES, one, zero))
        fq = jnp.where(lane < GATE_LANES, -one, jnp.where(lane < 2 * GATE_LANES, parts, zero))
        fk_ref[rows, :] = fk.astype(BF16)
        fq_ref[rows, :] = fq.astype(BF16)


def forget_features(h, wf_rep, bf_rep, *, batch, blk):
    m, d = h.shape
    s = m // batch
    part = (jnp.arange(LANES, dtype=jnp.int32) % GATE_PARTS).reshape(1, LANES)
    rows = pl.BlockSpec((s, LANES), lambda b: (b, 0))
    vec = pl.BlockSpec((1, LANES), lambda b: (0, 0))
    return pl.pallas_call(
        functools.partial(_forget_features_kernel, blk=blk),
        grid=(batch,),
        in_specs=[pl.BlockSpec((s, d), lambda b: (b, 0)),
                  pl.BlockSpec((d, LANES), lambda b: (0, 0)), vec, vec],
        out_specs=[rows, rows],
        out_shape=[jax.ShapeDtypeStruct((m, LANES), BF16)] * 2,
        compiler_params=_params("parallel"),
        name="forget_features",
    )(h, wf_rep, bf_rep, part)


def _head(ref, h):
    return ref[:, h * HEAD_DIM:(h + 1) * HEAD_DIM]


def _key_block(ref, kb, tk, h):
    return ref[pl.ds(pl.multiple_of(kb * tk, tk), tk), h * HEAD_DIM:(h + 1) * HEAD_DIM]


def _value_block_t(ref, kb, h):
    return ref[kb, h * HEAD_DIM:(h + 1) * HEAD_DIM, :]


def _run_pipeline(i, n_heads, stages, lag, prepare):
    n = len(stages)
    lead = [(n - 1 - k) * lag for k in range(n)]
    assert lead[0] <= n_heads
    unprepared = set(range(n_heads))

    def positions(cur, nxt, cur_masked, first, last):
        for t in range(first, last):
            for k in range(n):
                idx = t + lead[k]
                if 0 <= idx < n_heads:
                    if k == 0 and idx in unprepared:
                        unprepared.discard(idx)
                        prepare(idx)
                    stages[k](cur, idx, cur_masked)
                elif idx >= n_heads and nxt is not None:
                    stages[k](nxt, idx - n_heads, False)

    positions(i, None, True, -lead[0], 0)
    positions(i, jnp.maximum(i - 1, 0), True, 0, n_heads)

    def body(j, carry):
        cur = i - j
        positions(cur, jnp.maximum(cur - 1, 0), False, 0, n_heads)
        return carry

    lax.fori_loop(1, i + 1, body, 0)


def _softmax_stages(scores, values_t, keep, m_ref, l_ref, acc_ref, s_buf, mn_buf, al_buf, ps_buf, pv_buf):
    def stage_scores(kb, h, masked):
        s_buf[h] = scores(kb, h)

    def stage_probabilities(kb, h, masked):
        s = s_buf[h]
        if masked:
            s = jnp.where(keep, s, NEG_INF)
        m_prev = m_ref[h]
        m_new = jnp.maximum(m_prev, jnp.max(s, axis=0, keepdims=True))
        p = jnp.exp2(s - m_new)
        mn_buf[h] = m_new
        al_buf[h] = jnp.exp2(m_prev - m_new)
        ps_buf[h] = jnp.sum(p, axis=0, keepdims=True)
        pv_buf[h] = _dot(values_t(kb, h), p.astype(BF16))

    def stage_commit(kb, h, masked):
        alpha = al_buf[h]
        m_ref[h] = mn_buf[h]
        l_ref[h] = alpha * l_ref[h] + ps_buf[h]
        acc_ref[h] = alpha * acc_ref[h] + pv_buf[h]

    return stage_scores, stage_probabilities, stage_commit


def _softmax_scratch(n_heads, tk, nq):
    stat = pltpu.VMEM((n_heads, 1, nq), F32)
    wide = pltpu.VMEM((n_heads, HEAD_DIM, nq), F32)
    return [stat, stat, wide,
            pltpu.VMEM((n_heads, tk, nq), F32), stat, stat, stat, wide]


def _init_softmax_stats(m_ref, l_ref, acc_ref):
    m_ref[...] = jnp.full(m_ref.shape, NEG_INF, F32)
    l_ref[...] = jnp.zeros(l_ref.shape, F32)
    acc_ref[...] = jnp.zeros(acc_ref.shape, F32)


def _group_specs(s, tq, width, q_col, v_row):
    nq = s // tq
    q_spec = pl.BlockSpec((tq, width), lambda b, i: (b * nq + i, q_col))
    k_spec = pl.BlockSpec((s, width), lambda b, i: (b, q_col + 1))
    vt_spec = pl.BlockSpec((nq, width, tq), lambda b, i: (b, v_row, 0))
    o_spec = pl.BlockSpec((tq, width), lambda b, i: (b * nq + i, 0))
    return q_spec, k_spec, vt_spec, o_spec


def _diff_attn_kernel(lq_ref, lk_ref, g_ref, q_ref, k_ref, vt_ref, o_ref, qq_ref, m_ref, l_ref, acc_ref,
                      *bufs, tq, lam_init, lag):
    i = pl.program_id(1)
    low = lax.broadcasted_iota(jnp.int32, (HEAD_DIM, tq), 0) < DIFF_QK_DIM

    def prepare(h):
        q_t = _head(q_ref, h).astype(F32).T
        qq_ref[h, :, 0:tq] = jnp.where(low, q_t, 0.0).astype(BF16)
        qq_ref[h, :, tq:2 * tq] = jnp.where(low, 0.0, q_t).astype(BF16)

    _init_softmax_stats(m_ref, l_ref, acc_ref)

    key = lax.broadcasted_iota(jnp.int32, (tq, 2 * tq), 0)
    qry = lax.broadcasted_iota(jnp.int32, (tq, 2 * tq), 1)
    keep = jnp.where(qry >= tq, qry - tq, qry) >= key

    def scores(kb, h):
        return _dot(_key_block(k_ref, kb, tq, h), qq_ref[h])

    stages = _softmax_stages(scores, lambda kb, h: _value_block_t(vt_ref, kb, h), keep,
                             m_ref, l_ref, acc_ref, *bufs)
    _run_pipeline(i, N_DIFF, stages, lag, prepare)

    prod = lq_ref[...] * lk_ref[...]
    first = lax.broadcasted_iota(jnp.int32, prod.shape, 1) < DIFF_QK_DIM
    e1 = jnp.exp(jnp.sum(jnp.where(first, prod, 0.0), axis=1, keepdims=True))
    e2 = jnp.exp(jnp.sum(jnp.where(first, 0.0, prod), axis=1, keepdims=True))
    lam = (e1 - e2) + lam_init
    for h in range(N_DIFF):
        o_t = (acc_ref[h, :, 0:tq] / l_ref[h, :, 0:tq]
               - lam * (acc_ref[h, :, tq:2 * tq] / l_ref[h, :, tq:2 * tq]))
        o_ref[:, h * HEAD_DIM:(h + 1) * HEAD_DIM] = (_rms(o_t.T, g_ref[...]) * (1.0 - lam_init)).astype(BF16)


def diff_attention(qk, vt, lam_q, lam_k, g_sub, lam_init, *, batch, tq, lag):
    m = qk.shape[0]
    s = m // batch
    vec = pl.BlockSpec((1, LANES), lambda b, i: (0, 0))
    q_spec, k_spec, vt_spec, o_spec = _group_specs(s, tq, A_W, QA_COL, VA_ROW)
    return pl.pallas_call(
        functools.partial(_diff_attn_kernel, tq=tq, lam_init=lam_init, lag=lag),
        grid=(batch, s // tq),
        in_specs=[vec, vec, vec, q_spec, k_spec, vt_spec],
        out_specs=o_spec,
        out_shape=jax.ShapeDtypeStruct((m, A_W), BF16),
        scratch_shapes=[pltpu.VMEM((N_DIFF, HEAD_DIM, 2 * tq), BF16)] + _softmax_scratch(N_DIFF, tq, 2 * tq),
        compiler_params=_params("parallel", "arbitrary"),
        name="diff_attention",
    )(lam_q, lam_k, g_sub, qk, qk, vt)


def _sb_attn_kernel(q_ref, k_ref, vt_ref, o_ref, qn_ref, r_ref, acc_ref,
                    zn_buf, after_buf, cs_buf, pv_buf, *, tq, lag):
    i = pl.program_id(1)
    def prepare(h):
        qn_ref[h] = (-_head(q_ref, h).astype(F32)).T.astype(BF16)

    r_ref[...] = jnp.zeros(r_ref.shape, F32)
    acc_ref[...] = jnp.zeros(acc_ref.shape, F32)
    ss = lax.broadcasted_iota(jnp.int32, (tq, 2 * tq), 0)
    jj = lax.broadcasted_iota(jnp.int32, (tq, 2 * tq), 1)
    tri = (jnp.where(jj >= tq, jj - tq, jj) >= ss).astype(BF16)

    strict = lax.broadcasted_iota(jnp.int32, (tq, tq), 0) < lax.broadcasted_iota(jnp.int32, (tq, tq), 1)

    def scores(kb, h, masked):
        zn_buf[h] = _dot(_key_block(k_ref, kb, tq, h), qn_ref[h])

    def suffix_sums(kb, h, masked):
        zn = zn_buf[h]
        sp = jnp.log(1.0 + jnp.exp2(jnp.abs(zn) * -LOG2E))
        log_remain = jnp.minimum(zn, 0.0) - sp
        if masked:
            log_remain = jnp.where(strict, log_remain, 0.0)
        hi = log_remain.astype(BF16)
        lo = (log_remain - hi.astype(F32)).astype(BF16)
        after_buf[h] = _dot(tri, jnp.concatenate([hi, lo], axis=0))
        cs_buf[h] = jnp.sum(log_remain, axis=0, keepdims=True)

    def weights(kb, h, masked):
        w = jnp.exp((after_buf[h] + r_ref[h]) - zn_buf[h])
        if masked:
            w = jnp.where(strict, w, 0.0)
        pv_buf[h] = _dot(_value_block_t(vt_ref, kb, h), w.astype(BF16))
        r_ref[h] += cs_buf[h]

    def accumulate(kb, h, masked):
        acc_ref[h] += pv_buf[h]

    _run_pipeline(i, N_SB, (scores, suffix_sums, weights, accumulate), lag, prepare)
    for h in range(N_SB):
        o_ref[:, h * HEAD_DIM:(h + 1) * HEAD_DIM] = acc_ref[h].T.astype(BF16)


def stick_breaking_attention(qk, vt, *, batch, tq, lag):
    m = qk.shape[0]
    s = m // batch
    q_spec, k_spec, vt_spec, o_spec = _group_specs(s, tq, SB_W, QB_COL, VB_ROW)
    stat = pltpu.VMEM((N_SB, 1, tq), F32)
    wide = pltpu.VMEM((N_SB, HEAD_DIM, tq), F32)
    tile = pltpu.VMEM((N_SB, tq, tq), F32)
    return pl.pallas_call(
        functools.partial(_sb_attn_kernel, tq=tq, lag=lag),
        grid=(batch, s // tq),
        in_specs=[q_spec, k_spec, vt_spec],
        out_specs=o_spec,
        out_shape=jax.ShapeDtypeStruct((m, SB_W), BF16),
        scratch_shapes=[pltpu.VMEM((N_SB, HEAD_DIM, tq), BF16), stat, wide,
                        tile, tile, stat, wide],
        compiler_params=_params("parallel", "arbitrary"),
        name="stick_breaking_attention",
    )(qk, qk, vt)


def _fox_attn_kernel(fq_ref, fk_ref, q_ref, k_ref, vt_ref, o_ref, qa_ref, m_ref, l_ref, acc_ref, *bufs, tq, lag):
    i = pl.program_id(1)
    feat = lax.broadcasted_iota(jnp.int32, (LANES, tq), 0)
    fq_t = fq_ref[...].astype(F32).T

    def prepare(h):
        lo = GATE_PARTS * h
        mine = (((feat >= lo) & (feat < lo + GATE_PARTS))
                | ((feat >= GATE_LANES + lo) & (feat < GATE_LANES + lo + GATE_PARTS)))
        qa_ref[h, 0:HEAD_DIM, :] = _head(q_ref, h).astype(F32).T.astype(BF16)
        qa_ref[h, HEAD_DIM:2 * HEAD_DIM, :] = jnp.where(mine, fq_t, 0.0).astype(BF16)

    _init_softmax_stats(m_ref, l_ref, acc_ref)

    keep = lax.broadcasted_iota(jnp.int32, (tq, tq), 1) >= lax.broadcasted_iota(jnp.int32, (tq, tq), 0)

    def scores(kb, h):
        fk = fk_ref[pl.ds(pl.multiple_of(kb * tq, tq), tq), :]
        return _dot(jnp.concatenate([_key_block(k_ref, kb, tq, h), fk], axis=1), qa_ref[h])

    stages = _softmax_stages(scores, lambda kb, h: _value_block_t(vt_ref, kb, h), keep,
                             m_ref, l_ref, acc_ref, *bufs)
    _run_pipeline(i, N_FOX, stages, lag, prepare)
    for h in range(N_FOX):
        o_ref[:, h * HEAD_DIM:(h + 1) * HEAD_DIM] = (acc_ref[h] / l_ref[h]).T.astype(BF16)


def forgetting_attention(qk, vt, fk, fq, *, batch, tq, lag):
    m = qk.shape[0]
    s = m // batch
    nq = s // tq
    q_spec, k_spec, vt_spec, o_spec = _group_specs(s, tq, FOX_W, QC_COL, VC_ROW)
    return pl.pallas_call(
        functools.partial(_fox_attn_kernel, tq=tq, lag=lag),
        grid=(batch, nq),
        in_specs=[pl.BlockSpec((tq, LANES), lambda b, i: (b * nq + i, 0)),
                  pl.BlockSpec((s, LANES), lambda b, i: (b, 0)),
                  q_spec, k_spec, vt_spec],
        out_specs=o_spec,
        out_shape=jax.ShapeDtypeStruct((m, FOX_W), BF16),
        scratch_shapes=[pltpu.VMEM((N_FOX, 2 * HEAD_DIM, tq), BF16)] + _softmax_scratch(N_FOX, tq, tq),
        compiler_params=_params("parallel", "arbitrary"),
        name="forgetting_attention",
    )(fq, fk, qk, qk, vt)


def _residual_tail(y, x, g_post, g_next, x_out_ref, h_out_ref):
    x_new = x + _rms(y, g_post)
    x_out_ref[...] = x_new
    if h_out_ref is not None:
        h_out_ref[...] = _rms(x_new, g_next).astype(BF16)


def _mix_out_kernel(oa_ref, ob_ref, oc_ref, x_ref, w_ref, gb_ref, gc_ref, gp_ref, gn_ref, xo_ref, ho_ref):
    ob = _rms(ob_ref[...].astype(F32), gb_ref[...]).astype(BF16)
    oc = _rms(oc_ref[...].astype(F32), gc_ref[...]).astype(BF16)
    y = (_dot(oa_ref[...], w_ref[0:A_W, :]) + _dot(ob, w_ref[A_W:A_W + SB_W, :])
         + _dot(oc, w_ref[A_W + SB_W:, :]))
    _residual_tail(y, x_ref[...], gp_ref[...], gn_ref[...], xo_ref, ho_ref)


def mix_out(oa, ob, oc, x, w, g_sb, g_fox, g_post, g_next, *, bm):
    m, d = x.shape
    rows = lambda width: pl.BlockSpec((bm, width), lambda i: (i, 0))
    const = lambda width: pl.BlockSpec((1, width), lambda i: (0, 0))
    return pl.pallas_call(
        _mix_out_kernel,
        grid=(m // bm,),
        in_specs=[rows(A_W), rows(SB_W), rows(FOX_W), rows(d),
                  pl.BlockSpec(w.shape, lambda i: (0, 0)),
                  const(SB_W), const(FOX_W), const(d), const(d)],
        out_specs=[rows(d), rows(d)],
        out_shape=[jax.ShapeDtypeStruct((m, d), F32), jax.ShapeDtypeStruct((m, d), BF16)],
        compiler_params=_params("parallel"),
        name="mix_out",
    )(oa, ob, oc, x, w, g_sb.reshape(1, -1), g_fox.reshape(1, -1), g_post.reshape(1, -1), g_next.reshape(1, -1))


def _norm_matmul_kernel(x_ref, g_ref, w_ref, o_ref):
    o_ref[...] = _dot(_rms(x_ref[...], g_ref[...]).astype(BF16), w_ref[...]).astype(BF16)


def norm_matmul(x, g, w, *, bm):
    m, d = x.shape
    n = w.shape[1]
    return pl.pallas_call(
        _norm_matmul_kernel,
        grid=(m // bm,),
        in_specs=[pl.BlockSpec((bm, d), lambda i: (i, 0)), pl.BlockSpec((1, d), lambda i: (0, 0)),
                  pl.BlockSpec((d, n), lambda i: (0, 0))],
        out_specs=pl.BlockSpec((bm, n), lambda i: (i, 0)),
        out_shape=jax.ShapeDtypeStruct((m, n), BF16),
        compiler_params=_params("parallel"),
        name="memory_kv",
    )(x, g.reshape(1, d), w)


def _cross_attn_kernel(h_ref, x_ref, kv_ref, wq_ref, wo_ref, gp_ref, gn_ref, xo_ref, ho_ref):
    q = (_dot(h_ref[...], wq_ref[...]) * HEAD_DIM ** -0.5).astype(BF16)
    heads = []
    for hd in range(N_CROSS_HEADS):
        k = kv_ref[:, hd * HEAD_DIM:(hd + 1) * HEAD_DIM]
        v = kv_ref[:, CROSS_W + hd * HEAD_DIM:CROSS_W + (hd + 1) * HEAD_DIM]
        s = _dot_nt(q[:, hd * HEAD_DIM:(hd + 1) * HEAD_DIM], k)
        p = jnp.exp(s - jnp.max(s, axis=1, keepdims=True))
        o = _dot(p.astype(BF16), v) / jnp.sum(p, axis=1, keepdims=True)
        heads.append(o.astype(BF16))
    y = _dot(jnp.concatenate(heads, axis=1), wo_ref[...])
    _residual_tail(y, x_ref[...], gp_ref[...], gn_ref[...], xo_ref, ho_ref)


def cross_attention(h, x, kv, wq, wo, g_post, g_next, *, batch, bm):
    m, d = x.shape
    s = m // batch
    mem_len = kv.shape[0] // batch
    nb = s // bm
    rows = lambda width: pl.BlockSpec((bm, width), lambda b, i: (b * nb + i, 0))
    const = lambda shape: pl.BlockSpec(shape, lambda b, i: (0, 0))
    return pl.pallas_call(
        _cross_attn_kernel,
        grid=(batch, nb),
        in_specs=[rows(d), rows(d),
                  pl.BlockSpec((mem_len, 2 * CROSS_W), lambda b, i: (b, 0)),
                  const(wq.shape), const(wo.shape), const((1, d)), const((1, d))],
        out_specs=[rows(d), rows(d)],
        out_shape=[jax.ShapeDtypeStruct((m, d), F32), jax.ShapeDtypeStruct((m, d), BF16)],
        compiler_params=_params("parallel", "parallel"),
        name="cross_attention",
    )(h, x, kv, wq, wo, g_post.reshape(1, d), g_next.reshape(1, d))


def _ffn_up_kernel(h_ref, wg_ref, wu_ref, o_ref):
    h = h_ref[...]
    gate = _dot(h, wg_ref[...])
    up = _dot(h, wu_ref[...])
    o_ref[...] = (gate * jax.nn.sigmoid(gate) * up).astype(BF16)


def ffn_up(h, wg, wu, *, bm, bn):
    m, d = h.shape
    n = wg.shape[1]
    wspec = pl.BlockSpec((d, bn), lambda i, j: (0, j))
    return pl.pallas_call(
        _ffn_up_kernel,
        grid=(m // bm, n // bn),
        in_specs=[pl.BlockSpec((bm, d), lambda i, j: (i, 0)), wspec, wspec],
        out_specs=pl.BlockSpec((bm, bn), lambda i, j: (i, j)),
        out_shape=jax.ShapeDtypeStruct((m, n), BF16),
        compiler_params=_params("parallel", "arbitrary"),
        name="ffn_up",
    )(h, wg, wu)


def _ffn_down_kernel(*refs, emit_h):
    if emit_h:
        a_ref, w_ref, x_ref, gp_ref, gn_ref, xo_ref, ho_ref = refs
    else:
        a_ref, w_ref, x_ref, gp_ref, xo_ref = refs
        gn_ref = ho_ref = None
    kk = pl.program_id(1)
    last = pl.num_programs(1) - 1
    part = _dot(a_ref[...], w_ref[...])

    @pl.when(kk == 0)
    def _():
        xo_ref[...] = part

    @pl.when((kk > 0) & (kk < last))
    def _():
        xo_ref[...] += part

    @pl.when(kk == last)
    def _():
        g_next = gn_ref[...] if emit_h else None
        _residual_tail(xo_ref[...] + part, x_ref[...], gp_ref[...], g_next, xo_ref, ho_ref)


def ffn_down(a, w, x, g_post, g_next, *, bm, bk):
    m, d = x.shape
    kdim = a.shape[1]
    emit_h = g_next is not None
    assert kdim // bk >= 2
    rows = pl.BlockSpec((bm, d), lambda i, k: (i, 0))
    const = pl.BlockSpec((1, d), lambda i, k: (0, 0))
    gains = [g_post.reshape(1, d)] + ([g_next.reshape(1, d)] if emit_h else [])
    out_shape = [jax.ShapeDtypeStruct((m, d), F32)] + ([jax.ShapeDtypeStruct((m, d), BF16)] if emit_h else [])
    return pl.pallas_call(
        functools.partial(_ffn_down_kernel, emit_h=emit_h),
        grid=(m // bm, kdim // bk),
        in_specs=[pl.BlockSpec((bm, bk), lambda i, k: (i, k)),
                  pl.BlockSpec((bk, d), lambda i, k: (k, 0)),
                  rows] + [const] * len(gains),
        out_specs=[rows] * len(out_shape),
        out_shape=out_shape,
        compiler_params=_params("parallel", "arbitrary"),
        name="ffn_down",
    )(a, w, x, *gains)


def _pick(n, *candidates):
    for c in candidates:
        if n % c == 0:
            return c
    return n


def _columns(w, names):
    return jnp.concatenate([w[..., _IN_OFFSETS[n][0]:_IN_OFFSETS[n][1]] for n in names], axis=-1)


def kernel(x, mem, positions, g_mix_pre, g_mix_post, w_in, b_f, lam_q1, lam_k1, lam_q2, lam_k2, g_diff_sub, g_sb_out, g_fox_out, w_out, g_x_pre, g_x_post, g_mem, w_cq, w_ckv, w_co, g_ffn_pre, g_ffn_post, w_gate, w_up, w_down):
    batch, seq, d = x.shape
    depth = w_in.shape[0]
    m = batch * seq
    d_ff = w_gate.shape[2]
    assert w_in.shape[2] == _IN_OFFSETS["fc"][1] and d == V_ROWS

    bm_row = _pick(m, 512, 256, 128)
    bm_mm = _pick(m, 1024, 512, 256, 128)
    bm_big = _pick(m, 2048, 1024, 512, 256, 128)
    lag_diff, lag_sb, lag_fox = 2, 2, 3
    bn_in = 2 * A_W
    bn_ff = _pick(d_ff, 512, 256, 128)
    bk_ff = _pick(d_ff, 2816, 1408, 1024, 512, 256, 128)
    tq = _pick(seq, 256, 128)
    bm_x = _pick(seq, 512, 256, 128)
    bm_mem = _pick(mem.shape[0] * mem.shape[1], 512, 256, 128)

    w_vt = jnp.swapaxes(_columns(w_in, V_ORDER), 1, 2).astype(BF16)
    gate_lane_head = jnp.arange(2 * GATE_LANES) % GATE_LANES // GATE_PARTS
    wf_rep = jnp.pad(_columns(w_in, ("fc",))[:, :, gate_lane_head],
                     ((0, 0), (0, 0), (0, LANES - 2 * GATE_LANES))).astype(BF16)
    bf_rep = jnp.pad(b_f[:, gate_lane_head], ((0, 0), (0, LANES - 2 * GATE_LANES))).reshape(depth, 1, LANES)
    w_out_b, w_cq_b, w_ckv_b, w_co_b = (w.astype(BF16) for w in (w_out, w_cq, w_ckv, w_co))
    w_gate_b, w_up_b, w_down_b = (w.astype(BF16) for w in (w_gate, w_up, w_down))
    lam_q = jnp.concatenate([lam_q1, lam_q2], axis=1).reshape(depth, 1, LANES)
    lam_k = jnp.concatenate([lam_k1, lam_k2], axis=1).reshape(depth, 1, LANES)

    def scales(*groups):
        return jnp.concatenate([jnp.full((w,), v, F32) for w, v in groups]).reshape(1, -1)

    colscale = scales((SB_W, HEAD_DIM ** -0.5), (SB_W, 1.0), (FOX_W, HEAD_DIM ** -0.5 * LOG2E), (FOX_W, 1.0))
    colscale_rope = scales((A_W, DIFF_QK_DIM ** -0.5 * LOG2E), (A_W, 1.0))
    qk_blocks = tuple(_IN_OFFSETS[name][0] // SB_W for name in QK_ORDER)

    x = x.reshape(m, d)
    mem2 = mem.reshape(-1, d)
    cos_t, sin_t = rope_tables(positions, bm=bm_row)
    h = prenorm(x, g_mix_pre[0], bm=bm_row)

    for l in range(depth):
        lam_init = 0.8 - 0.6 * math.exp(-0.3 * l)
        qk = in_proj(h, w_in, l, qk_blocks, colscale, None, bm=bm_big, bn=SB_W)
        qk_rope = in_proj(h, w_in, l, (0,), colscale_rope, (cos_t, sin_t), bm=bm_mm, bn=bn_in)
        vt = in_proj_vt(h, w_vt[l], bm=bm_big, bn=bn_in, tk=tq)
        fk, fq = forget_features(h, wf_rep[l], bf_rep[l], batch=batch, blk=tq)
        oa = diff_attention(qk_rope, vt, lam_q[l], lam_k[l], g_diff_sub[l].reshape(1, LANES), lam_init, batch=batch, tq=tq,
                            lag=lag_diff)
        ob = stick_breaking_attention(qk, vt, batch=batch, tq=tq, lag=lag_sb)
        oc = forgetting_attention(qk, vt, fk, fq, batch=batch, tq=tq, lag=lag_fox)
        x, h = mix_out(oa, ob, oc, x, w_out_b[l], g_sb_out[l], g_fox_out[l], g_mix_post[l], g_x_pre[l], bm=bm_row)
        kv = norm_matmul(mem2, g_mem[l], w_ckv_b[l], bm=bm_mem)
        x, h = cross_attention(h, x, kv, w_cq_b[l], w_co_b[l], g_x_post[l], g_ffn_pre[l], batch=batch, bm=bm_x)
        a = ffn_up(h, w_gate_b[l], w_up_b[l], bm=bm_mm, bn=bn_ff)
        g_next = g_mix_pre[l + 1] if l + 1 < depth else None
        x, *rest = ffn_down(a, w_down_b[l], x, g_ffn_post[l], g_next, bm=bm_row, bk=bk_ff)
        h = rest[0] if rest else None
    return x.reshape(batch, seq, d)
```

```python
import functools
import math

import jax
import jax.numpy as jnp
from jax import lax
from jax.experimental import pallas as pl
from jax.experimental.pallas import tpu as pltpu

F32 = jnp.float32
BF16 = jnp.bfloat16

EPS = 1e-6
NEG_INF = -1e30
ROPE_THETA = 10000.0
LOG2E = math.log2(math.e)

LANES = 128
HEAD_DIM = 128
N_DIFF, N_SB, N_FOX = 4, 6, 6
DIFF_QK_DIM = HEAD_DIM // 2
A_W, SB_W, FOX_W = N_DIFF * HEAD_DIM, N_SB * HEAD_DIM, N_FOX * HEAD_DIM
V_ROWS = A_W + SB_W + FOX_W
N_CROSS_HEADS = 4
CROSS_W = N_CROSS_HEADS * HEAD_DIM
GATE_PARTS = 3
GATE_LANES = GATE_PARTS * N_FOX

VMEM_LIMIT = 56 * 1024 * 1024

_IN_OFFSETS = {}
_off = 0
for _name, _width in (("qa", A_W), ("ka", A_W), ("va", A_W), ("qb", SB_W), ("kb", SB_W), ("vb", SB_W),
                      ("qc", FOX_W), ("kc", FOX_W), ("vc", FOX_W), ("fc", N_FOX)):
    _IN_OFFSETS[_name] = (_off, _off + _width)
    _off += _width
QK_ORDER = ("qb", "kb", "qc", "kc")
QK_ROPE_ORDER = ("qa", "ka")
V_ORDER = ("vb", "vc", "va")
QB_COL, QC_COL, QA_COL = 0, 2, 0
VB_ROW, VC_ROW, VA_ROW = 0, 1, (2 * SB_W) // A_W


def _params(*semantics):
    return pltpu.CompilerParams(dimension_semantics=semantics, vmem_limit_bytes=VMEM_LIMIT)


def _rms(xf, g):
    return xf * lax.rsqrt(jnp.mean(xf * xf, axis=-1, keepdims=True) + EPS) * g


def _dot(a, b):
    return jnp.dot(a, b, preferred_element_type=F32)


def _dot_nt(a, b):
    return lax.dot_general(a, b, (((1,), (1,)), ((), ())), preferred_element_type=F32)


def _bf16_parts(x):
    p0 = x.astype(BF16)
    r1 = x - p0.astype(F32)
    p1 = r1.astype(BF16)
    p2 = (r1 - p1.astype(F32)).astype(BF16)
    return p0, p1, p2


def _prenorm_kernel(x_ref, g_ref, h_ref):
    h_ref[...] = _rms(x_ref[...], g_ref[...]).astype(BF16)


def prenorm(x, g, *, bm):
    m, d = x.shape
    return pl.pallas_call(
        _prenorm_kernel,
        grid=(m // bm,),
        in_specs=[pl.BlockSpec((bm, d), lambda i: (i, 0)), pl.BlockSpec((1, d), lambda i: (0, 0))],
        out_specs=pl.BlockSpec((bm, d), lambda i: (i, 0)),
        out_shape=jax.ShapeDtypeStruct((m, d), BF16),
        compiler_params=_params("parallel"),
        name="prenorm",
    )(x, g.reshape(1, d))


def _rope_table_kernel(pos_ref, invf_ref, sign_ref, cos_ref, sin_ref):
    ang = pos_ref[...].astype(F32) * invf_ref[...]
    cos_ref[...] = jnp.cos(ang)
    sin_ref[...] = jnp.sin(ang) * sign_ref[...]


def rope_tables(positions, *, bm):
    m = positions.size
    half = DIFF_QK_DIM // 2
    inv_freq = ROPE_THETA ** (-jnp.arange(half, dtype=F32) / half)
    invf = jnp.tile(inv_freq, LANES // half).reshape(1, LANES)
    sign = jnp.where((jnp.arange(LANES) % DIFF_QK_DIM) < half, -1.0, 1.0).astype(F32).reshape(1, LANES)
    row = pl.BlockSpec((bm, LANES), lambda i: (i, 0))
    const = pl.BlockSpec((1, LANES), lambda i: (0, 0))
    return pl.pallas_call(
        _rope_table_kernel,
        grid=(m // bm,),
        in_specs=[pl.BlockSpec((bm, 1), lambda i: (i, 0)), const, const],
        out_specs=[row, row],
        out_shape=[jax.ShapeDtypeStruct((m, LANES), F32)] * 2,
        compiler_params=_params("parallel"),
        name="rope_tables",
    )(positions.reshape(m, 1), invf, sign)


def _in_proj_kernel(h_ref, w_ref, cs_ref, o_ref):
    o_ref[...] = (_dot(h_ref[...], w_ref[...]) * cs_ref[...]).astype(BF16)


def _in_proj_rope_kernel(h_ref, w_ref, cs_ref, cos_ref, sin_ref, o_ref):
    acc = _dot(h_ref[...], w_ref[...]) * cs_ref[...]
    bm, bn = acc.shape
    c, s = cos_ref[...], sin_ref[...]
    first_half = (lax.broadcasted_iota(jnp.int32, (bm, LANES), 1) % DIFF_QK_DIM) < DIFF_QK_DIM // 2
    for t in range(bn // LANES):
        a = acc[:, t * LANES:(t + 1) * LANES]
        partner = jnp.where(first_half, pltpu.roll(a, LANES - DIFF_QK_DIM // 2, 1),
                            pltpu.roll(a, DIFF_QK_DIM // 2, 1))
        o_ref[:, t * LANES:(t + 1) * LANES] = (a * c + partner * s).astype(BF16)


def in_proj(h, w, colscale, rope_tables_or_none, *, bm, bn):
    m, d = h.shape
    n = w.shape[1]
    assert n % bn == 0 and m % bm == 0
    rope = rope_tables_or_none is not None
    row_tab = pl.BlockSpec((bm, LANES), lambda i, j: (i, 0))
    return pl.pallas_call(
        _in_proj_rope_kernel if rope else _in_proj_kernel,
        grid=(m // bm, n // bn),
        in_specs=[pl.BlockSpec((bm, d), lambda i, j: (i, 0)),
                  pl.BlockSpec((d, bn), lambda i, j: (0, j)),
                  pl.BlockSpec((1, bn), lambda i, j: (0, j))] + ([row_tab, row_tab] if rope else []),
        out_specs=pl.BlockSpec((bm, bn), lambda i, j: (i, j)),
        out_shape=jax.ShapeDtypeStruct((m, n), BF16),
        compiler_params=_params("parallel", "arbitrary"),
        name="in_proj_rope" if rope else "in_proj",
    )(h, w, colscale, *(rope_tables_or_none or ()))


def _in_proj_vt_kernel(h_ref, wt_ref, o_ref, *, tk):
    res = _dot_nt(wt_ref[...], h_ref[...]).astype(BF16)
    for c in range(o_ref.shape[0]):
        o_ref[c] = res[:, c * tk:(c + 1) * tk]


def in_proj_vt(h, wt, *, bm, bn, tk):
    m, d = h.shape
    n = wt.shape[0]
    return pl.pallas_call(
        functools.partial(_in_proj_vt_kernel, tk=tk),
        grid=(m // bm, n // bn),
        in_specs=[pl.BlockSpec((bm, d), lambda i, j: (i, 0)),
                  pl.BlockSpec((bn, d), lambda i, j: (j, 0))],
        out_specs=pl.BlockSpec((bm // tk, bn, tk), lambda i, j: (i, j, 0)),
        out_shape=jax.ShapeDtypeStruct((m // tk, n, tk), BF16),
        compiler_params=_params("parallel", "arbitrary"),
        name="in_proj_vt",
    )(h, wt)


def _forget_features_kernel(h_ref, wf_ref, bf_ref, part_ref, fk_ref, fq_ref, *, blk):
    s = h_ref.shape[0]
    lane = lax.broadcasted_iota(jnp.int32, (blk, LANES), 1)
    part = jnp.broadcast_to(part_ref[...], (blk, LANES))
    lower = (lax.broadcasted_iota(jnp.int32, (blk, blk), 0)
             >= lax.broadcasted_iota(jnp.int32, (blk, blk), 1)).astype(BF16)
    one = jnp.ones((blk, LANES), F32)
    zero = jnp.zeros((blk, LANES), F32)
    carry = jnp.zeros((1, LANES), F32)
    for t in range(s // blk):
        rows = slice(t * blk, (t + 1) * blk)
        fc = _dot(h_ref[rows, :], wf_ref[...]) + bf_ref[...]
        log_f = (jnp.minimum(fc, 0.0) - jnp.log1p(jnp.exp(-jnp.abs(fc)))) * LOG2E
        x0, x1, x2 = _bf16_parts(log_f)
        cum = _dot(lower, x0) + _dot(lower, x1) + _dot(lower, x2) + carry
        carry = cum[blk - 1:blk, :]
        c0, c1, c2 = (c.astype(F32) for c in _bf16_parts(cum))
        parts = jnp.where(part == 0, c0, jnp.where(part == 1, c1, c2))
        fk = jnp.where(lane < GATE_LANES, parts, jnp.where(lane < 2 * GATE_LANES, one, zero))
        fq = jnp.where(lane < GATE_LANES, -one, jnp.where(lane < 2 * GATE_LANES, parts, zero))
        fk_ref[rows, :] = fk.astype(BF16)
        fq_ref[rows, :] = fq.astype(BF16)


def forget_features(h, wf_rep, bf_rep, *, batch, blk):
    m, d = h.shape
    s = m // batch
    part = (jnp.arange(LANES, dtype=jnp.int32) % GATE_PARTS).reshape(1, LANES)
    rows = pl.BlockSpec((s, LANES), lambda b: (b, 0))
    vec = pl.BlockSpec((1, LANES), lambda b: (0, 0))
    return pl.pallas_call(
        functools.partial(_forget_features_kernel, blk=blk),
        grid=(batch,),
        in_specs=[pl.BlockSpec((s, d), lambda b: (b, 0)),
                  pl.BlockSpec((d, LANES), lambda b: (0, 0)), vec, vec],
        out_specs=[rows, rows],
        out_shape=[jax.ShapeDtypeStruct((m, LANES), BF16)] * 2,
        compiler_params=_params("parallel"),
        name="forget_features",
    )(h, wf_rep, bf_rep, part)


def _head(ref, h):
    return ref[:, h * HEAD_DIM:(h + 1) * HEAD_DIM]


def _key_block(ref, kb, tk, h):
    return ref[pl.ds(pl.multiple_of(kb * tk, tk), tk), h * HEAD_DIM:(h + 1) * HEAD_DIM]


def _value_block_t(ref, kb, h):
    return ref[kb, h * HEAD_DIM:(h + 1) * HEAD_DIM, :]


def _run_pipeline(i, n_heads, stages, lag, prepare):
    n = len(stages)
    lead = [(n - 1 - k) * lag for k in range(n)]
    assert lead[0] <= n_heads
    unprepared = set(range(n_heads))

    def positions(cur, nxt, cur_masked, first, last):
        for t in range(first, last):
            for k in range(n):
                idx = t + lead[k]
                if 0 <= idx < n_heads:
                    if k == 0 and idx in unprepared:
                        unprepared.discard(idx)
                        prepare(idx)
                    stages[k](cur, idx, cur_masked)
                elif idx >= n_heads and nxt is not None:
                    stages[k](nxt, idx - n_heads, False)

    positions(i, None, True, -lead[0], 0)
    positions(i, jnp.maximum(i - 1, 0), True, 0, n_heads)

    def body(j, carry):
        cur = i - j
        positions(cur, cur - 1, False, 0, n_heads)
        return carry

    lax.fori_loop(1, i, body, 0)

    @pl.when(i > 0)
    def _():
        positions(0, None, False, 0, n_heads)


def _softmax_stages(scores, values_t, keep, m_ref, l_ref, acc_ref, s_buf, mn_buf, al_buf, ps_buf, pv_buf):
    def stage_scores(kb, h, masked):
        s_buf[h] = scores(kb, h)

    def stage_probabilities(kb, h, masked):
        s = s_buf[h]
        if masked:
            s = jnp.where(keep, s, NEG_INF)
        m_prev = m_ref[h]
        m_new = jnp.maximum(m_prev, jnp.max(s, axis=0, keepdims=True))
        p = jnp.exp2(s - m_new)
        mn_buf[h] = m_new
        al_buf[h] = jnp.exp2(m_prev - m_new)
        ps_buf[h] = jnp.sum(p, axis=0, keepdims=True)
        pv_buf[h] = _dot(values_t(kb, h), p.astype(BF16))

    def stage_commit(kb, h, masked):
        alpha = al_buf[h]
        m_ref[h] = mn_buf[h]
        l_ref[h] = alpha * l_ref[h] + ps_buf[h]
        acc_ref[h] = alpha * acc_ref[h] + pv_buf[h]

    return stage_scores, stage_probabilities, stage_commit


def _softmax_scratch(n_heads, tk, nq):
    stat = pltpu.VMEM((n_heads, 1, nq), F32)
    wide = pltpu.VMEM((n_heads, HEAD_DIM, nq), F32)
    return [stat, stat, wide,
            pltpu.VMEM((n_heads, tk, nq), F32), stat, stat, stat, wide]


def _init_softmax_stats(m_ref, l_ref, acc_ref):
    m_ref[...] = jnp.full(m_ref.shape, NEG_INF, F32)
    l_ref[...] = jnp.zeros(l_ref.shape, F32)
    acc_ref[...] = jnp.zeros(acc_ref.shape, F32)


def _group_specs(s, tq, width, q_col, v_row):
    nq = s // tq
    q_spec = pl.BlockSpec((tq, width), lambda b, i: (b * nq + i, q_col))
    k_spec = pl.BlockSpec((s, width), lambda b, i: (b, q_col + 1))
    vt_spec = pl.BlockSpec((nq, width, tq), lambda b, i: (b, v_row, 0))
    o_spec = pl.BlockSpec((tq, width), lambda b, i: (b * nq + i, 0))
    return q_spec, k_spec, vt_spec, o_spec


def _diff_attn_kernel(lq_ref, lk_ref, g_ref, q_ref, k_ref, vt_ref, o_ref, qq_ref, m_ref, l_ref, acc_ref,
                      *bufs, tq, lam_init, lag):
    i = pl.program_id(1)
    low = lax.broadcasted_iota(jnp.int32, (HEAD_DIM, tq), 0) < DIFF_QK_DIM

    def prepare(h):
        q_t = _head(q_ref, h).astype(F32).T
        qq_ref[h, :, 0:tq] = jnp.where(low, q_t, 0.0).astype(BF16)
        qq_ref[h, :, tq:2 * tq] = jnp.where(low, 0.0, q_t).astype(BF16)

    _init_softmax_stats(m_ref, l_ref, acc_ref)

    key = lax.broadcasted_iota(jnp.int32, (tq, 2 * tq), 0)
    qry = lax.broadcasted_iota(jnp.int32, (tq, 2 * tq), 1)
    keep = jnp.where(qry >= tq, qry - tq, qry) >= key

    def scores(kb, h):
        return _dot(_key_block(k_ref, kb, tq, h), qq_ref[h])

    stages = _softmax_stages(scores, lambda kb, h: _value_block_t(vt_ref, kb, h), keep,
                             m_ref, l_ref, acc_ref, *bufs)
    _run_pipeline(i, N_DIFF, stages, lag, prepare)

    prod = lq_ref[...] * lk_ref[...]
    first = lax.broadcasted_iota(jnp.int32, prod.shape, 1) < DIFF_QK_DIM
    e1 = jnp.exp(jnp.sum(jnp.where(first, prod, 0.0), axis=1, keepdims=True))
    e2 = jnp.exp(jnp.sum(jnp.where(first, 0.0, prod), axis=1, keepdims=True))
    lam = (e1 - e2) + lam_init
    for h in range(N_DIFF):
        o_t = (acc_ref[h, :, 0:tq] / l_ref[h, :, 0:tq]
               - lam * (acc_ref[h, :, tq:2 * tq] / l_ref[h, :, tq:2 * tq]))
        o_ref[:, h * HEAD_DIM:(h + 1) * HEAD_DIM] = (_rms(o_t.T, g_ref[...]) * (1.0 - lam_init)).astype(BF16)


def diff_attention(qk, vt, lam_q, lam_k, g_sub, lam_init, *, batch, tq, lag):
    m = qk.shape[0]
    s = m // batch
    vec = pl.BlockSpec((1, LANES), lambda b, i: (0, 0))
    q_spec, k_spec, vt_spec, o_spec = _group_specs(s, tq, A_W, QA_COL, VA_ROW)
    return pl.pallas_call(
        functools.partial(_diff_attn_kernel, tq=tq, lam_init=lam_init, lag=lag),
        grid=(batch, s // tq),
        in_specs=[vec, vec, vec, q_spec, k_spec, vt_spec],
        out_specs=o_spec,
        out_shape=jax.ShapeDtypeStruct((m, A_W), BF16),
        scratch_shapes=[pltpu.VMEM((N_DIFF, HEAD_DIM, 2 * tq), BF16)] + _softmax_scratch(N_DIFF, tq, 2 * tq),
        compiler_params=_params("parallel", "arbitrary"),
        name="diff_attention",
    )(lam_q, lam_k, g_sub, qk, qk, vt)


def _sb_attn_kernel(q_ref, k_ref, vt_ref, o_ref, qn_ref, r_ref, acc_ref,
                    zn_buf, after_buf, cs_buf, pv_buf, *, tq, lag):
    i = pl.program_id(1)
    def prepare(h):
        qn_ref[h] = (-_head(q_ref, h).astype(F32)).T.astype(BF16)

    r_ref[...] = jnp.zeros(r_ref.shape, F32)
    acc_ref[...] = jnp.zeros(acc_ref.shape, F32)
    ss = lax.broadcasted_iota(jnp.int32, (tq, 2 * tq), 0)
    jj = lax.broadcasted_iota(jnp.int32, (tq, 2 * tq), 1)
    tri = (jnp.where(jj >= tq, jj - tq, jj) >= ss).astype(BF16)

    strict = lax.broadcasted_iota(jnp.int32, (tq, tq), 0) < lax.broadcasted_iota(jnp.int32, (tq, tq), 1)

    def scores(kb, h, masked):
        zn_buf[h] = _dot(_key_block(k_ref, kb, tq, h), qn_ref[h])

    def suffix_sums(kb, h, masked):
        zn = zn_buf[h]
        sp = jnp.log(1.0 + jnp.exp2(jnp.abs(zn) * -LOG2E))
        log_remain = jnp.minimum(zn, 0.0) - sp
        if masked:
            log_remain = jnp.where(strict, log_remain, 0.0)
        hi = log_remain.astype(BF16)
        lo = (log_remain - hi.astype(F32)).astype(BF16)
        after_buf[h] = _dot(tri, jnp.concatenate([hi, lo], axis=0))
        cs_buf[h] = jnp.sum(log_remain, axis=0, keepdims=True)

    def weights(kb, h, masked):
        w = jnp.exp((after_buf[h] + r_ref[h]) - zn_buf[h])
        if masked:
            w = jnp.where(strict, w, 0.0)
        pv_buf[h] = _dot(_value_block_t(vt_ref, kb, h), w.astype(BF16))
        r_ref[h] += cs_buf[h]

    def accumulate(kb, h, masked):
        acc_ref[h] += pv_buf[h]

    _run_pipeline(i, N_SB, (scores, suffix_sums, weights, accumulate), lag, prepare)
    for h in range(N_SB):
        o_ref[:, h * HEAD_DIM:(h + 1) * HEAD_DIM] = acc_ref[h].T.astype(BF16)


def stick_breaking_attention(qk, vt, *, batch, tq, lag):
    m = qk.shape[0]
    s = m // batch
    q_spec, k_spec, vt_spec, o_spec = _group_specs(s, tq, SB_W, QB_COL, VB_ROW)
    stat = pltpu.VMEM((N_SB, 1, tq), F32)
    wide = pltpu.VMEM((N_SB, HEAD_DIM, tq), F32)
    tile = pltpu.VMEM((N_SB, tq, tq), F32)
    return pl.pallas_call(
        functools.partial(_sb_attn_kernel, tq=tq, lag=lag),
        grid=(batch, s // tq),
        in_specs=[q_spec, k_spec, vt_spec],
        out_specs=o_spec,
        out_shape=jax.ShapeDtypeStruct((m, SB_W), BF16),
        scratch_shapes=[pltpu.VMEM((N_SB, HEAD_DIM, tq), BF16), stat, wide,
                        tile, tile, stat, wide],
        compiler_params=_params("parallel", "arbitrary"),
        name="stick_breaking_attention",
    )(qk, qk, vt)


def _fox_attn_kernel(fq_ref, fk_ref, q_ref, k_ref, vt_ref, o_ref, qa_ref, m_ref, l_ref, acc_ref, *bufs, tq, lag):
    i = pl.program_id(1)
    feat = lax.broadcasted_iota(jnp.int32, (LANES, tq), 0)
    fq_t = fq_ref[...].astype(F32).T

    def prepare(h):
        lo = GATE_PARTS * h
        mine = (((feat >= lo) & (feat < lo + GATE_PARTS))
                | ((feat >= GATE_LANES + lo) & (feat < GATE_LANES + lo + GATE_PARTS)))
        qa_ref[h, 0:HEAD_DIM, :] = _head(q_ref, h).astype(F32).T.astype(BF16)
        qa_ref[h, HEAD_DIM:2 * HEAD_DIM, :] = jnp.where(mine, fq_t, 0.0).astype(BF16)

    _init_softmax_stats(m_ref, l_ref, acc_ref)

    keep = lax.broadcasted_iota(jnp.int32, (tq, tq), 1) >= lax.broadcasted_iota(jnp.int32, (tq, tq), 0)

    def scores(kb, h):
        fk = fk_ref[pl.ds(pl.multiple_of(kb * tq, tq), tq), :]
        return _dot(jnp.concatenate([_key_block(k_ref, kb, tq, h), fk], axis=1), qa_ref[h])

    stages = _softmax_stages(scores, lambda kb, h: _value_block_t(vt_ref, kb, h), keep,
                             m_ref, l_ref, acc_ref, *bufs)
    _run_pipeline(i, N_FOX, stages, lag, prepare)
    for h in range(N_FOX):
        o_ref[:, h * HEAD_DIM:(h + 1) * HEAD_DIM] = (acc_ref[h] / l_ref[h]).T.astype(BF16)


def forgetting_attention(qk, vt, fk, fq, *, batch, tq, lag):
    m = qk.shape[0]
    s = m // batch
    nq = s // tq
    q_spec, k_spec, vt_spec, o_spec = _group_specs(s, tq, FOX_W, QC_COL, VC_ROW)
    return pl.pallas_call(
        functools.partial(_fox_attn_kernel, tq=tq, lag=lag),
        grid=(batch, nq),
        in_specs=[pl.BlockSpec((tq, LANES), lambda b, i: (b * nq + i, 0)),
                  pl.BlockSpec((s, LANES), lambda b, i: (b, 0)),
                  q_spec, k_spec, vt_spec],
        out_specs=o_spec,
        out_shape=jax.ShapeDtypeStruct((m, FOX_W), BF16),
        scratch_shapes=[pltpu.VMEM((N_FOX, 2 * HEAD_DIM, tq), BF16)] + _softmax_scratch(N_FOX, tq, tq),
        compiler_params=_params("parallel", "arbitrary"),
        name="forgetting_attention",
    )(fq, fk, qk, qk, vt)


def _residual_tail(y, x, g_post, g_next, x_out_ref, h_out_ref):
    x_new = x + _rms(y, g_post)
    x_out_ref[...] = x_new
    if h_out_ref is not None:
        h_out_ref[...] = _rms(x_new, g_next).astype(BF16)


def _mix_out_kernel(oa_ref, ob_ref, oc_ref, x_ref, w_ref, gb_ref, gc_ref, gp_ref, gn_ref, xo_ref, ho_ref):
    ob = _rms(ob_ref[...].astype(F32), gb_ref[...]).astype(BF16)
    oc = _rms(oc_ref[...].astype(F32), gc_ref[...]).astype(BF16)
    y = (_dot(oa_ref[...], w_ref[0:A_W, :]) + _dot(ob, w_ref[A_W:A_W + SB_W, :])
         + _dot(oc, w_ref[A_W + SB_W:, :]))
    _residual_tail(y, x_ref[...], gp_ref[...], gn_ref[...], xo_ref, ho_ref)


def mix_out(oa, ob, oc, x, w, g_sb, g_fox, g_post, g_next, *, bm):
    m, d = x.shape
    rows = lambda width: pl.BlockSpec((bm, width), lambda i: (i, 0))
    const = lambda width: pl.BlockSpec((1, width), lambda i: (0, 0))
    return pl.pallas_call(
        _mix_out_kernel,
        grid=(m // bm,),
        in_specs=[rows(A_W), rows(SB_W), rows(FOX_W), rows(d),
                  pl.BlockSpec(w.shape, lambda i: (0, 0)),
                  const(SB_W), const(FOX_W), const(d), const(d)],
        out_specs=[rows(d), rows(d)],
        out_shape=[jax.ShapeDtypeStruct((m, d), F32), jax.ShapeDtypeStruct((m, d), BF16)],
        compiler_params=_params("parallel"),
        name="mix_out",
    )(oa, ob, oc, x, w, g_sb.reshape(1, -1), g_fox.reshape(1, -1), g_post.reshape(1, -1), g_next.reshape(1, -1))


def _norm_matmul_kernel(x_ref, g_ref, w_ref, o_ref):
    o_ref[...] = _dot(_rms(x_ref[...], g_ref[...]).astype(BF16), w_ref[...]).astype(BF16)


def norm_matmul(x, g, w, *, bm):
    m, d = x.shape
    n = w.shape[1]
    return pl.pallas_call(
        _norm_matmul_kernel,
        grid=(m // bm,),
        in_specs=[pl.BlockSpec((bm, d), lambda i: (i, 0)), pl.BlockSpec((1, d), lambda i: (0, 0)),
                  pl.BlockSpec((d, n), lambda i: (0, 0))],
        out_specs=pl.BlockSpec((bm, n), lambda i: (i, 0)),
        out_shape=jax.ShapeDtypeStruct((m, n), BF16),
        compiler_params=_params("parallel"),
        name="memory_kv",
    )(x, g.reshape(1, d), w)


def _cross_attn_kernel(h_ref, x_ref, kv_ref, wq_ref, wo_ref, gp_ref, gn_ref, xo_ref, ho_ref):
    q = (_dot(h_ref[...], wq_ref[...]) * HEAD_DIM ** -0.5).astype(BF16)
    heads = []
    for hd in range(N_CROSS_HEADS):
        k = kv_ref[:, hd * HEAD_DIM:(hd + 1) * HEAD_DIM]
        v = kv_ref[:, CROSS_W + hd * HEAD_DIM:CROSS_W + (hd + 1) * HEAD_DIM]
        s = _dot_nt(q[:, hd * HEAD_DIM:(hd + 1) * HEAD_DIM], k)
        p = jnp.exp(s - jnp.max(s, axis=1, keepdims=True))
        o = _dot(p.astype(BF16), v) / jnp.sum(p, axis=1, keepdims=True)
        heads.append(o.astype(BF16))
    y = _dot(jnp.concatenate(heads, axis=1), wo_ref[...])
    _residual_tail(y, x_ref[...], gp_ref[...], gn_ref[...], xo_ref, ho_ref)


def cross_attention(h, x, kv, wq, wo, g_post, g_next, *, batch, bm):
    m, d = x.shape
    s = m // batch
    mem_len = kv.shape[0] // batch
    nb = s // bm
    rows = lambda width: pl.BlockSpec((bm, width), lambda b, i: (b * nb + i, 0))
    const = lambda shape: pl.BlockSpec(shape, lambda b, i: (0, 0))
    return pl.pallas_call(
        _cross_attn_kernel,
        grid=(batch, nb),
        in_specs=[rows(d), rows(d),
                  pl.BlockSpec((mem_len, 2 * CROSS_W), lambda b, i: (b, 0)),
                  const(wq.shape), const(wo.shape), const((1, d)), const((1, d))],
        out_specs=[rows(d), rows(d)],
        out_shape=[jax.ShapeDtypeStruct((m, d), F32), jax.ShapeDtypeStruct((m, d), BF16)],
        compiler_params=_params("parallel", "parallel"),
        name="cross_attention",
    )(h, x, kv, wq, wo, g_post.reshape(1, d), g_next.reshape(1, d))


def _ffn_up_kernel(h_ref, wg_ref, wu_ref, o_ref):
    h = h_ref[...]
    gate = _dot(h, wg_ref[...])
    up = _dot(h, wu_ref[...])
    o_ref[...] = (gate * jax.nn.sigmoid(gate) * up).astype(BF16)


def ffn_up(h, wg, wu, *, bm, bn):
    m, d = h.shape
    n = wg.shape[1]
    wspec = pl.BlockSpec((d, bn), lambda i, j: (0, j))
    return pl.pallas_call(
        _ffn_up_kernel,
        grid=(m // bm, n // bn),
        in_specs=[pl.BlockSpec((bm, d), lambda i, j: (i, 0)), wspec, wspec],
        out_specs=pl.BlockSpec((bm, bn), lambda i, j: (i, j)),
        out_shape=jax.ShapeDtypeStruct((m, n), BF16),
        compiler_params=_params("parallel", "arbitrary"),
        name="ffn_up",
    )(h, wg, wu)


def _ffn_down_kernel(*refs, emit_h):
    if emit_h:
        a_ref, w_ref, x_ref, gp_ref, gn_ref, xo_ref, ho_ref = refs
    else:
        a_ref, w_ref, x_ref, gp_ref, xo_ref = refs
        gn_ref = ho_ref = None
    kk = pl.program_id(1)
    last = pl.num_programs(1) - 1
    part = _dot(a_ref[...], w_ref[...])

    @pl.when(kk == 0)
    def _():
        xo_ref[...] = part

    @pl.when((kk > 0) & (kk < last))
    def _():
        xo_ref[...] += part

    @pl.when(kk == last)
    def _():
        g_next = gn_ref[...] if emit_h else None
        _residual_tail(xo_ref[...] + part, x_ref[...], gp_ref[...], g_next, xo_ref, ho_ref)


def ffn_down(a, w, x, g_post, g_next, *, bm, bk):
    m, d = x.shape
    kdim = a.shape[1]
    emit_h = g_next is not None
    assert kdim // bk >= 2
    rows = pl.BlockSpec((bm, d), lambda i, k: (i, 0))
    const = pl.BlockSpec((1, d), lambda i, k: (0, 0))
    gains = [g_post.reshape(1, d)] + ([g_next.reshape(1, d)] if emit_h else [])
    out_shape = [jax.ShapeDtypeStruct((m, d), F32)] + ([jax.ShapeDtypeStruct((m, d), BF16)] if emit_h else [])
    return pl.pallas_call(
        functools.partial(_ffn_down_kernel, emit_h=emit_h),
        grid=(m // bm, kdim // bk),
        in_specs=[pl.BlockSpec((bm, bk), lambda i, k: (i, k)),
                  pl.BlockSpec((bk, d), lambda i, k: (k, 0)),
                  rows] + [const] * len(gains),
        out_specs=[rows] * len(out_shape),
        out_shape=out_shape,
        compiler_params=_params("parallel", "arbitrary"),
        name="ffn_down",
    )(a, w, x, *gains)


def _pick(n, *candidates):
    for c in candidates:
        if n % c == 0:
            return c
    return n


def _columns(w, names):
    return jnp.concatenate([w[..., _IN_OFFSETS[n][0]:_IN_OFFSETS[n][1]] for n in names], axis=-1)


def kernel(x, mem, positions, g_mix_pre, g_mix_post, w_in, b_f, lam_q1, lam_k1, lam_q2, lam_k2, g_diff_sub, g_sb_out, g_fox_out, w_out, g_x_pre, g_x_post, g_mem, w_cq, w_ckv, w_co, g_ffn_pre, g_ffn_post, w_gate, w_up, w_down):
    batch, seq, d = x.shape
    depth = w_in.shape[0]
    m = batch * seq
    d_ff = w_gate.shape[2]
    assert w_in.shape[2] == _IN_OFFSETS["fc"][1] and d == V_ROWS

    bm_row = _pick(m, 512, 256, 128)
    bm_mm = _pick(m, 1024, 512, 256, 128)
    bm_big = _pick(m, 2048, 1024, 512, 256, 128)
    lag_diff, lag_sb, lag_fox = 2, 2, 3
    bn_in = 2 * A_W
    bn_ff = _pick(d_ff, 512, 256, 128)
    bk_ff = _pick(d_ff, 2816, 1408, 1024, 512, 256, 128)
    tq = _pick(seq, 256, 128)
    bm_x = _pick(seq, 512, 256, 128)
    bm_mem = _pick(mem.shape[0] * mem.shape[1], 512, 256, 128)

    w_qk = _columns(w_in, QK_ORDER).astype(BF16)
    w_qk_rope = _columns(w_in, QK_ROPE_ORDER).astype(BF16)
    w_vt = jnp.swapaxes(_columns(w_in, V_ORDER), 1, 2).astype(BF16)
    gate_lane_head = jnp.arange(2 * GATE_LANES) % GATE_LANES // GATE_PARTS
    wf_rep = jnp.pad(_columns(w_in, ("fc",))[:, :, gate_lane_head],
                     ((0, 0), (0, 0), (0, LANES - 2 * GATE_LANES))).astype(BF16)
    bf_rep = jnp.pad(b_f[:, gate_lane_head], ((0, 0), (0, LANES - 2 * GATE_LANES))).reshape(depth, 1, LANES)
    w_out_b, w_cq_b, w_ckv_b, w_co_b = (w.astype(BF16) for w in (w_out, w_cq, w_ckv, w_co))
    w_gate_b, w_up_b, w_down_b = (w.astype(BF16) for w in (w_gate, w_up, w_down))
    lam_q = jnp.concatenate([lam_q1, lam_q2], axis=1).reshape(depth, 1, LANES)
    lam_k = jnp.concatenate([lam_k1, lam_k2], axis=1).reshape(depth, 1, LANES)

    def scales(*groups):
        return jnp.concatenate([jnp.full((w,), v, F32) for w, v in groups]).reshape(1, -1)

    colscale = scales((SB_W, HEAD_DIM ** -0.5), (SB_W, 1.0), (FOX_W, HEAD_DIM ** -0.5 * LOG2E), (FOX_W, 1.0))
    colscale_rope = scales((A_W, DIFF_QK_DIM ** -0.5 * LOG2E), (A_W, 1.0))

    x = x.reshape(m, d)
    mem2 = mem.reshape(-1, d)
    cos_t, sin_t = rope_tables(positions, bm=bm_row)
    h = prenorm(x, g_mix_pre[0], bm=bm_row)

    for l in range(depth):
        lam_init = 0.8 - 0.6 * math.exp(-0.3 * l)
        qk = in_proj(h, w_qk[l], colscale, None, bm=bm_big, bn=bn_in)
        qk_rope = in_proj(h, w_qk_rope[l], colscale_rope, (cos_t, sin_t), bm=bm_big, bn=bn_in)
        vt = in_proj_vt(h, w_vt[l], bm=bm_big, bn=bn_in, tk=tq)
        fk, fq = forget_features(h, wf_rep[l], bf_rep[l], batch=batch, blk=tq)
        oa = diff_attention(qk_rope, vt, lam_q[l], lam_k[l], g_diff_sub[l].reshape(1, LANES), lam_init, batch=batch, tq=tq,
                            lag=lag_diff)
        ob = stick_breaking_attention(qk, vt, batch=batch, tq=tq, lag=lag_sb)
        oc = forgetting_attention(qk, vt, fk, fq, batch=batch, tq=tq, lag=lag_fox)
        x, h = mix_out(oa, ob, oc, x, w_out_b[l], g_sb_out[l], g_fox_out[l], g_mix_post[l], g_x_pre[l], bm=bm_row)
        kv = norm_matmul(mem2, g_mem[l], w_ckv_b[l], bm=bm_mem)
        x, h = cross_attention(h, x, kv, w_cq_b[l], w_co_b[l], g_x_post[l], g_ffn_pre[l], batch=batch, bm=bm_x)
        a = ffn_up(h, w_gate_b[l], w_up_b[l], bm=bm_mm, bn=bn_ff)
        g_next = g_mix_pre[l + 1] if l + 1 < depth else None
        x, *rest = ffn_down(a, w_down_b[l], x, g_ffn_post[l], g_next, bm=bm_row, bk=bk_ff)
        h = rest[0] if rest else None
    return x.reshape(batch, seq, d)
```

```python
import functools
import math

import jax
import jax.numpy as jnp
from jax import lax
from jax.experimental import pallas as pl
from jax.experimental.pallas import tpu as pltpu

F32 = jnp.float32
BF16 = jnp.bfloat16

EPS = 1e-6
NEG_INF = -1e30
ROPE_THETA = 10000.0
LOG2E = math.log2(math.e)

LANES = 128
HEAD_DIM = 128
N_DIFF, N_SB, N_FOX = 4, 6, 6
DIFF_QK_DIM = HEAD_DIM // 2
A_W, SB_W, FOX_W = N_DIFF * HEAD_DIM, N_SB * HEAD_DIM, N_FOX * HEAD_DIM
V_ROWS = A_W + SB_W + FOX_W
N_CROSS_HEADS = 4
CROSS_W = N_CROSS_HEADS * HEAD_DIM
GATE_PARTS = 3
GATE_LANES = GATE_PARTS * N_FOX

VMEM_LIMIT = 56 * 1024 * 1024

_IN_OFFSETS = {}
_off = 0
for _name, _width in (("qa", A_W), ("ka", A_W), ("va", A_W), ("qb", SB_W), ("kb", SB_W), ("vb", SB_W),
                      ("qc", FOX_W), ("kc", FOX_W), ("vc", FOX_W), ("fc", N_FOX)):
    _IN_OFFSETS[_name] = (_off, _off + _width)
    _off += _width
QK_ORDER = ("qb", "kb", "qc", "kc")
QK_ROPE_ORDER = ("qa", "ka")
V_ORDER = ("vb", "vc", "va")
QB_COL, QC_COL, QA_COL = 0, 2, 0
VB_ROW, VC_ROW, VA_ROW = 0, 1, (2 * SB_W) // A_W


def _params(*semantics):
    return pltpu.CompilerParams(dimension_semantics=semantics, vmem_limit_bytes=VMEM_LIMIT)


def _rms(xf, g):
    return xf * lax.rsqrt(jnp.mean(xf * xf, axis=-1, keepdims=True) + EPS) * g


def _dot(a, b):
    return jnp.dot(a, b, preferred_element_type=F32)


def _dot_nt(a, b):
    return lax.dot_general(a, b, (((1,), (1,)), ((), ())), preferred_element_type=F32)


def _bf16_parts(x):
    p0 = x.astype(BF16)
    r1 = x - p0.astype(F32)
    p1 = r1.astype(BF16)
    p2 = (r1 - p1.astype(F32)).astype(BF16)
    return p0, p1, p2


def _prenorm_kernel(x_ref, g_ref, h_ref):
    h_ref[...] = _rms(x_ref[...], g_ref[...]).astype(BF16)


def prenorm(x, g, *, bm):
    m, d = x.shape
    return pl.pallas_call(
        _prenorm_kernel,
        grid=(m // bm,),
        in_specs=[pl.BlockSpec((bm, d), lambda i: (i, 0)), pl.BlockSpec((1, d), lambda i: (0, 0))],
        out_specs=pl.BlockSpec((bm, d), lambda i: (i, 0)),
        out_shape=jax.ShapeDtypeStruct((m, d), BF16),
        compiler_params=_params("parallel"),
        name="prenorm",
    )(x, g.reshape(1, d))


def _rope_table_kernel(pos_ref, invf_ref, sign_ref, cos_ref, sin_ref):
    ang = pos_ref[...].astype(F32) * invf_ref[...]
    cos_ref[...] = jnp.cos(ang)
    sin_ref[...] = jnp.sin(ang) * sign_ref[...]


def rope_tables(positions, *, bm):
    m = positions.size
    half = DIFF_QK_DIM // 2
    inv_freq = ROPE_THETA ** (-jnp.arange(half, dtype=F32) / half)
    invf = jnp.tile(inv_freq, LANES // half).reshape(1, LANES)
    sign = jnp.where((jnp.arange(LANES) % DIFF_QK_DIM) < half, -1.0, 1.0).astype(F32).reshape(1, LANES)
    row = pl.BlockSpec((bm, LANES), lambda i: (i, 0))
    const = pl.BlockSpec((1, LANES), lambda i: (0, 0))
    return pl.pallas_call(
        _rope_table_kernel,
        grid=(m // bm,),
        in_specs=[pl.BlockSpec((bm, 1), lambda i: (i, 0)), const, const],
        out_specs=[row, row],
        out_shape=[jax.ShapeDtypeStruct((m, LANES), F32)] * 2,
        compiler_params=_params("parallel"),
        name="rope_tables",
    )(positions.reshape(m, 1), invf, sign)


def _in_proj_kernel(h_ref, w_ref, cs_ref, o_ref):
    o_ref[...] = (_dot(h_ref[...], w_ref[...]) * cs_ref[...]).astype(BF16)


def _in_proj_rope_kernel(h_ref, w_ref, cs_ref, cos_ref, sin_ref, o_ref):
    acc = _dot(h_ref[...], w_ref[...]) * cs_ref[...]
    bm, bn = acc.shape
    c, s = cos_ref[...], sin_ref[...]
    first_half = (lax.broadcasted_iota(jnp.int32, (bm, LANES), 1) % DIFF_QK_DIM) < DIFF_QK_DIM // 2
    for t in range(bn // LANES):
        a = acc[:, t * LANES:(t + 1) * LANES]
        partner = jnp.where(first_half, pltpu.roll(a, LANES - DIFF_QK_DIM // 2, 1),
                            pltpu.roll(a, DIFF_QK_DIM // 2, 1))
        o_ref[:, t * LANES:(t + 1) * LANES] = (a * c + partner * s).astype(BF16)


def in_proj(h, w, colscale, rope_tables_or_none, *, bm, bn):
    m, d = h.shape
    n = w.shape[1]
    assert n % bn == 0 and m % bm == 0
    rope = rope_tables_or_none is not None
    row_tab = pl.BlockSpec((bm, LANES), lambda i, j: (i, 0))
    return pl.pallas_call(
        _in_proj_rope_kernel if rope else _in_proj_kernel,
        grid=(m // bm, n // bn),
        in_specs=[pl.BlockSpec((bm, d), lambda i, j: (i, 0)),
                  pl.BlockSpec((d, bn), lambda i, j: (0, j)),
                  pl.BlockSpec((1, bn), lambda i, j: (0, j))] + ([row_tab, row_tab] if rope else []),
        out_specs=pl.BlockSpec((bm, bn), lambda i, j: (i, j)),
        out_shape=jax.ShapeDtypeStruct((m, n), BF16),
        compiler_params=_params("parallel", "arbitrary"),
        name="in_proj_rope" if rope else "in_proj",
    )(h, w, colscale, *(rope_tables_or_none or ()))


def _in_proj_vt_kernel(h_ref, wt_ref, o_ref, *, tk):
    res = _dot_nt(wt_ref[...], h_ref[...]).astype(BF16)
    for c in range(o_ref.shape[0]):
        o_ref[c] = res[:, c * tk:(c + 1) * tk]


def in_proj_vt(h, wt, *, bm, bn, tk):
    m, d = h.shape
    n = wt.shape[0]
    return pl.pallas_call(
        functools.partial(_in_proj_vt_kernel, tk=tk),
        grid=(m // bm, n // bn),
        in_specs=[pl.BlockSpec((bm, d), lambda i, j: (i, 0)),
                  pl.BlockSpec((bn, d), lambda i, j: (j, 0))],
        out_specs=pl.BlockSpec((bm // tk, bn, tk), lambda i, j: (i, j, 0)),
        out_shape=jax.ShapeDtypeStruct((m // tk, n, tk), BF16),
        compiler_params=_params("parallel", "arbitrary"),
        name="in_proj_vt",
    )(h, wt)


def _forget_features_kernel(h_ref, wf_ref, bf_ref, part_ref, fk_ref, fq_ref, *, blk):
    s = h_ref.shape[0]
    lane = lax.broadcasted_iota(jnp.int32, (blk, LANES), 1)
    part = jnp.broadcast_to(part_ref[...], (blk, LANES))
    lower = (lax.broadcasted_iota(jnp.int32, (blk, blk), 0)
             >= lax.broadcasted_iota(jnp.int32, (blk, blk), 1)).astype(BF16)
    one = jnp.ones((blk, LANES), F32)
    zero = jnp.zeros((blk, LANES), F32)
    carry = jnp.zeros((1, LANES), F32)
    for t in range(s // blk):
        rows = slice(t * blk, (t + 1) * blk)
        fc = _dot(h_ref[rows, :], wf_ref[...]) + bf_ref[...]
        log_f = (jnp.minimum(fc, 0.0) - jnp.log1p(jnp.exp(-jnp.abs(fc)))) * LOG2E
        x0, x1, x2 = _bf16_parts(log_f)
        cum = _dot(lower, x0) + _dot(lower, x1) + _dot(lower, x2) + carry
        carry = cum[blk - 1:blk, :]
        c0, c1, c2 = (c.astype(F32) for c in _bf16_parts(cum))
        parts = jnp.where(part == 0, c0, jnp.where(part == 1, c1, c2))
        fk = jnp.where(lane < GATE_LANES, parts, jnp.where(lane < 2 * GATE_LANES, one, zero))
        fq = jnp.where(lane < GATE_LANES, -one, jnp.where(lane < 2 * GATE_LANES, parts, zero))
        fk_ref[rows, :] = fk.astype(BF16)
        fq_ref[rows, :] = fq.astype(BF16)


def forget_features(h, wf_rep, bf_rep, *, batch, blk):
    m, d = h.shape
    s = m // batch
    part = (jnp.arange(LANES, dtype=jnp.int32) % GATE_PARTS).reshape(1, LANES)
    rows = pl.BlockSpec((s, LANES), lambda b: (b, 0))
    vec = pl.BlockSpec((1, LANES), lambda b: (0, 0))
    return pl.pallas_call(
        functools.partial(_forget_features_kernel, blk=blk),
        grid=(batch,),
        in_specs=[pl.BlockSpec((s, d), lambda b: (b, 0)),
                  pl.BlockSpec((d, LANES), lambda b: (0, 0)), vec, vec],
        out_specs=[rows, rows],
        out_shape=[jax.ShapeDtypeStruct((m, LANES), BF16)] * 2,
        compiler_params=_params("parallel"),
        name="forget_features",
    )(h, wf_rep, bf_rep, part)


def _head(ref, h):
    return ref[:, h * HEAD_DIM:(h + 1) * HEAD_DIM]


def _key_block(ref, kb, tk, h):
    return ref[pl.ds(pl.multiple_of(kb * tk, tk), tk), h * HEAD_DIM:(h + 1) * HEAD_DIM]


def _value_block_t(ref, kb, h):
    return ref[kb, h * HEAD_DIM:(h + 1) * HEAD_DIM, :]


def _run_pipeline(i, n_heads, stages, lag, prepare, finish):
    n = len(stages)
    lead = [(n - 1 - k) * lag for k in range(n)]
    assert lead[0] <= n_heads
    unprepared = set(range(n_heads))

    def positions(cur, nxt, cur_masked, first, last, finishing=False):
        for t in range(first, last):
            for k in range(n):
                idx = t + lead[k]
                if 0 <= idx < n_heads:
                    if k == 0 and idx in unprepared:
                        unprepared.discard(idx)
                        prepare(idx)
                    stages[k](cur, idx, cur_masked)
                    if finishing and k == n - 1:
                        finish(idx)
                elif idx >= n_heads and nxt is not None:
                    stages[k](nxt, idx - n_heads, False)

    positions(i, None, True, -lead[0], 0)
    positions(i, jnp.maximum(i - 1, 0), True, 0, n_heads)

    def body(j, carry):
        cur = i - j
        positions(cur, cur - 1, False, 0, n_heads)
        return carry

    lax.fori_loop(1, i, body, 0)

    @pl.when(i > 0)
    def _():
        positions(0, None, False, 0, n_heads, finishing=True)

    @pl.when(i == 0)
    def _():
        for h in range(n_heads):
            finish(h)


def _softmax_stages(scores, values_t, keep, m_ref, l_ref, acc_ref, s_buf, mn_buf, al_buf, ps_buf, pv_buf):
    def stage_scores(kb, h, masked):
        s_buf[h] = scores(kb, h)

    def stage_probabilities(kb, h, masked):
        s = s_buf[h]
        if masked:
            s = jnp.where(keep, s, NEG_INF)
        m_prev = m_ref[h]
        m_new = jnp.maximum(m_prev, jnp.max(s, axis=0, keepdims=True))
        p = jnp.exp2(s - m_new)
        mn_buf[h] = m_new
        al_buf[h] = jnp.exp2(m_prev - m_new)
        ps_buf[h] = jnp.sum(p, axis=0, keepdims=True)
        pv_buf[h] = _dot(values_t(kb, h), p.astype(BF16))

    def stage_commit(kb, h, masked):
        alpha = al_buf[h]
        m_ref[h] = mn_buf[h]
        l_ref[h] = alpha * l_ref[h] + ps_buf[h]
        acc_ref[h] = alpha * acc_ref[h] + pv_buf[h]

    return stage_scores, stage_probabilities, stage_commit


def _softmax_scratch(n_heads, tk, nq):
    stat = pltpu.VMEM((n_heads, 1, nq), F32)
    wide = pltpu.VMEM((n_heads, HEAD_DIM, nq), F32)
    return [stat, stat, wide,
            pltpu.VMEM((n_heads, tk, nq), F32), stat, stat, stat, wide]


def _init_softmax_stats(m_ref, l_ref, acc_ref):
    m_ref[...] = jnp.full(m_ref.shape, NEG_INF, F32)
    l_ref[...] = jnp.zeros(l_ref.shape, F32)
    acc_ref[...] = jnp.zeros(acc_ref.shape, F32)


def _group_specs(s, tq, width, q_col, v_row):
    nq = s // tq
    q_spec = pl.BlockSpec((tq, width), lambda b, i: (b * nq + i, q_col))
    k_spec = pl.BlockSpec((s, width), lambda b, i: (b, q_col + 1))
    vt_spec = pl.BlockSpec((nq, width, tq), lambda b, i: (b, v_row, 0))
    o_spec = pl.BlockSpec((tq, width), lambda b, i: (b * nq + i, 0))
    return q_spec, k_spec, vt_spec, o_spec


def _diff_attn_kernel(lq_ref, lk_ref, g_ref, q_ref, k_ref, vt_ref, o_ref, qq_ref, m_ref, l_ref, acc_ref,
                      *bufs, tq, lam_init, lag):
    i = pl.program_id(1)
    low = lax.broadcasted_iota(jnp.int32, (HEAD_DIM, tq), 0) < DIFF_QK_DIM

    def prepare(h):
        q_t = _head(q_ref, h).astype(F32).T
        qq_ref[h, :, 0:tq] = jnp.where(low, q_t, 0.0).astype(BF16)
        qq_ref[h, :, tq:2 * tq] = jnp.where(low, 0.0, q_t).astype(BF16)

    _init_softmax_stats(m_ref, l_ref, acc_ref)

    key = lax.broadcasted_iota(jnp.int32, (tq, 2 * tq), 0)
    qry = lax.broadcasted_iota(jnp.int32, (tq, 2 * tq), 1)
    keep = jnp.where(qry >= tq, qry - tq, qry) >= key

    def scores(kb, h):
        return _dot(_key_block(k_ref, kb, tq, h), qq_ref[h])

    stages = _softmax_stages(scores, lambda kb, h: _value_block_t(vt_ref, kb, h), keep,
                             m_ref, l_ref, acc_ref, *bufs)
    def finish(h):
        prod = lq_ref[...] * lk_ref[...]
        first = lax.broadcasted_iota(jnp.int32, prod.shape, 1) < DIFF_QK_DIM
        e1 = jnp.exp(jnp.sum(jnp.where(first, prod, 0.0), axis=1, keepdims=True))
        e2 = jnp.exp(jnp.sum(jnp.where(first, 0.0, prod), axis=1, keepdims=True))
        lam = (e1 - e2) + lam_init
        o_t = (acc_ref[h, :, 0:tq] / l_ref[h, :, 0:tq]
               - lam * (acc_ref[h, :, tq:2 * tq] / l_ref[h, :, tq:2 * tq]))
        o_ref[:, h * HEAD_DIM:(h + 1) * HEAD_DIM] = (_rms(o_t.T, g_ref[...]) * (1.0 - lam_init)).astype(BF16)

    _run_pipeline(i, N_DIFF, stages, lag, prepare, finish)


def diff_attention(qk, vt, lam_q, lam_k, g_sub, lam_init, *, batch, tq, lag):
    m = qk.shape[0]
    s = m // batch
    vec = pl.BlockSpec((1, LANES), lambda b, i: (0, 0))
    q_spec, k_spec, vt_spec, o_spec = _group_specs(s, tq, A_W, QA_COL, VA_ROW)
    return pl.pallas_call(
        functools.partial(_diff_attn_kernel, tq=tq, lam_init=lam_init, lag=lag),
        grid=(batch, s // tq),
        in_specs=[vec, vec, vec, q_spec, k_spec, vt_spec],
        out_specs=o_spec,
        out_shape=jax.ShapeDtypeStruct((m, A_W), BF16),
        scratch_shapes=[pltpu.VMEM((N_DIFF, HEAD_DIM, 2 * tq), BF16)] + _softmax_scratch(N_DIFF, tq, 2 * tq),
        compiler_params=_params("parallel", "arbitrary"),
        name="diff_attention",
    )(lam_q, lam_k, g_sub, qk, qk, vt)


def _sb_attn_kernel(q_ref, k_ref, vt_ref, o_ref, qn_ref, r_ref, acc_ref,
                    zn_buf, after_buf, cs_buf, pv_buf, *, tq, lag):
    i = pl.program_id(1)
    def prepare(h):
        qn_ref[h] = (-_head(q_ref, h).astype(F32)).T.astype(BF16)

    r_ref[...] = jnp.zeros(r_ref.shape, F32)
    acc_ref[...] = jnp.zeros(acc_ref.shape, F32)
    ss = lax.broadcasted_iota(jnp.int32, (tq, 2 * tq), 0)
    jj = lax.broadcasted_iota(jnp.int32, (tq, 2 * tq), 1)
    tri = (jnp.where(jj >= tq, jj - tq, jj) >= ss).astype(BF16)

    strict = lax.broadcasted_iota(jnp.int32, (tq, tq), 0) < lax.broadcasted_iota(jnp.int32, (tq, tq), 1)

    def scores(kb, h, masked):
        zn_buf[h] = _dot(_key_block(k_ref, kb, tq, h), qn_ref[h])

    def suffix_sums(kb, h, masked):
        zn = zn_buf[h]
        sp = jnp.log(1.0 + jnp.exp2(jnp.abs(zn) * -LOG2E))
        log_remain = jnp.minimum(zn, 0.0) - sp
        if masked:
            log_remain = jnp.where(strict, log_remain, 0.0)
        hi = log_remain.astype(BF16)
        lo = (log_remain - hi.astype(F32)).astype(BF16)
        after_buf[h] = _dot(tri, jnp.concatenate([hi, lo], axis=0))
        cs_buf[h] = jnp.sum(log_remain, axis=0, keepdims=True)

    def weights(kb, h, masked):
        w = jnp.exp((after_buf[h] + r_ref[h]) - zn_buf[h])
        if masked:
            w = jnp.where(strict, w, 0.0)
        pv_buf[h] = _dot(_value_block_t(vt_ref, kb, h), w.astype(BF16))
        r_ref[h] += cs_buf[h]

    def accumulate(kb, h, masked):
        acc_ref[h] += pv_buf[h]

    def finish(h):
        o_ref[:, h * HEAD_DIM:(h + 1) * HEAD_DIM] = acc_ref[h].T.astype(BF16)

    _run_pipeline(i, N_SB, (scores, suffix_sums, weights, accumulate), lag, prepare, finish)


def stick_breaking_attention(qk, vt, *, batch, tq, lag):
    m = qk.shape[0]
    s = m // batch
    q_spec, k_spec, vt_spec, o_spec = _group_specs(s, tq, SB_W, QB_COL, VB_ROW)
    stat = pltpu.VMEM((N_SB, 1, tq), F32)
    wide = pltpu.VMEM((N_SB, HEAD_DIM, tq), F32)
    tile = pltpu.VMEM((N_SB, tq, tq), F32)
    return pl.pallas_call(
        functools.partial(_sb_attn_kernel, tq=tq, lag=lag),
        grid=(batch, s // tq),
        in_specs=[q_spec, k_spec, vt_spec],
        out_specs=o_spec,
        out_shape=jax.ShapeDtypeStruct((m, SB_W), BF16),
        scratch_shapes=[pltpu.VMEM((N_SB, HEAD_DIM, tq), BF16), stat, wide,
                        tile, tile, stat, wide],
        compiler_params=_params("parallel", "arbitrary"),
        name="stick_breaking_attention",
    )(qk, qk, vt)


def _fox_attn_kernel(fq_ref, fk_ref, q_ref, k_ref, vt_ref, o_ref, qa_ref, m_ref, l_ref, acc_ref, *bufs, tq, lag):
    i = pl.program_id(1)
    feat = lax.broadcasted_iota(jnp.int32, (LANES, tq), 0)
    fq_t = fq_ref[...].astype(F32).T

    def prepare(h):
        lo = GATE_PARTS * h
        mine = (((feat >= lo) & (feat < lo + GATE_PARTS))
                | ((feat >= GATE_LANES + lo) & (feat < GATE_LANES + lo + GATE_PARTS)))
        qa_ref[h, 0:HEAD_DIM, :] = _head(q_ref, h).astype(F32).T.astype(BF16)
        qa_ref[h, HEAD_DIM:2 * HEAD_DIM, :] = jnp.where(mine, fq_t, 0.0).astype(BF16)

    _init_softmax_stats(m_ref, l_ref, acc_ref)

    keep = lax.broadcasted_iota(jnp.int32, (tq, tq), 1) >= lax.broadcasted_iota(jnp.int32, (tq, tq), 0)

    def scores(kb, h):
        fk = fk_ref[pl.ds(pl.multiple_of(kb * tq, tq), tq), :]
        return _dot(jnp.concatenate([_key_block(k_ref, kb, tq, h), fk], axis=1), qa_ref[h])

    stages = _softmax_stages(scores, lambda kb, h: _value_block_t(vt_ref, kb, h), keep,
                             m_ref, l_ref, acc_ref, *bufs)
    def finish(h):
        o_ref[:, h * HEAD_DIM:(h + 1) * HEAD_DIM] = (acc_ref[h] / l_ref[h]).T.astype(BF16)

    _run_pipeline(i, N_FOX, stages, lag, prepare, finish)


def forgetting_attention(qk, vt, fk, fq, *, batch, tq, lag):
    m = qk.shape[0]
    s = m // batch
    nq = s // tq
    q_spec, k_spec, vt_spec, o_spec = _group_specs(s, tq, FOX_W, QC_COL, VC_ROW)
    return pl.pallas_call(
        functools.partial(_fox_attn_kernel, tq=tq, lag=lag),
        grid=(batch, nq),
        in_specs=[pl.BlockSpec((tq, LANES), lambda b, i: (b * nq + i, 0)),
                  pl.BlockSpec((s, LANES), lambda b, i: (b, 0)),
                  q_spec, k_spec, vt_spec],
        out_specs=o_spec,
        out_shape=jax.ShapeDtypeStruct((m, FOX_W), BF16),
        scratch_shapes=[pltpu.VMEM((N_FOX, 2 * HEAD_DIM, tq), BF16)] + _softmax_scratch(N_FOX, tq, tq),
        compiler_params=_params("parallel", "arbitrary"),
        name="forgetting_attention",
    )(fq, fk, qk, qk, vt)


def _residual_tail(y, x, g_post, g_next, x_out_ref, h_out_ref):
    x_new = x + _rms(y, g_post)
    x_out_ref[...] = x_new
    if h_out_ref is not None:
        h_out_ref[...] = _rms(x_new, g_next).astype(BF16)


def _mix_out_kernel(oa_ref, ob_ref, oc_ref, x_ref, w_ref, gb_ref, gc_ref, gp_ref, gn_ref, xo_ref, ho_ref):
    ob = _rms(ob_ref[...].astype(F32), gb_ref[...]).astype(BF16)
    oc = _rms(oc_ref[...].astype(F32), gc_ref[...]).astype(BF16)
    y = (_dot(oa_ref[...], w_ref[0:A_W, :]) + _dot(ob, w_ref[A_W:A_W + SB_W, :])
         + _dot(oc, w_ref[A_W + SB_W:, :]))
    _residual_tail(y, x_ref[...], gp_ref[...], gn_ref[...], xo_ref, ho_ref)


def mix_out(oa, ob, oc, x, w, g_sb, g_fox, g_post, g_next, *, bm):
    m, d = x.shape
    rows = lambda width: pl.BlockSpec((bm, width), lambda i: (i, 0))
    const = lambda width: pl.BlockSpec((1, width), lambda i: (0, 0))
    return pl.pallas_call(
        _mix_out_kernel,
        grid=(m // bm,),
        in_specs=[rows(A_W), rows(SB_W), rows(FOX_W), rows(d),
                  pl.BlockSpec(w.shape, lambda i: (0, 0)),
                  const(SB_W), const(FOX_W), const(d), const(d)],
        out_specs=[rows(d), rows(d)],
        out_shape=[jax.ShapeDtypeStruct((m, d), F32), jax.ShapeDtypeStruct((m, d), BF16)],
        compiler_params=_params("parallel"),
        name="mix_out",
    )(oa, ob, oc, x, w, g_sb.reshape(1, -1), g_fox.reshape(1, -1), g_post.reshape(1, -1), g_next.reshape(1, -1))


def _norm_matmul_kernel(x_ref, g_ref, w_ref, o_ref):
    o_ref[...] = _dot(_rms(x_ref[...], g_ref[...]).astype(BF16), w_ref[...]).astype(BF16)


def norm_matmul(x, g, w, *, bm):
    m, d = x.shape
    n = w.shape[1]
    return pl.pallas_call(
        _norm_matmul_kernel,
        grid=(m // bm,),
        in_specs=[pl.BlockSpec((bm, d), lambda i: (i, 0)), pl.BlockSpec((1, d), lambda i: (0, 0)),
                  pl.BlockSpec((d, n), lambda i: (0, 0))],
        out_specs=pl.BlockSpec((bm, n), lambda i: (i, 0)),
        out_shape=jax.ShapeDtypeStruct((m, n), BF16),
        compiler_params=_params("parallel"),
        name="memory_kv",
    )(x, g.reshape(1, d), w)


def _cross_attn_kernel(h_ref, x_ref, kv_ref, wq_ref, wo_ref, gp_ref, gn_ref, xo_ref, ho_ref):
    q = (_dot(h_ref[...], wq_ref[...]) * HEAD_DIM ** -0.5).astype(BF16)
    heads = []
    for hd in range(N_CROSS_HEADS):
        k = kv_ref[:, hd * HEAD_DIM:(hd + 1) * HEAD_DIM]
        v = kv_ref[:, CROSS_W + hd * HEAD_DIM:CROSS_W + (hd + 1) * HEAD_DIM]
        s = _dot_nt(q[:, hd * HEAD_DIM:(hd + 1) * HEAD_DIM], k)
        p = jnp.exp(s - jnp.max(s, axis=1, keepdims=True))
        o = _dot(p.astype(BF16), v) / jnp.sum(p, axis=1, keepdims=True)
        heads.append(o.astype(BF16))
    y = _dot(jnp.concatenate(heads, axis=1), wo_ref[...])
    _residual_tail(y, x_ref[...], gp_ref[...], gn_ref[...], xo_ref, ho_ref)


def cross_attention(h, x, kv, wq, wo, g_post, g_next, *, batch, bm):
    m, d = x.shape
    s = m // batch
    mem_len = kv.shape[0] // batch
    nb = s // bm
    rows = lambda width: pl.BlockSpec((bm, width), lambda b, i: (b * nb + i, 0))
    const = lambda shape: pl.BlockSpec(shape, lambda b, i: (0, 0))
    return pl.pallas_call(
        _cross_attn_kernel,
        grid=(batch, nb),
        in_specs=[rows(d), rows(d),
                  pl.BlockSpec((mem_len, 2 * CROSS_W), lambda b, i: (b, 0)),
                  const(wq.shape), const(wo.shape), const((1, d)), const((1, d))],
        out_specs=[rows(d), rows(d)],
        out_shape=[jax.ShapeDtypeStruct((m, d), F32), jax.ShapeDtypeStruct((m, d), BF16)],
        compiler_params=_params("parallel", "parallel"),
        name="cross_attention",
    )(h, x, kv, wq, wo, g_post.reshape(1, d), g_next.reshape(1, d))


def _ffn_up_kernel(h_ref, wg_ref, wu_ref, o_ref):
    h = h_ref[...]
    gate = _dot(h, wg_ref[...])
    up = _dot(h, wu_ref[...])
    o_ref[...] = (gate * jax.nn.sigmoid(gate) * up).astype(BF16)


def ffn_up(h, wg, wu, *, bm, bn):
    m, d = h.shape
    n = wg.shape[1]
    wspec = pl.BlockSpec((d, bn), lambda i, j: (0, j))
    return pl.pallas_call(
        _ffn_up_kernel,
        grid=(m // bm, n // bn),
        in_specs=[pl.BlockSpec((bm, d), lambda i, j: (i, 0)), wspec, wspec],
        out_specs=pl.BlockSpec((bm, bn), lambda i, j: (i, j)),
        out_shape=jax.ShapeDtypeStruct((m, n), BF16),
        compiler_params=_params("parallel", "arbitrary"),
        name="ffn_up",
    )(h, wg, wu)


def _ffn_down_kernel(*refs, emit_h):
    if emit_h:
        a_ref, w_ref, x_ref, gp_ref, gn_ref, xo_ref, ho_ref = refs
    else:
        a_ref, w_ref, x_ref, gp_ref, xo_ref = refs
        gn_ref = ho_ref = None
    kk = pl.program_id(1)
    last = pl.num_programs(1) - 1
    part = _dot(a_ref[...], w_ref[...])

    @pl.when(kk == 0)
    def _():
        xo_ref[...] = part

    @pl.when((kk > 0) & (kk < last))
    def _():
        xo_ref[...] += part

    @pl.when(kk == last)
    def _():
        g_next = gn_ref[...] if emit_h else None
        _residual_tail(xo_ref[...] + part, x_ref[...], gp_ref[...], g_next, xo_ref, ho_ref)


def ffn_down(a, w, x, g_post, g_next, *, bm, bk):
    m, d = x.shape
    kdim = a.shape[1]
    emit_h = g_next is not None
    assert kdim // bk >= 2
    rows = pl.BlockSpec((bm, d), lambda i, k: (i, 0))
    const = pl.BlockSpec((1, d), lambda i, k: (0, 0))
    gains = [g_post.reshape(1, d)] + ([g_next.reshape(1, d)] if emit_h else [])
    out_shape = [jax.ShapeDtypeStruct((m, d), F32)] + ([jax.ShapeDtypeStruct((m, d), BF16)] if emit_h else [])
    return pl.pallas_call(
        functools.partial(_ffn_down_kernel, emit_h=emit_h),
        grid=(m // bm, kdim // bk),
        in_specs=[pl.BlockSpec((bm, bk), lambda i, k: (i, k)),
                  pl.BlockSpec((bk, d), lambda i, k: (k, 0)),
                  rows] + [const] * len(gains),
        out_specs=[rows] * len(out_shape),
        out_shape=out_shape,
        compiler_params=_params("parallel", "arbitrary"),
        name="ffn_down",
    )(a, w, x, *gains)


def _pick(n, *candidates):
    for c in candidates:
        if n % c == 0:
            return c
    return n


def _columns(w, names):
    return jnp.concatenate([w[..., _IN_OFFSETS[n][0]:_IN_OFFSETS[n][1]] for n in names], axis=-1)


def kernel(x, mem, positions, g_mix_pre, g_mix_post, w_in, b_f, lam_q1, lam_k1, lam_q2, lam_k2, g_diff_sub, g_sb_out, g_fox_out, w_out, g_x_pre, g_x_post, g_mem, w_cq, w_ckv, w_co, g_ffn_pre, g_ffn_post, w_gate, w_up, w_down):
    batch, seq, d = x.shape
    depth = w_in.shape[0]
    m = batch * seq
    d_ff = w_gate.shape[2]
    assert w_in.shape[2] == _IN_OFFSETS["fc"][1] and d == V_ROWS

    bm_row = _pick(m, 512, 256, 128)
    bm_mm = _pick(m, 1024, 512, 256, 128)
    bm_big = _pick(m, 2048, 1024, 512, 256, 128)
    lag_diff, lag_sb, lag_fox = 2, 2, 3
    bn_in = 2 * A_W
    bn_ff = _pick(d_ff, 512, 256, 128)
    bk_ff = _pick(d_ff, 2816, 1408, 1024, 512, 256, 128)
    tq = _pick(seq, 256, 128)
    bm_x = _pick(seq, 512, 256, 128)
    bm_mem = _pick(mem.shape[0] * mem.shape[1], 512, 256, 128)

    w_qk = _columns(w_in, QK_ORDER).astype(BF16)
    w_qk_rope = _columns(w_in, QK_ROPE_ORDER).astype(BF16)
    w_vt = jnp.swapaxes(_columns(w_in, V_ORDER), 1, 2).astype(BF16)
    gate_lane_head = jnp.arange(2 * GATE_LANES) % GATE_LANES // GATE_PARTS
    wf_rep = jnp.pad(_columns(w_in, ("fc",))[:, :, gate_lane_head],
                     ((0, 0), (0, 0), (0, LANES - 2 * GATE_LANES))).astype(BF16)
    bf_rep = jnp.pad(b_f[:, gate_lane_head], ((0, 0), (0, LANES - 2 * GATE_LANES))).reshape(depth, 1, LANES)
    w_out_b, w_cq_b, w_ckv_b, w_co_b = (w.astype(BF16) for w in (w_out, w_cq, w_ckv, w_co))
    w_gate_b, w_up_b, w_down_b = (w.astype(BF16) for w in (w_gate, w_up, w_down))
    lam_q = jnp.concatenate([lam_q1, lam_q2], axis=1).reshape(depth, 1, LANES)
    lam_k = jnp.concatenate([lam_k1, lam_k2], axis=1).reshape(depth, 1, LANES)

    def scales(*groups):
        return jnp.concatenate([jnp.full((w,), v, F32) for w, v in groups]).reshape(1, -1)

    colscale = scales((SB_W, HEAD_DIM ** -0.5), (SB_W, 1.0), (FOX_W, HEAD_DIM ** -0.5 * LOG2E), (FOX_W, 1.0))
    colscale_rope = scales((A_W, DIFF_QK_DIM ** -0.5 * LOG2E), (A_W, 1.0))

    x = x.reshape(m, d)
    mem2 = mem.reshape(-1, d)
    cos_t, sin_t = rope_tables(positions, bm=bm_row)
    h = prenorm(x, g_mix_pre[0], bm=bm_row)

    for l in range(depth):
        lam_init = 0.8 - 0.6 * math.exp(-0.3 * l)
        qk = in_proj(h, w_qk[l], colscale, None, bm=bm_big, bn=bn_in)
        qk_rope = in_proj(h, w_qk_rope[l], colscale_rope, (cos_t, sin_t), bm=bm_big, bn=bn_in)
        vt = in_proj_vt(h, w_vt[l], bm=bm_big, bn=bn_in, tk=tq)
        fk, fq = forget_features(h, wf_rep[l], bf_rep[l], batch=batch, blk=tq)
        oa = diff_attention(qk_rope, vt, lam_q[l], lam_k[l], g_diff_sub[l].reshape(1, LANES), lam_init, batch=batch, tq=tq,
                            lag=lag_diff)
        ob = stick_breaking_attention(qk, vt, batch=batch, tq=tq, lag=lag_sb)
        oc = forgetting_attention(qk, vt, fk, fq, batch=batch, tq=tq, lag=lag_fox)
        x, h = mix_out(oa, ob, oc, x, w_out_b[l], g_sb_out[l], g_fox_out[l], g_mix_post[l], g_x_pre[l], bm=bm_row)
        kv = norm_matmul(mem2, g_mem[l], w_ckv_b[l], bm=bm_mem)
        x, h = cross_attention(h, x, kv, w_cq_b[l], w_co_b[l], g_x_post[l], g_ffn_pre[l], batch=batch, bm=bm_x)
        a = ffn_up(h, w_gate_b[l], w_up_b[l], bm=bm_mm, bn=bn_ff)
        g_next = g_mix_pre[l + 1] if l + 1 < depth else None
        x, *rest = ffn_down(a, w_down_b[l], x, g_ffn_post[l], g_next, bm=bm_row, bk=bk_ff)
        h = rest[0] if rest else None
    return x.reshape(batch, seq, d)
```

```python
import functools
import math

import jax
import jax.numpy as jnp
from jax import lax
from jax.experimental import pallas as pl
from jax.experimental.pallas import tpu as pltpu

F32 = jnp.float32
BF16 = jnp.bfloat16

EPS = 1e-6
NEG_INF = -1e30
ROPE_THETA = 10000.0
LOG2E = math.log2(math.e)

LANES = 128
HEAD_DIM = 128
N_DIFF, N_SB, N_FOX = 4, 6, 6
DIFF_QK_DIM = HEAD_DIM // 2
A_W, SB_W, FOX_W = N_DIFF * HEAD_DIM, N_SB * HEAD_DIM, N_FOX * HEAD_DIM
V_ROWS = A_W + SB_W + FOX_W
N_CROSS_HEADS = 4
CROSS_W = N_CROSS_HEADS * HEAD_DIM
GATE_PARTS = 3
GATE_LANES = GATE_PARTS * N_FOX

VMEM_LIMIT = 56 * 1024 * 1024

_IN_OFFSETS = {}
_off = 0
for _name, _width in (("qa", A_W), ("ka", A_W), ("va", A_W), ("qb", SB_W), ("kb", SB_W), ("vb", SB_W),
                      ("qc", FOX_W), ("kc", FOX_W), ("vc", FOX_W), ("fc", N_FOX)):
    _IN_OFFSETS[_name] = (_off, _off + _width)
    _off += _width
QK_ORDER = ("qb", "kb", "qc", "kc")
QK_ROPE_ORDER = ("qa", "ka")
V_ORDER = ("vb", "vc", "va")
QB_COL, QC_COL, QA_COL = 0, 2, 0
VB_ROW, VC_ROW, VA_ROW = 0, 1, (2 * SB_W) // A_W


def _params(*semantics):
    return pltpu.CompilerParams(dimension_semantics=semantics, vmem_limit_bytes=VMEM_LIMIT)


def _rms(xf, g):
    return xf * lax.rsqrt(jnp.mean(xf * xf, axis=-1, keepdims=True) + EPS) * g


def _dot(a, b):
    return jnp.dot(a, b, preferred_element_type=F32)


def _dot_nt(a, b):
    return lax.dot_general(a, b, (((1,), (1,)), ((), ())), preferred_element_type=F32)


def _bf16_parts(x):
    p0 = x.astype(BF16)
    r1 = x - p0.astype(F32)
    p1 = r1.astype(BF16)
    p2 = (r1 - p1.astype(F32)).astype(BF16)
    return p0, p1, p2


def _prenorm_kernel(x_ref, g_ref, h_ref):
    h_ref[...] = _rms(x_ref[...], g_ref[...]).astype(BF16)


def prenorm(x, g, *, bm):
    m, d = x.shape
    return pl.pallas_call(
        _prenorm_kernel,
        grid=(m // bm,),
        in_specs=[pl.BlockSpec((bm, d), lambda i: (i, 0)), pl.BlockSpec((1, d), lambda i: (0, 0))],
        out_specs=pl.BlockSpec((bm, d), lambda i: (i, 0)),
        out_shape=jax.ShapeDtypeStruct((m, d), BF16),
        compiler_params=_params("parallel"),
        name="prenorm",
    )(x, g.reshape(1, d))


def _rope_table_kernel(pos_ref, invf_ref, sign_ref, cos_ref, sin_ref):
    ang = pos_ref[...].astype(F32) * invf_ref[...]
    cos_ref[...] = jnp.cos(ang)
    sin_ref[...] = jnp.sin(ang) * sign_ref[...]


def rope_tables(positions, *, bm):
    m = positions.size
    half = DIFF_QK_DIM // 2
    inv_freq = ROPE_THETA ** (-jnp.arange(half, dtype=F32) / half)
    invf = jnp.tile(inv_freq, LANES // half).reshape(1, LANES)
    sign = jnp.where((jnp.arange(LANES) % DIFF_QK_DIM) < half, -1.0, 1.0).astype(F32).reshape(1, LANES)
    row = pl.BlockSpec((bm, LANES), lambda i: (i, 0))
    const = pl.BlockSpec((1, LANES), lambda i: (0, 0))
    return pl.pallas_call(
        _rope_table_kernel,
        grid=(m // bm,),
        in_specs=[pl.BlockSpec((bm, 1), lambda i: (i, 0)), const, const],
        out_specs=[row, row],
        out_shape=[jax.ShapeDtypeStruct((m, LANES), F32)] * 2,
        compiler_params=_params("parallel"),
        name="rope_tables",
    )(positions.reshape(m, 1), invf, sign)


def _in_proj_kernel(h_ref, w_ref, cs_ref, o_ref):
    o_ref[...] = (_dot(h_ref[...], w_ref[...]) * cs_ref[...]).astype(BF16)


def _in_proj_rope_kernel(h_ref, w_ref, cs_ref, cos_ref, sin_ref, o_ref):
    acc = _dot(h_ref[...], w_ref[...]) * cs_ref[...]
    bm, bn = acc.shape
    c, s = cos_ref[...], sin_ref[...]
    first_half = (lax.broadcasted_iota(jnp.int32, (bm, LANES), 1) % DIFF_QK_DIM) < DIFF_QK_DIM // 2
    for t in range(bn // LANES):
        a = acc[:, t * LANES:(t + 1) * LANES]
        partner = jnp.where(first_half, pltpu.roll(a, LANES - DIFF_QK_DIM // 2, 1),
                            pltpu.roll(a, DIFF_QK_DIM // 2, 1))
        o_ref[:, t * LANES:(t + 1) * LANES] = (a * c + partner * s).astype(BF16)


def in_proj(h, w, colscale, rope_tables_or_none, *, bm, bn):
    m, d = h.shape
    n = w.shape[1]
    assert n % bn == 0 and m % bm == 0
    rope = rope_tables_or_none is not None
    row_tab = pl.BlockSpec((bm, LANES), lambda i, j: (i, 0))
    return pl.pallas_call(
        _in_proj_rope_kernel if rope else _in_proj_kernel,
        grid=(m // bm, n // bn),
        in_specs=[pl.BlockSpec((bm, d), lambda i, j: (i, 0)),
                  pl.BlockSpec((d, bn), lambda i, j: (0, j)),
                  pl.BlockSpec((1, bn), lambda i, j: (0, j))] + ([row_tab, row_tab] if rope else []),
        out_specs=pl.BlockSpec((bm, bn), lambda i, j: (i, j)),
        out_shape=jax.ShapeDtypeStruct((m, n), BF16),
        compiler_params=_params("parallel", "arbitrary"),
        name="in_proj_rope" if rope else "in_proj",
    )(h, w, colscale, *(rope_tables_or_none or ()))


def _in_proj_vt_kernel(h_ref, wt_ref, o_ref, *, tk):
    res = _dot_nt(wt_ref[...], h_ref[...]).astype(BF16)
    for c in range(o_ref.shape[0]):
        o_ref[c] = res[:, c * tk:(c + 1) * tk]


def in_proj_vt(h, wt, *, bm, bn, tk):
    m, d = h.shape
    n = wt.shape[0]
    return pl.pallas_call(
        functools.partial(_in_proj_vt_kernel, tk=tk),
        grid=(m // bm, n // bn),
        in_specs=[pl.BlockSpec((bm, d), lambda i, j: (i, 0)),
                  pl.BlockSpec((bn, d), lambda i, j: (j, 0))],
        out_specs=pl.BlockSpec((bm // tk, bn, tk), lambda i, j: (i, j, 0)),
        out_shape=jax.ShapeDtypeStruct((m // tk, n, tk), BF16),
        compiler_params=_params("parallel", "arbitrary"),
        name="in_proj_vt",
    )(h, wt)


def _forget_features_kernel(h_ref, wf_ref, bf_ref, part_ref, fk_ref, fq_ref, *, blk):
    s = h_ref.shape[0]
    lane = lax.broadcasted_iota(jnp.int32, (blk, LANES), 1)
    part = jnp.broadcast_to(part_ref[...], (blk, LANES))
    lower = (lax.broadcasted_iota(jnp.int32, (blk, blk), 0)
             >= lax.broadcasted_iota(jnp.int32, (blk, blk), 1)).astype(BF16)
    one = jnp.ones((blk, LANES), F32)
    zero = jnp.zeros((blk, LANES), F32)
    carry = jnp.zeros((1, LANES), F32)
    for t in range(s // blk):
        rows = slice(t * blk, (t + 1) * blk)
        fc = _dot(h_ref[rows, :], wf_ref[...]) + bf_ref[...]
        log_f = (jnp.minimum(fc, 0.0) - jnp.log1p(jnp.exp(-jnp.abs(fc)))) * LOG2E
        x0, x1, x2 = _bf16_parts(log_f)
        cum = _dot(lower, x0) + _dot(lower, x1) + _dot(lower, x2) + carry
        carry = cum[blk - 1:blk, :]
        c0, c1, c2 = (c.astype(F32) for c in _bf16_parts(cum))
        parts = jnp.where(part == 0, c0, jnp.where(part == 1, c1, c2))
        fk = jnp.where(lane < GATE_LANES, parts, jnp.where(lane < 2 * GATE_LANES, one, zero))
        fq = jnp.where(lane < GATE_LANES, -one, jnp.where(lane < 2 * GATE_LANES, parts, zero))
        fk_ref[rows, :] = fk.astype(BF16)
        fq_ref[rows, :] = fq.astype(BF16)


def forget_features(h, wf_rep, bf_rep, *, batch, blk):
    m, d = h.shape
    s = m // batch
    part = (jnp.arange(LANES, dtype=jnp.int32) % GATE_PARTS).reshape(1, LANES)
    rows = pl.BlockSpec((s, LANES), lambda b: (b, 0))
    vec = pl.BlockSpec((1, LANES), lambda b: (0, 0))
    return pl.pallas_call(
        functools.partial(_forget_features_kernel, blk=blk),
        grid=(batch,),
        in_specs=[pl.BlockSpec((s, d), lambda b: (b, 0)),
                  pl.BlockSpec((d, LANES), lambda b: (0, 0)), vec, vec],
        out_specs=[rows, rows],
        out_shape=[jax.ShapeDtypeStruct((m, LANES), BF16)] * 2,
        compiler_params=_params("parallel"),
        name="forget_features",
    )(h, wf_rep, bf_rep, part)


def _rows(blk, t):
    return pl.ds(pl.multiple_of(blk * t, t), t)


def _row_block(ref, blk, t, h):
    return ref[_rows(blk, t), h * HEAD_DIM:(h + 1) * HEAD_DIM]


def _over_query_blocks(block_fn, nq):
    def q_block(i, carry):
        block_fn(i)
        return carry

    lax.fori_loop(0, nq, q_block, 0)


def _value_block_t(ref, kb, h):
    return ref[kb, h * HEAD_DIM:(h + 1) * HEAD_DIM, :]


def _run_pipeline(i, n_heads, stages, lag, prepare, finish):
    n = len(stages)
    lead = [(n - 1 - k) * lag for k in range(n)]
    assert lead[0] <= n_heads
    unprepared = set(range(n_heads))

    def positions(cur, nxt, cur_masked, first, last, finishing=False):
        for t in range(first, last):
            for k in range(n):
                idx = t + lead[k]
                if 0 <= idx < n_heads:
                    if k == 0 and idx in unprepared:
                        unprepared.discard(idx)
                        prepare(idx)
                    stages[k](cur, idx, cur_masked)
                    if finishing and k == n - 1:
                        finish(idx)
                elif idx >= n_heads and nxt is not None:
                    stages[k](nxt, idx - n_heads, False)

    positions(i, None, True, -lead[0], 0)
    positions(i, jnp.maximum(i - 1, 0), True, 0, n_heads)

    def body(j, carry):
        cur = i - j
        positions(cur, cur - 1, False, 0, n_heads)
        return carry

    lax.fori_loop(1, i, body, 0)

    @pl.when(i > 0)
    def _():
        positions(0, None, False, 0, n_heads, finishing=True)

    @pl.when(i == 0)
    def _():
        for h in range(n_heads):
            finish(h)


def _softmax_stages(scores, values_t, keep, m_ref, l_ref, acc_ref, s_buf, mn_buf, al_buf, ps_buf, pv_buf):
    def stage_scores(kb, h, masked):
        s_buf[h] = scores(kb, h)

    def stage_probabilities(kb, h, masked):
        s = s_buf[h]
        if masked:
            s = jnp.where(keep, s, NEG_INF)
        m_prev = m_ref[h]
        m_new = jnp.maximum(m_prev, jnp.max(s, axis=0, keepdims=True))
        p = jnp.exp2(s - m_new)
        mn_buf[h] = m_new
        al_buf[h] = jnp.exp2(m_prev - m_new)
        ps_buf[h] = jnp.sum(p, axis=0, keepdims=True)
        pv_buf[h] = _dot(values_t(kb, h), p.astype(BF16))

    def stage_commit(kb, h, masked):
        alpha = al_buf[h]
        m_ref[h] = mn_buf[h]
        l_ref[h] = alpha * l_ref[h] + ps_buf[h]
        acc_ref[h] = alpha * acc_ref[h] + pv_buf[h]

    return stage_scores, stage_probabilities, stage_commit


def _softmax_scratch(n_heads, tk, nq):
    stat = pltpu.VMEM((n_heads, 1, nq), F32)
    wide = pltpu.VMEM((n_heads, HEAD_DIM, nq), F32)
    return [stat, stat, wide,
            pltpu.VMEM((n_heads, tk, nq), F32), stat, stat, stat, wide]


def _init_softmax_stats(m_ref, l_ref, acc_ref):
    m_ref[...] = jnp.full(m_ref.shape, NEG_INF, F32)
    l_ref[...] = jnp.zeros(l_ref.shape, F32)
    acc_ref[...] = jnp.zeros(acc_ref.shape, F32)


def _group_specs(s, tq, width, q_col, v_row):
    nq = s // tq
    q_spec = pl.BlockSpec((s, width), lambda b: (b, q_col))
    k_spec = pl.BlockSpec((s, width), lambda b: (b, q_col + 1))
    vt_spec = pl.BlockSpec((nq, width, tq), lambda b: (b, v_row, 0))
    o_spec = pl.BlockSpec((s, width), lambda b: (b, 0))
    return q_spec, k_spec, vt_spec, o_spec


def _diff_attn_kernel(*refs, nq, **static):
    _over_query_blocks(lambda i: _diff_attn_block(i, *refs, **static), nq)


def _diff_attn_block(i, lq_ref, lk_ref, g_ref, q_ref, k_ref, vt_ref, o_ref, qq_ref, m_ref, l_ref, acc_ref,
                     *bufs, tq, lam_init, lag):
    low = lax.broadcasted_iota(jnp.int32, (HEAD_DIM, tq), 0) < DIFF_QK_DIM

    def prepare(h):
        q_t = _row_block(q_ref, i, tq, h).astype(F32).T
        qq_ref[h, :, 0:tq] = jnp.where(low, q_t, 0.0).astype(BF16)
        qq_ref[h, :, tq:2 * tq] = jnp.where(low, 0.0, q_t).astype(BF16)

    _init_softmax_stats(m_ref, l_ref, acc_ref)

    key = lax.broadcasted_iota(jnp.int32, (tq, 2 * tq), 0)
    qry = lax.broadcasted_iota(jnp.int32, (tq, 2 * tq), 1)
    keep = jnp.where(qry >= tq, qry - tq, qry) >= key

    def scores(kb, h):
        return _dot(_row_block(k_ref, kb, tq, h), qq_ref[h])

    stages = _softmax_stages(scores, lambda kb, h: _value_block_t(vt_ref, kb, h), keep,
                             m_ref, l_ref, acc_ref, *bufs)
    def finish(h):
        prod = lq_ref[...] * lk_ref[...]
        first = lax.broadcasted_iota(jnp.int32, prod.shape, 1) < DIFF_QK_DIM
        e1 = jnp.exp(jnp.sum(jnp.where(first, prod, 0.0), axis=1, keepdims=True))
        e2 = jnp.exp(jnp.sum(jnp.where(first, 0.0, prod), axis=1, keepdims=True))
        lam = (e1 - e2) + lam_init
        o_t = (acc_ref[h, :, 0:tq] / l_ref[h, :, 0:tq]
               - lam * (acc_ref[h, :, tq:2 * tq] / l_ref[h, :, tq:2 * tq]))
        o_ref[_rows(i, tq), h * HEAD_DIM:(h + 1) * HEAD_DIM] = (_rms(o_t.T, g_ref[...])
                                                                * (1.0 - lam_init)).astype(BF16)

    _run_pipeline(i, N_DIFF, stages, lag, prepare, finish)


def diff_attention(qk, vt, lam_q, lam_k, g_sub, lam_init, *, batch, tq, lag):
    m = qk.shape[0]
    s = m // batch
    vec = pl.BlockSpec((1, LANES), lambda b: (0, 0))
    q_spec, k_spec, vt_spec, o_spec = _group_specs(s, tq, A_W, QA_COL, VA_ROW)
    return pl.pallas_call(
        functools.partial(_diff_attn_kernel, nq=s // tq, tq=tq, lam_init=lam_init, lag=lag),
        grid=(batch,),
        in_specs=[vec, vec, vec, q_spec, k_spec, vt_spec],
        out_specs=o_spec,
        out_shape=jax.ShapeDtypeStruct((m, A_W), BF16),
        scratch_shapes=[pltpu.VMEM((N_DIFF, HEAD_DIM, 2 * tq), BF16)] + _softmax_scratch(N_DIFF, tq, 2 * tq),
        compiler_params=_params("parallel"),
        name="diff_attention",
    )(lam_q, lam_k, g_sub, qk, qk, vt)


def _sb_attn_kernel(*refs, nq, **static):
    _over_query_blocks(lambda i: _sb_attn_block(i, *refs, **static), nq)


def _sb_attn_block(i, q_ref, k_ref, vt_ref, o_ref, qn_ref, r_ref, acc_ref,
                   zn_buf, after_buf, cs_buf, pv_buf, *, tq, lag):
    def prepare(h):
        qn_ref[h] = (-_row_block(q_ref, i, tq, h).astype(F32)).T.astype(BF16)

    r_ref[...] = jnp.zeros(r_ref.shape, F32)
    acc_ref[...] = jnp.zeros(acc_ref.shape, F32)
    ss = lax.broadcasted_iota(jnp.int32, (tq, 2 * tq), 0)
    jj = lax.broadcasted_iota(jnp.int32, (tq, 2 * tq), 1)
    tri = (jnp.where(jj >= tq, jj - tq, jj) >= ss).astype(BF16)

    strict = lax.broadcasted_iota(jnp.int32, (tq, tq), 0) < lax.broadcasted_iota(jnp.int32, (tq, tq), 1)

    def scores(kb, h, masked):
        zn_buf[h] = _dot(_row_block(k_ref, kb, tq, h), qn_ref[h])

    def suffix_sums(kb, h, masked):
        zn = zn_buf[h]
        sp = jnp.log(1.0 + jnp.exp2(jnp.abs(zn) * -LOG2E))
        log_remain = jnp.minimum(zn, 0.0) - sp
        if masked:
            log_remain = jnp.where(strict, log_remain, 0.0)
        hi = log_remain.astype(BF16)
        lo = (log_remain - hi.astype(F32)).astype(BF16)
        after_buf[h] = _dot(tri, jnp.concatenate([hi, lo], axis=0))
        cs_buf[h] = jnp.sum(log_remain, axis=0, keepdims=True)

    def weights(kb, h, masked):
        w = jnp.exp((after_buf[h] + r_ref[h]) - zn_buf[h])
        if masked:
            w = jnp.where(strict, w, 0.0)
        pv_buf[h] = _dot(_value_block_t(vt_ref, kb, h), w.astype(BF16))
        r_ref[h] += cs_buf[h]

    def accumulate(kb, h, masked):
        acc_ref[h] += pv_buf[h]

    def finish(h):
        o_ref[_rows(i, tq), h * HEAD_DIM:(h + 1) * HEAD_DIM] = acc_ref[h].T.astype(BF16)

    _run_pipeline(i, N_SB, (scores, suffix_sums, weights, accumulate), lag, prepare, finish)


def stick_breaking_attention(qk, vt, *, batch, tq, lag):
    m = qk.shape[0]
    s = m // batch
    q_spec, k_spec, vt_spec, o_spec = _group_specs(s, tq, SB_W, QB_COL, VB_ROW)
    stat = pltpu.VMEM((N_SB, 1, tq), F32)
    wide = pltpu.VMEM((N_SB, HEAD_DIM, tq), F32)
    tile = pltpu.VMEM((N_SB, tq, tq), F32)
    return pl.pallas_call(
        functools.partial(_sb_attn_kernel, nq=s // tq, tq=tq, lag=lag),
        grid=(batch,),
        in_specs=[q_spec, k_spec, vt_spec],
        out_specs=o_spec,
        out_shape=jax.ShapeDtypeStruct((m, SB_W), BF16),
        scratch_shapes=[pltpu.VMEM((N_SB, HEAD_DIM, tq), BF16), stat, wide,
                        tile, tile, stat, wide],
        compiler_params=_params("parallel"),
        name="stick_breaking_attention",
    )(qk, qk, vt)


def _fox_attn_kernel(*refs, nq, **static):
    _over_query_blocks(lambda i: _fox_attn_block(i, *refs, **static), nq)


def _fox_attn_block(i, fq_ref, fk_ref, q_ref, k_ref, vt_ref, o_ref, qa_ref, m_ref, l_ref, acc_ref, *bufs, tq, lag):
    feat = lax.broadcasted_iota(jnp.int32, (LANES, tq), 0)
    fq_t = fq_ref[_rows(i, tq), :].astype(F32).T

    def prepare(h):
        lo = GATE_PARTS * h
        mine = (((feat >= lo) & (feat < lo + GATE_PARTS))
                | ((feat >= GATE_LANES + lo) & (feat < GATE_LANES + lo + GATE_PARTS)))
        qa_ref[h, 0:HEAD_DIM, :] = _row_block(q_ref, i, tq, h).astype(F32).T.astype(BF16)
        qa_ref[h, HEAD_DIM:2 * HEAD_DIM, :] = jnp.where(mine, fq_t, 0.0).astype(BF16)

    _init_softmax_stats(m_ref, l_ref, acc_ref)

    keep = lax.broadcasted_iota(jnp.int32, (tq, tq), 1) >= lax.broadcasted_iota(jnp.int32, (tq, tq), 0)

    def scores(kb, h):
        fk = fk_ref[pl.ds(pl.multiple_of(kb * tq, tq), tq), :]
        return _dot(jnp.concatenate([_row_block(k_ref, kb, tq, h), fk], axis=1), qa_ref[h])

    stages = _softmax_stages(scores, lambda kb, h: _value_block_t(vt_ref, kb, h), keep,
                             m_ref, l_ref, acc_ref, *bufs)
    def finish(h):
        o_ref[_rows(i, tq), h * HEAD_DIM:(h + 1) * HEAD_DIM] = (acc_ref[h] / l_ref[h]).T.astype(BF16)

    _run_pipeline(i, N_FOX, stages, lag, prepare, finish)


def forgetting_attention(qk, vt, fk, fq, *, batch, tq, lag):
    m = qk.shape[0]
    s = m // batch
    nq = s // tq
    q_spec, k_spec, vt_spec, o_spec = _group_specs(s, tq, FOX_W, QC_COL, VC_ROW)
    return pl.pallas_call(
        functools.partial(_fox_attn_kernel, nq=nq, tq=tq, lag=lag),
        grid=(batch,),
        in_specs=[pl.BlockSpec((s, LANES), lambda b: (b, 0)),
                  pl.BlockSpec((s, LANES), lambda b: (b, 0)),
                  q_spec, k_spec, vt_spec],
        out_specs=o_spec,
        out_shape=jax.ShapeDtypeStruct((m, FOX_W), BF16),
        scratch_shapes=[pltpu.VMEM((N_FOX, 2 * HEAD_DIM, tq), BF16)] + _softmax_scratch(N_FOX, tq, tq),
        compiler_params=_params("parallel"),
        name="forgetting_attention",
    )(fq, fk, qk, qk, vt)


def _residual_tail(y, x, g_post, g_next, x_out_ref, h_out_ref):
    x_new = x + _rms(y, g_post)
    x_out_ref[...] = x_new
    if h_out_ref is not None:
        h_out_ref[...] = _rms(x_new, g_next).astype(BF16)


def _mix_out_kernel(oa_ref, ob_ref, oc_ref, x_ref, w_ref, gb_ref, gc_ref, gp_ref, gn_ref, xo_ref, ho_ref):
    ob = _rms(ob_ref[...].astype(F32), gb_ref[...]).astype(BF16)
    oc = _rms(oc_ref[...].astype(F32), gc_ref[...]).astype(BF16)
    y = (_dot(oa_ref[...], w_ref[0:A_W, :]) + _dot(ob, w_ref[A_W:A_W + SB_W, :])
         + _dot(oc, w_ref[A_W + SB_W:, :]))
    _residual_tail(y, x_ref[...], gp_ref[...], gn_ref[...], xo_ref, ho_ref)


def mix_out(oa, ob, oc, x, w, g_sb, g_fox, g_post, g_next, *, bm):
    m, d = x.shape
    rows = lambda width: pl.BlockSpec((bm, width), lambda i: (i, 0))
    const = lambda width: pl.BlockSpec((1, width), lambda i: (0, 0))
    return pl.pallas_call(
        _mix_out_kernel,
        grid=(m // bm,),
        in_specs=[rows(A_W), rows(SB_W), rows(FOX_W), rows(d),
                  pl.BlockSpec(w.shape, lambda i: (0, 0)),
                  const(SB_W), const(FOX_W), const(d), const(d)],
        out_specs=[rows(d), rows(d)],
        out_shape=[jax.ShapeDtypeStruct((m, d), F32), jax.ShapeDtypeStruct((m, d), BF16)],
        compiler_params=_params("parallel"),
        name="mix_out",
    )(oa, ob, oc, x, w, g_sb.reshape(1, -1), g_fox.reshape(1, -1), g_post.reshape(1, -1), g_next.reshape(1, -1))


def _norm_matmul_kernel(x_ref, g_ref, w_ref, o_ref):
    o_ref[...] = _dot(_rms(x_ref[...], g_ref[...]).astype(BF16), w_ref[...]).astype(BF16)


def norm_matmul(x, g, w, *, bm):
    m, d = x.shape
    n = w.shape[1]
    return pl.pallas_call(
        _norm_matmul_kernel,
        grid=(m // bm,),
        in_specs=[pl.BlockSpec((bm, d), lambda i: (i, 0)), pl.BlockSpec((1, d), lambda i: (0, 0)),
                  pl.BlockSpec((d, n), lambda i: (0, 0))],
        out_specs=pl.BlockSpec((bm, n), lambda i: (i, 0)),
        out_shape=jax.ShapeDtypeStruct((m, n), BF16),
        compiler_params=_params("parallel"),
        name="memory_kv",
    )(x, g.reshape(1, d), w)


def _cross_attn_kernel(h_ref, x_ref, kv_ref, wq_ref, wo_ref, gp_ref, gn_ref, xo_ref, ho_ref):
    q = (_dot(h_ref[...], wq_ref[...]) * HEAD_DIM ** -0.5).astype(BF16)
    heads = []
    for hd in range(N_CROSS_HEADS):
        k = kv_ref[:, hd * HEAD_DIM:(hd + 1) * HEAD_DIM]
        v = kv_ref[:, CROSS_W + hd * HEAD_DIM:CROSS_W + (hd + 1) * HEAD_DIM]
        s = _dot_nt(q[:, hd * HEAD_DIM:(hd + 1) * HEAD_DIM], k)
        p = jnp.exp(s - jnp.max(s, axis=1, keepdims=True))
        o = _dot(p.astype(BF16), v) / jnp.sum(p, axis=1, keepdims=True)
        heads.append(o.astype(BF16))
    y = _dot(jnp.concatenate(heads, axis=1), wo_ref[...])
    _residual_tail(y, x_ref[...], gp_ref[...], gn_ref[...], xo_ref, ho_ref)


def cross_attention(h, x, kv, wq, wo, g_post, g_next, *, batch, bm):
    m, d = x.shape
    s = m // batch
    mem_len = kv.shape[0] // batch
    nb = s // bm
    rows = lambda width: pl.BlockSpec((bm, width), lambda b, i: (b * nb + i, 0))
    const = lambda shape: pl.BlockSpec(shape, lambda b, i: (0, 0))
    return pl.pallas_call(
        _cross_attn_kernel,
        grid=(batch, nb),
        in_specs=[rows(d), rows(d),
                  pl.BlockSpec((mem_len, 2 * CROSS_W), lambda b, i: (b, 0)),
                  const(wq.shape), const(wo.shape), const((1, d)), const((1, d))],
        out_specs=[rows(d), rows(d)],
        out_shape=[jax.ShapeDtypeStruct((m, d), F32), jax.ShapeDtypeStruct((m, d), BF16)],
        compiler_params=_params("parallel", "parallel"),
        name="cross_attention",
    )(h, x, kv, wq, wo, g_post.reshape(1, d), g_next.reshape(1, d))


def _ffn_up_kernel(h_ref, wg_ref, wu_ref, o_ref):
    h = h_ref[...]
    gate = _dot(h, wg_ref[...])
    up = _dot(h, wu_ref[...])
    o_ref[...] = (gate * jax.nn.sigmoid(gate) * up).astype(BF16)


def ffn_up(h, wg, wu, *, bm, bn):
    m, d = h.shape
    n = wg.shape[1]
    wspec = pl.BlockSpec((d, bn), lambda i, j: (0, j))
    return pl.pallas_call(
        _ffn_up_kernel,
        grid=(m // bm, n // bn),
        in_specs=[pl.BlockSpec((bm, d), lambda i, j: (i, 0)), wspec, wspec],
        out_specs=pl.BlockSpec((bm, bn), lambda i, j: (i, j)),
        out_shape=jax.ShapeDtypeStruct((m, n), BF16),
        compiler_params=_params("parallel", "arbitrary"),
        name="ffn_up",
    )(h, wg, wu)


def _ffn_down_kernel(*refs, emit_h):
    if emit_h:
        a_ref, w_ref, x_ref, gp_ref, gn_ref, xo_ref, ho_ref = refs
    else:
        a_ref, w_ref, x_ref, gp_ref, xo_ref = refs
        gn_ref = ho_ref = None
    kk = pl.program_id(1)
    last = pl.num_programs(1) - 1
    part = _dot(a_ref[...], w_ref[...])

    @pl.when(kk == 0)
    def _():
        xo_ref[...] = part

    @pl.when((kk > 0) & (kk < last))
    def _():
        xo_ref[...] += part

    @pl.when(kk == last)
    def _():
        g_next = gn_ref[...] if emit_h else None
        _residual_tail(xo_ref[...] + part, x_ref[...], gp_ref[...], g_next, xo_ref, ho_ref)


def ffn_down(a, w, x, g_post, g_next, *, bm, bk):
    m, d = x.shape
    kdim = a.shape[1]
    emit_h = g_next is not None
    assert kdim // bk >= 2
    rows = pl.BlockSpec((bm, d), lambda i, k: (i, 0))
    const = pl.BlockSpec((1, d), lambda i, k: (0, 0))
    gains = [g_post.reshape(1, d)] + ([g_next.reshape(1, d)] if emit_h else [])
    out_shape = [jax.ShapeDtypeStruct((m, d), F32)] + ([jax.ShapeDtypeStruct((m, d), BF16)] if emit_h else [])
    return pl.pallas_call(
        functools.partial(_ffn_down_kernel, emit_h=emit_h),
        grid=(m // bm, kdim // bk),
        in_specs=[pl.BlockSpec((bm, bk), lambda i, k: (i, k)),
                  pl.BlockSpec((bk, d), lambda i, k: (k, 0)),
                  rows] + [const] * len(gains),
        out_specs=[rows] * len(out_shape),
        out_shape=out_shape,
        compiler_params=_params("parallel", "arbitrary"),
        name="ffn_down",
    )(a, w, x, *gains)


def _pick(n, *candidates):
    for c in candidates:
        if n % c == 0:
            return c
    return n


def _columns(w, names):
    return jnp.concatenate([w[..., _IN_OFFSETS[n][0]:_IN_OFFSETS[n][1]] for n in names], axis=-1)


def kernel(x, mem, positions, g_mix_pre, g_mix_post, w_in, b_f, lam_q1, lam_k1, lam_q2, lam_k2, g_diff_sub, g_sb_out, g_fox_out, w_out, g_x_pre, g_x_post, g_mem, w_cq, w_ckv, w_co, g_ffn_pre, g_ffn_post, w_gate, w_up, w_down):
    batch, seq, d = x.shape
    depth = w_in.shape[0]
    m = batch * seq
    d_ff = w_gate.shape[2]
    assert w_in.shape[2] == _IN_OFFSETS["fc"][1] and d == V_ROWS

    bm_row = _pick(m, 512, 256, 128)
    bm_mm = _pick(m, 1024, 512, 256, 128)
    bm_big = _pick(m, 2048, 1024, 512, 256, 128)
    lag_diff, lag_sb, lag_fox = 2, 2, 3
    bn_in = 2 * A_W
    bn_ff = _pick(d_ff, 512, 256, 128)
    bk_ff = _pick(d_ff, 2816, 1408, 1024, 512, 256, 128)
    tq = _pick(seq, 256, 128)
    bm_x = _pick(seq, 512, 256, 128)
    bm_mem = _pick(mem.shape[0] * mem.shape[1], 512, 256, 128)

    w_qk = _columns(w_in, QK_ORDER).astype(BF16)
    w_qk_rope = _columns(w_in, QK_ROPE_ORDER).astype(BF16)
    w_vt = jnp.swapaxes(_columns(w_in, V_ORDER), 1, 2).astype(BF16)
    gate_lane_head = jnp.arange(2 * GATE_LANES) % GATE_LANES // GATE_PARTS
    wf_rep = jnp.pad(_columns(w_in, ("fc",))[:, :, gate_lane_head],
                     ((0, 0), (0, 0), (0, LANES - 2 * GATE_LANES))).astype(BF16)
    bf_rep = jnp.pad(b_f[:, gate_lane_head], ((0, 0), (0, LANES - 2 * GATE_LANES))).reshape(depth, 1, LANES)
    w_out_b, w_cq_b, w_ckv_b, w_co_b = (w.astype(BF16) for w in (w_out, w_cq, w_ckv, w_co))
    w_gate_b, w_up_b, w_down_b = (w.astype(BF16) for w in (w_gate, w_up, w_down))
    lam_q = jnp.concatenate([lam_q1, lam_q2], axis=1).reshape(depth, 1, LANES)
    lam_k = jnp.concatenate([lam_k1, lam_k2], axis=1).reshape(depth, 1, LANES)

    def scales(*groups):
        return jnp.concatenate([jnp.full((w,), v, F32) for w, v in groups]).reshape(1, -1)

    colscale = scales((SB_W, HEAD_DIM ** -0.5), (SB_W, 1.0), (FOX_W, HEAD_DIM ** -0.5 * LOG2E), (FOX_W, 1.0))
    colscale_rope = scales((A_W, DIFF_QK_DIM ** -0.5 * LOG2E), (A_W, 1.0))

    x = x.reshape(m, d)
    mem2 = mem.reshape(-1, d)
    cos_t, sin_t = rope_tables(positions, bm=bm_row)
    h = prenorm(x, g_mix_pre[0], bm=bm_row)

    for l in range(depth):
        lam_init = 0.8 - 0.6 * math.exp(-0.3 * l)
        qk = in_proj(h, w_qk[l], colscale, None, bm=bm_big, bn=bn_in)
        qk_rope = in_proj(h, w_qk_rope[l], colscale_rope, (cos_t, sin_t), bm=bm_big, bn=bn_in)
        vt = in_proj_vt(h, w_vt[l], bm=bm_big, bn=bn_in, tk=tq)
        fk, fq = forget_features(h, wf_rep[l], bf_rep[l], batch=batch, blk=tq)
        oa = diff_attention(qk_rope, vt, lam_q[l], lam_k[l], g_diff_sub[l].reshape(1, LANES), lam_init, batch=batch, tq=tq,
                            lag=lag_diff)
        ob = stick_breaking_attention(qk, vt, batch=batch, tq=tq, lag=lag_sb)
        oc = forgetting_attention(qk, vt, fk, fq, batch=batch, tq=tq, lag=lag_fox)
        x, h = mix_out(oa, ob, oc, x, w_out_b[l], g_sb_out[l], g_fox_out[l], g_mix_post[l], g_x_pre[l], bm=bm_row)
        kv = norm_matmul(mem2, g_mem[l], w_ckv_b[l], bm=bm_mem)
        x, h = cross_attention(h, x, kv, w_cq_b[l], w_co_b[l], g_x_post[l], g_ffn_pre[l], batch=batch, bm=bm_x)
        a = ffn_up(h, w_gate_b[l], w_up_b[l], bm=bm_mm, bn=bn_ff)
        g_next = g_mix_pre[l + 1] if l + 1 < depth else None
        x, *rest = ffn_down(a, w_down_b[l], x, g_ffn_post[l], g_next, bm=bm_row, bk=bk_ff)
        h = rest[0] if rest else None
    return x.reshape(batch, seq, d)
```

```python
import functools
import math

import jax
import jax.numpy as jnp
from jax import lax
from jax.experimental import pallas as pl
from jax.experimental.pallas import tpu as pltpu

F32 = jnp.float32
BF16 = jnp.bfloat16

EPS = 1e-6
NEG_INF = -1e30
ROPE_THETA = 10000.0
LOG2E = math.log2(math.e)

LANES = 128
HEAD_DIM = 128
N_DIFF, N_SB, N_FOX = 4, 6, 6
DIFF_QK_DIM = HEAD_DIM // 2
A_W, SB_W, FOX_W = N_DIFF * HEAD_DIM, N_SB * HEAD_DIM, N_FOX * HEAD_DIM
V_ROWS = A_W + SB_W + FOX_W
N_CROSS_HEADS = 4
CROSS_W = N_CROSS_HEADS * HEAD_DIM
GATE_PARTS = 3
GATE_LANES = GATE_PARTS * N_FOX

VMEM_LIMIT = 56 * 1024 * 1024

_IN_OFFSETS = {}
_off = 0
for _name, _width in (("qa", A_W), ("ka", A_W), ("va", A_W), ("qb", SB_W), ("kb", SB_W), ("vb", SB_W),
                      ("qc", FOX_W), ("kc", FOX_W), ("vc", FOX_W), ("fc", N_FOX)):
    _IN_OFFSETS[_name] = (_off, _off + _width)
    _off += _width
QK_ORDER = ("qb", "kb", "qc", "kc")
QK_ROPE_ORDER = ("qa", "ka")
V_ORDER = ("vb", "vc", "va")
QB_COL, QC_COL, QA_COL = 0, 2, 0
VB_ROW, VC_ROW, VA_ROW = 0, 1, (2 * SB_W) // A_W


def _params(*semantics):
    return pltpu.CompilerParams(dimension_semantics=semantics, vmem_limit_bytes=VMEM_LIMIT)


def _rms(xf, g):
    return xf * lax.rsqrt(jnp.mean(xf * xf, axis=-1, keepdims=True) + EPS) * g


def _dot(a, b):
    return jnp.dot(a, b, preferred_element_type=F32)


def _dot_nt(a, b):
    return lax.dot_general(a, b, (((1,), (1,)), ((), ())), preferred_element_type=F32)


def _bf16_parts(x):
    p0 = x.astype(BF16)
    r1 = x - p0.astype(F32)
    p1 = r1.astype(BF16)
    p2 = (r1 - p1.astype(F32)).astype(BF16)
    return p0, p1, p2


def _prenorm_kernel(x_ref, g_ref, h_ref):
    h_ref[...] = _rms(x_ref[...], g_ref[...]).astype(BF16)


def prenorm(x, g, *, bm):
    m, d = x.shape
    return pl.pallas_call(
        _prenorm_kernel,
        grid=(m // bm,),
        in_specs=[pl.BlockSpec((bm, d), lambda i: (i, 0)), pl.BlockSpec((1, d), lambda i: (0, 0))],
        out_specs=pl.BlockSpec((bm, d), lambda i: (i, 0)),
        out_shape=jax.ShapeDtypeStruct((m, d), BF16),
        compiler_params=_params("parallel"),
        name="prenorm",
    )(x, g.reshape(1, d))


def _rope_table_kernel(pos_ref, invf_ref, sign_ref, cos_ref, sin_ref):
    ang = pos_ref[...].astype(F32) * invf_ref[...]
    cos_ref[...] = jnp.cos(ang)
    sin_ref[...] = jnp.sin(ang) * sign_ref[...]


def rope_tables(positions, *, bm):
    m = positions.size
    half = DIFF_QK_DIM // 2
    inv_freq = ROPE_THETA ** (-jnp.arange(half, dtype=F32) / half)
    invf = jnp.tile(inv_freq, LANES // half).reshape(1, LANES)
    sign = jnp.where((jnp.arange(LANES) % DIFF_QK_DIM) < half, -1.0, 1.0).astype(F32).reshape(1, LANES)
    row = pl.BlockSpec((bm, LANES), lambda i: (i, 0))
    const = pl.BlockSpec((1, LANES), lambda i: (0, 0))
    return pl.pallas_call(
        _rope_table_kernel,
        grid=(m // bm,),
        in_specs=[pl.BlockSpec((bm, 1), lambda i: (i, 0)), const, const],
        out_specs=[row, row],
        out_shape=[jax.ShapeDtypeStruct((m, LANES), F32)] * 2,
        compiler_params=_params("parallel"),
        name="rope_tables",
    )(positions.reshape(m, 1), invf, sign)


def _in_proj_kernel(h_ref, w_ref, cs_ref, o_ref):
    o_ref[...] = (_dot(h_ref[...], w_ref[...]) * cs_ref[...]).astype(BF16)


def _in_proj_rope_kernel(h_ref, w_ref, cs_ref, cos_ref, sin_ref, o_ref):
    acc = _dot(h_ref[...], w_ref[...]) * cs_ref[...]
    bm, bn = acc.shape
    c, s = cos_ref[...], sin_ref[...]
    first_half = (lax.broadcasted_iota(jnp.int32, (bm, LANES), 1) % DIFF_QK_DIM) < DIFF_QK_DIM // 2
    for t in range(bn // LANES):
        a = acc[:, t * LANES:(t + 1) * LANES]
        partner = jnp.where(first_half, pltpu.roll(a, LANES - DIFF_QK_DIM // 2, 1),
                            pltpu.roll(a, DIFF_QK_DIM // 2, 1))
        o_ref[:, t * LANES:(t + 1) * LANES] = (a * c + partner * s).astype(BF16)


def in_proj(h, w, colscale, rope_tables_or_none, *, bm, bn):
    m, d = h.shape
    n = w.shape[1]
    assert n % bn == 0 and m % bm == 0
    rope = rope_tables_or_none is not None
    row_tab = pl.BlockSpec((bm, LANES), lambda i, j: (i, 0))
    return pl.pallas_call(
        _in_proj_rope_kernel if rope else _in_proj_kernel,
        grid=(m // bm, n // bn),
        in_specs=[pl.BlockSpec((bm, d), lambda i, j: (i, 0)),
                  pl.BlockSpec((d, bn), lambda i, j: (0, j)),
                  pl.BlockSpec((1, bn), lambda i, j: (0, j))] + ([row_tab, row_tab] if rope else []),
        out_specs=pl.BlockSpec((bm, bn), lambda i, j: (i, j)),
        out_shape=jax.ShapeDtypeStruct((m, n), BF16),
        compiler_params=_params("parallel", "arbitrary"),
        name="in_proj_rope" if rope else "in_proj",
    )(h, w, colscale, *(rope_tables_or_none or ()))


def _in_proj_vt_kernel(h_ref, wt_ref, o_ref, *, tk):
    res = _dot_nt(wt_ref[...], h_ref[...]).astype(BF16)
    for c in range(o_ref.shape[0]):
        o_ref[c] = res[:, c * tk:(c + 1) * tk]


def in_proj_vt(h, wt, *, bm, bn, tk):
    m, d = h.shape
    n = wt.shape[0]
    return pl.pallas_call(
        functools.partial(_in_proj_vt_kernel, tk=tk),
        grid=(m // bm, n // bn),
        in_specs=[pl.BlockSpec((bm, d), lambda i, j: (i, 0)),
                  pl.BlockSpec((bn, d), lambda i, j: (j, 0))],
        out_specs=pl.BlockSpec((bm // tk, bn, tk), lambda i, j: (i, j, 0)),
        out_shape=jax.ShapeDtypeStruct((m // tk, n, tk), BF16),
        compiler_params=_params("parallel", "arbitrary"),
        name="in_proj_vt",
    )(h, wt)


def _forget_features_kernel(h_ref, wf_ref, bf_ref, part_ref, fk_ref, fq_ref, *, blk):
    s = h_ref.shape[0]
    lane = lax.broadcasted_iota(jnp.int32, (blk, LANES), 1)
    part = jnp.broadcast_to(part_ref[...], (blk, LANES))
    lower = (lax.broadcasted_iota(jnp.int32, (blk, blk), 0)
             >= lax.broadcasted_iota(jnp.int32, (blk, blk), 1)).astype(BF16)
    one = jnp.ones((blk, LANES), F32)
    zero = jnp.zeros((blk, LANES), F32)
    carry = jnp.zeros((1, LANES), F32)
    for t in range(s // blk):
        rows = slice(t * blk, (t + 1) * blk)
        fc = _dot(h_ref[rows, :], wf_ref[...]) + bf_ref[...]
        log_f = (jnp.minimum(fc, 0.0) - jnp.log1p(jnp.exp(-jnp.abs(fc)))) * LOG2E
        x0, x1, x2 = _bf16_parts(log_f)
        cum = _dot(lower, x0) + _dot(lower, x1) + _dot(lower, x2) + carry
        carry = cum[blk - 1:blk, :]
        c0, c1, c2 = (c.astype(F32) for c in _bf16_parts(cum))
        parts = jnp.where(part == 0, c0, jnp.where(part == 1, c1, c2))
        fk = jnp.where(lane < GATE_LANES, parts, jnp.where(lane < 2 * GATE_LANES, one, zero))
        fq = jnp.where(lane < GATE_LANES, -one, jnp.where(lane < 2 * GATE_LANES, parts, zero))
        fk_ref[rows, :] = fk.astype(BF16)
        fq_ref[rows, :] = fq.astype(BF16)


def forget_features(h, wf_rep, bf_rep, *, batch, blk):
    m, d = h.shape
    s = m // batch
    part = (jnp.arange(LANES, dtype=jnp.int32) % GATE_PARTS).reshape(1, LANES)
    rows = pl.BlockSpec((s, LANES), lambda b: (b, 0))
    vec = pl.BlockSpec((1, LANES), lambda b: (0, 0))
    return pl.pallas_call(
        functools.partial(_forget_features_kernel, blk=blk),
        grid=(batch,),
        in_specs=[pl.BlockSpec((s, d), lambda b: (b, 0)),
                  pl.BlockSpec((d, LANES), lambda b: (0, 0)), vec, vec],
        out_specs=[rows, rows],
        out_shape=[jax.ShapeDtypeStruct((m, LANES), BF16)] * 2,
        compiler_params=_params("parallel"),
        name="forget_features",
    )(h, wf_rep, bf_rep, part)


def _rows(blk, t):
    return pl.ds(pl.multiple_of(blk * t, t), t)


def _row_block(ref, blk, t, h):
    return ref[_rows(blk, t), h * HEAD_DIM:(h + 1) * HEAD_DIM]


def _over_query_blocks(block_fn, nq):
    def q_block(i, carry):
        block_fn(i)
        return carry

    lax.fori_loop(0, nq, q_block, 0)


def _value_block_t(ref, kb, h):
    return ref[kb, h * HEAD_DIM:(h + 1) * HEAD_DIM, :]


def _run_pipeline(i, nq, n_heads, stages, lag, reset, prepare, finish):
    n = len(stages)
    lead = [(n - 1 - k) * lag for k in range(n)]
    assert lead[0] == n_heads
    i_next = jnp.minimum(i + 1, nq - 1)

    def positions(first, last, cur, cur_masked, nxt=None, nxt_masked=None, new_query=False):
        for t in range(first, last):
            for k in range(n):
                idx = t + lead[k]
                if 0 <= idx < n_heads:
                    if t < 0 and k == 0:
                        prepare(i, idx)
                    stages[k](cur, idx, cur_masked)
                    if new_query and k == n - 1:
                        finish(i, idx)
                        reset(idx)
                elif idx >= n_heads and nxt is not None:
                    if new_query and k == 0:
                        prepare(i_next, idx - n_heads)
                    stages[k](nxt, idx - n_heads, nxt_masked)

    @pl.when(i == 0)
    def _():
        for h in range(n_heads):
            reset(h)
        positions(-n_heads, 0, 0, True)
        positions(0, n_heads, 0, True, i_next, True, new_query=True)

    @pl.when(i > 0)
    def _():
        positions(0, n_heads, i, True, i - 1, False)

        def body(j, carry):
            cur = i - j
            positions(0, n_heads, cur, False, cur - 1, False)
            return carry

        lax.fori_loop(1, i, body, 0)
        positions(0, n_heads, 0, False, i_next, True, new_query=True)


def _softmax_stages(scores, values_t, keep, m_ref, l_ref, acc_ref, s_buf, mn_buf, al_buf, ps_buf, pv_buf):
    def stage_scores(kb, h, masked):
        s_buf[h] = scores(kb, h)

    def stage_probabilities(kb, h, masked):
        s = s_buf[h]
        if masked:
            s = jnp.where(keep, s, NEG_INF)
        m_prev = m_ref[h]
        m_new = jnp.maximum(m_prev, jnp.max(s, axis=0, keepdims=True))
        p = jnp.exp2(s - m_new)
        mn_buf[h] = m_new
        al_buf[h] = jnp.exp2(m_prev - m_new)
        ps_buf[h] = jnp.sum(p, axis=0, keepdims=True)
        pv_buf[h] = _dot(values_t(kb, h), p.astype(BF16))

    def stage_commit(kb, h, masked):
        alpha = al_buf[h]
        m_ref[h] = mn_buf[h]
        l_ref[h] = alpha * l_ref[h] + ps_buf[h]
        acc_ref[h] = alpha * acc_ref[h] + pv_buf[h]

    return stage_scores, stage_probabilities, stage_commit


def _softmax_scratch(n_heads, tk, nq):
    stat = pltpu.VMEM((n_heads, 1, nq), F32)
    wide = pltpu.VMEM((n_heads, HEAD_DIM, nq), F32)
    return [stat, stat, wide,
            pltpu.VMEM((n_heads, tk, nq), F32), stat, stat, stat, wide]


def _reset_softmax_stats(m_ref, l_ref, acc_ref, h):
    m_ref[h] = jnp.full(m_ref.shape[1:], NEG_INF, F32)
    l_ref[h] = jnp.zeros(l_ref.shape[1:], F32)
    acc_ref[h] = jnp.zeros(acc_ref.shape[1:], F32)


def _group_specs(s, tq, width, q_col, v_row):
    nq = s // tq
    q_spec = pl.BlockSpec((s, width), lambda b: (b, q_col))
    k_spec = pl.BlockSpec((s, width), lambda b: (b, q_col + 1))
    vt_spec = pl.BlockSpec((nq, width, tq), lambda b: (b, v_row, 0))
    o_spec = pl.BlockSpec((s, width), lambda b: (b, 0))
    return q_spec, k_spec, vt_spec, o_spec


def _diff_attn_kernel(*refs, nq, **static):
    _over_query_blocks(lambda i: _diff_attn_block(i, *refs, nq=nq, **static), nq)


def _diff_attn_block(i, lq_ref, lk_ref, g_ref, q_ref, k_ref, vt_ref, o_ref, qq_ref, m_ref, l_ref, acc_ref,
                     *bufs, nq, tq, lam_init, lag):
    low = lax.broadcasted_iota(jnp.int32, (HEAD_DIM, tq), 0) < DIFF_QK_DIM

    def prepare(qi, h):
        q_t = _row_block(q_ref, qi, tq, h).astype(F32).T
        qq_ref[h, :, 0:tq] = jnp.where(low, q_t, 0.0).astype(BF16)
        qq_ref[h, :, tq:2 * tq] = jnp.where(low, 0.0, q_t).astype(BF16)

    reset = functools.partial(_reset_softmax_stats, m_ref, l_ref, acc_ref)

    key = lax.broadcasted_iota(jnp.int32, (tq, 2 * tq), 0)
    qry = lax.broadcasted_iota(jnp.int32, (tq, 2 * tq), 1)
    keep = jnp.where(qry >= tq, qry - tq, qry) >= key

    def scores(kb, h):
        return _dot(_row_block(k_ref, kb, tq, h), qq_ref[h])

    stages = _softmax_stages(scores, lambda kb, h: _value_block_t(vt_ref, kb, h), keep,
                             m_ref, l_ref, acc_ref, *bufs)
    def finish(qi, h):
        prod = lq_ref[...] * lk_ref[...]
        first = lax.broadcasted_iota(jnp.int32, prod.shape, 1) < DIFF_QK_DIM
        e1 = jnp.exp(jnp.sum(jnp.where(first, prod, 0.0), axis=1, keepdims=True))
        e2 = jnp.exp(jnp.sum(jnp.where(first, 0.0, prod), axis=1, keepdims=True))
        lam = (e1 - e2) + lam_init
        o_t = (acc_ref[h, :, 0:tq] / l_ref[h, :, 0:tq]
               - lam * (acc_ref[h, :, tq:2 * tq] / l_ref[h, :, tq:2 * tq]))
        o_ref[_rows(qi, tq), h * HEAD_DIM:(h + 1) * HEAD_DIM] = (_rms(o_t.T, g_ref[...])
                                                                 * (1.0 - lam_init)).astype(BF16)

    _run_pipeline(i, nq, N_DIFF, stages, lag, reset, prepare, finish)


def diff_attention(qk, vt, lam_q, lam_k, g_sub, lam_init, *, batch, tq, lag):
    m = qk.shape[0]
    s = m // batch
    vec = pl.BlockSpec((1, LANES), lambda b: (0, 0))
    q_spec, k_spec, vt_spec, o_spec = _group_specs(s, tq, A_W, QA_COL, VA_ROW)
    return pl.pallas_call(
        functools.partial(_diff_attn_kernel, nq=s // tq, tq=tq, lam_init=lam_init, lag=lag),
        grid=(batch,),
        in_specs=[vec, vec, vec, q_spec, k_spec, vt_spec],
        out_specs=o_spec,
        out_shape=jax.ShapeDtypeStruct((m, A_W), BF16),
        scratch_shapes=[pltpu.VMEM((N_DIFF, HEAD_DIM, 2 * tq), BF16)] + _softmax_scratch(N_DIFF, tq, 2 * tq),
        compiler_params=_params("parallel"),
        name="diff_attention",
    )(lam_q, lam_k, g_sub, qk, qk, vt)


def _sb_attn_kernel(*refs, nq, **static):
    _over_query_blocks(lambda i: _sb_attn_block(i, *refs, nq=nq, **static), nq)


def _sb_attn_block(i, q_ref, k_ref, vt_ref, o_ref, qn_ref, r_ref, acc_ref,
                   zn_buf, after_buf, cs_buf, pv_buf, *, nq, tq, lag):
    def prepare(qi, h):
        qn_ref[h] = (-_row_block(q_ref, qi, tq, h).astype(F32)).T.astype(BF16)

    def reset(h):
        r_ref[h] = jnp.zeros(r_ref.shape[1:], F32)
        acc_ref[h] = jnp.zeros(acc_ref.shape[1:], F32)

    ss = lax.broadcasted_iota(jnp.int32, (tq, 2 * tq), 0)
    jj = lax.broadcasted_iota(jnp.int32, (tq, 2 * tq), 1)
    tri = (jnp.where(jj >= tq, jj - tq, jj) >= ss).astype(BF16)

    strict = lax.broadcasted_iota(jnp.int32, (tq, tq), 0) < lax.broadcasted_iota(jnp.int32, (tq, tq), 1)

    def scores(kb, h, masked):
        zn_buf[h] = _dot(_row_block(k_ref, kb, tq, h), qn_ref[h])

    def suffix_sums(kb, h, masked):
        zn = zn_buf[h]
        sp = jnp.log(1.0 + jnp.exp2(jnp.abs(zn) * -LOG2E))
        log_remain = jnp.minimum(zn, 0.0) - sp
        if masked:
            log_remain = jnp.where(strict, log_remain, 0.0)
        hi = log_remain.astype(BF16)
        lo = (log_remain - hi.astype(F32)).astype(BF16)
        after_buf[h] = _dot(tri, jnp.concatenate([hi, lo], axis=0))
        cs_buf[h] = jnp.sum(log_remain, axis=0, keepdims=True)

    def weights(kb, h, masked):
        w = jnp.exp((after_buf[h] + r_ref[h]) - zn_buf[h])
        if masked:
            w = jnp.where(strict, w, 0.0)
        pv_buf[h] = _dot(_value_block_t(vt_ref, kb, h), w.astype(BF16))
        r_ref[h] += cs_buf[h]

    def accumulate(kb, h, masked):
        acc_ref[h] += pv_buf[h]

    def finish(qi, h):
        o_ref[_rows(qi, tq), h * HEAD_DIM:(h + 1) * HEAD_DIM] = acc_ref[h].T.astype(BF16)

    _run_pipeline(i, nq, N_SB, (scores, suffix_sums, weights, accumulate), lag, reset, prepare, finish)


def stick_breaking_attention(qk, vt, *, batch, tq, lag):
    m = qk.shape[0]
    s = m // batch
    q_spec, k_spec, vt_spec, o_spec = _group_specs(s, tq, SB_W, QB_COL, VB_ROW)
    stat = pltpu.VMEM((N_SB, 1, tq), F32)
    wide = pltpu.VMEM((N_SB, HEAD_DIM, tq), F32)
    tile = pltpu.VMEM((N_SB, tq, tq), F32)
    return pl.pallas_call(
        functools.partial(_sb_attn_kernel, nq=s // tq, tq=tq, lag=lag),
        grid=(batch,),
        in_specs=[q_spec, k_spec, vt_spec],
        out_specs=o_spec,
        out_shape=jax.ShapeDtypeStruct((m, SB_W), BF16),
        scratch_shapes=[pltpu.VMEM((N_SB, HEAD_DIM, tq), BF16), stat, wide,
                        tile, tile, stat, wide],
        compiler_params=_params("parallel"),
        name="stick_breaking_attention",
    )(qk, qk, vt)


def _fox_attn_kernel(*refs, nq, **static):
    _over_query_blocks(lambda i: _fox_attn_block(i, *refs, nq=nq, **static), nq)


def _fox_attn_block(i, fq_ref, fk_ref, q_ref, k_ref, vt_ref, o_ref, qa_ref, m_ref, l_ref, acc_ref, *bufs,
                    nq, tq, lag):
    feat = lax.broadcasted_iota(jnp.int32, (LANES, tq), 0)

    def prepare(qi, h):
        fq_t = fq_ref[_rows(qi, tq), :].astype(F32).T
        lo = GATE_PARTS * h
        mine = (((feat >= lo) & (feat < lo + GATE_PARTS))
                | ((feat >= GATE_LANES + lo) & (feat < GATE_LANES + lo + GATE_PARTS)))
        qa_ref[h, 0:HEAD_DIM, :] = _row_block(q_ref, qi, tq, h).astype(F32).T.astype(BF16)
        qa_ref[h, HEAD_DIM:2 * HEAD_DIM, :] = jnp.where(mine, fq_t, 0.0).astype(BF16)

    reset = functools.partial(_reset_softmax_stats, m_ref, l_ref, acc_ref)

    keep = lax.broadcasted_iota(jnp.int32, (tq, tq), 1) >= lax.broadcasted_iota(jnp.int32, (tq, tq), 0)

    def scores(kb, h):
        fk = fk_ref[pl.ds(pl.multiple_of(kb * tq, tq), tq), :]
        return _dot(jnp.concatenate([_row_block(k_ref, kb, tq, h), fk], axis=1), qa_ref[h])

    stages = _softmax_stages(scores, lambda kb, h: _value_block_t(vt_ref, kb, h), keep,
                             m_ref, l_ref, acc_ref, *bufs)
    def finish(qi, h):
        o_ref[_rows(qi, tq), h * HEAD_DIM:(h + 1) * HEAD_DIM] = (acc_ref[h] / l_ref[h]).T.astype(BF16)

    _run_pipeline(i, nq, N_FOX, stages, lag, reset, prepare, finish)


def forgetting_attention(qk, vt, fk, fq, *, batch, tq, lag):
    m = qk.shape[0]
    s = m // batch
    nq = s // tq
    q_spec, k_spec, vt_spec, o_spec = _group_specs(s, tq, FOX_W, QC_COL, VC_ROW)
    return pl.pallas_call(
        functools.partial(_fox_attn_kernel, nq=nq, tq=tq, lag=lag),
        grid=(batch,),
        in_specs=[pl.BlockSpec((s, LANES), lambda b: (b, 0)),
                  pl.BlockSpec((s, LANES), lambda b: (b, 0)),
                  q_spec, k_spec, vt_spec],
        out_specs=o_spec,
        out_shape=jax.ShapeDtypeStruct((m, FOX_W), BF16),
        scratch_shapes=[pltpu.VMEM((N_FOX, 2 * HEAD_DIM, tq), BF16)] + _softmax_scratch(N_FOX, tq, tq),
        compiler_params=_params("parallel"),
        name="forgetting_attention",
    )(fq, fk, qk, qk, vt)


def _residual_tail(y, x, g_post, g_next, x_out_ref, h_out_ref):
    x_new = x + _rms(y, g_post)
    x_out_ref[...] = x_new
    if h_out_ref is not None:
        h_out_ref[...] = _rms(x_new, g_next).astype(BF16)


def _mix_out_kernel(oa_ref, ob_ref, oc_ref, x_ref, w_ref, gb_ref, gc_ref, gp_ref, gn_ref, xo_ref, ho_ref):
    ob = _rms(ob_ref[...].astype(F32), gb_ref[...]).astype(BF16)
    oc = _rms(oc_ref[...].astype(F32), gc_ref[...]).astype(BF16)
    y = (_dot(oa_ref[...], w_ref[0:A_W, :]) + _dot(ob, w_ref[A_W:A_W + SB_W, :])
         + _dot(oc, w_ref[A_W + SB_W:, :]))
    _residual_tail(y, x_ref[...], gp_ref[...], gn_ref[...], xo_ref, ho_ref)


def mix_out(oa, ob, oc, x, w, g_sb, g_fox, g_post, g_next, *, bm):
    m, d = x.shape
    rows = lambda width: pl.BlockSpec((bm, width), lambda i: (i, 0))
    const = lambda width: pl.BlockSpec((1, width), lambda i: (0, 0))
    return pl.pallas_call(
        _mix_out_kernel,
        grid=(m // bm,),
        in_specs=[rows(A_W), rows(SB_W), rows(FOX_W), rows(d),
                  pl.BlockSpec(w.shape, lambda i: (0, 0)),
                  const(SB_W), const(FOX_W), const(d), const(d)],
        out_specs=[rows(d), rows(d)],
        out_shape=[jax.ShapeDtypeStruct((m, d), F32), jax.ShapeDtypeStruct((m, d), BF16)],
        compiler_params=_params("parallel"),
        name="mix_out",
    )(oa, ob, oc, x, w, g_sb.reshape(1, -1), g_fox.reshape(1, -1), g_post.reshape(1, -1), g_next.reshape(1, -1))


def _norm_matmul_kernel(x_ref, g_ref, w_ref, o_ref):
    o_ref[...] = _dot(_rms(x_ref[...], g_ref[...]).astype(BF16), w_ref[...]).astype(BF16)


def norm_matmul(x, g, w, *, bm):
    m, d = x.shape
    n = w.shape[1]
    return pl.pallas_call(
        _norm_matmul_kernel,
        grid=(m // bm,),
        in_specs=[pl.BlockSpec((bm, d), lambda i: (i, 0)), pl.BlockSpec((1, d), lambda i: (0, 0)),
                  pl.BlockSpec((d, n), lambda i: (0, 0))],
        out_specs=pl.BlockSpec((bm, n), lambda i: (i, 0)),
        out_shape=jax.ShapeDtypeStruct((m, n), BF16),
        compiler_params=_params("parallel"),
        name="memory_kv",
    )(x, g.reshape(1, d), w)


def _cross_attn_kernel(h_ref, x_ref, kv_ref, wq_ref, wo_ref, gp_ref, gn_ref, xo_ref, ho_ref):
    q = (_dot(h_ref[...], wq_ref[...]) * HEAD_DIM ** -0.5).astype(BF16)
    heads = []
    for hd in range(N_CROSS_HEADS):
        k = kv_ref[:, hd * HEAD_DIM:(hd + 1) * HEAD_DIM]
        v = kv_ref[:, CROSS_W + hd * HEAD_DIM:CROSS_W + (hd + 1) * HEAD_DIM]
        s = _dot_nt(q[:, hd * HEAD_DIM:(hd + 1) * HEAD_DIM], k)
        p = jnp.exp(s - jnp.max(s, axis=1, keepdims=True))
        o = _dot(p.astype(BF16), v) / jnp.sum(p, axis=1, keepdims=True)
        heads.append(o.astype(BF16))
    y = _dot(jnp.concatenate(heads, axis=1), wo_ref[...])
    _residual_tail(y, x_ref[...], gp_ref[...], gn_ref[...], xo_ref, ho_ref)


def cross_attention(h, x, kv, wq, wo, g_post, g_next, *, batch, bm):
    m, d = x.shape
    s = m // batch
    mem_len = kv.shape[0] // batch
    nb = s // bm
    rows = lambda width: pl.BlockSpec((bm, width), lambda b, i: (b * nb + i, 0))
    const = lambda shape: pl.BlockSpec(shape, lambda b, i: (0, 0))
    return pl.pallas_call(
        _cross_attn_kernel,
        grid=(batch, nb),
        in_specs=[rows(d), rows(d),
                  pl.BlockSpec((mem_len, 2 * CROSS_W), lambda b, i: (b, 0)),
                  const(wq.shape), const(wo.shape), const((1, d)), const((1, d))],
        out_specs=[rows(d), rows(d)],
        out_shape=[jax.ShapeDtypeStruct((m, d), F32), jax.ShapeDtypeStruct((m, d), BF16)],
        compiler_params=_params("parallel", "parallel"),
        name="cross_attention",
    )(h, x, kv, wq, wo, g_post.reshape(1, d), g_next.reshape(1, d))


def _ffn_up_kernel(h_ref, wg_ref, wu_ref, o_ref):
    h = h_ref[...]
    gate = _dot(h, wg_ref[...])
    up = _dot(h, wu_ref[...])
    o_ref[...] = (gate * jax.nn.sigmoid(gate) * up).astype(BF16)


def ffn_up(h, wg, wu, *, bm, bn):
    m, d = h.shape
    n = wg.shape[1]
    wspec = pl.BlockSpec((d, bn), lambda i, j: (0, j))
    return pl.pallas_call(
        _ffn_up_kernel,
        grid=(m // bm, n // bn),
        in_specs=[pl.BlockSpec((bm, d), lambda i, j: (i, 0)), wspec, wspec],
        out_specs=pl.BlockSpec((bm, bn), lambda i, j: (i, j)),
        out_shape=jax.ShapeDtypeStruct((m, n), BF16),
        compiler_params=_params("parallel", "arbitrary"),
        name="ffn_up",
    )(h, wg, wu)


def _ffn_down_kernel(*refs, emit_h):
    if emit_h:
        a_ref, w_ref, x_ref, gp_ref, gn_ref, xo_ref, ho_ref = refs
    else:
        a_ref, w_ref, x_ref, gp_ref, xo_ref = refs
        gn_ref = ho_ref = None
    kk = pl.program_id(1)
    last = pl.num_programs(1) - 1
    part = _dot(a_ref[...], w_ref[...])

    @pl.when(kk == 0)
    def _():
        xo_ref[...] = part

    @pl.when((kk > 0) & (kk < last))
    def _():
        xo_ref[...] += part

    @pl.when(kk == last)
    def _():
        g_next = gn_ref[...] if emit_h else None
        _residual_tail(xo_ref[...] + part, x_ref[...], gp_ref[...], g_next, xo_ref, ho_ref)


def ffn_down(a, w, x, g_post, g_next, *, bm, bk):
    m, d = x.shape
    kdim = a.shape[1]
    emit_h = g_next is not None
    assert kdim // bk >= 2
    rows = pl.BlockSpec((bm, d), lambda i, k: (i, 0))
    const = pl.BlockSpec((1, d), lambda i, k: (0, 0))
    gains = [g_post.reshape(1, d)] + ([g_next.reshape(1, d)] if emit_h else [])
    out_shape = [jax.ShapeDtypeStruct((m, d), F32)] + ([jax.ShapeDtypeStruct((m, d), BF16)] if emit_h else [])
    return pl.pallas_call(
        functools.partial(_ffn_down_kernel, emit_h=emit_h),
        grid=(m // bm, kdim // bk),
        in_specs=[pl.BlockSpec((bm, bk), lambda i, k: (i, k)),
                  pl.BlockSpec((bk, d), lambda i, k: (k, 0)),
                  rows] + [const] * len(gains),
        out_specs=[rows] * len(out_shape),
        out_shape=out_shape,
        compiler_params=_params("parallel", "arbitrary"),
        name="ffn_down",
    )(a, w, x, *gains)


def _pick(n, *candidates):
    for c in candidates:
        if n % c == 0:
            return c
    return n


def _columns(w, names):
    return jnp.concatenate([w[..., _IN_OFFSETS[n][0]:_IN_OFFSETS[n][1]] for n in names], axis=-1)


def kernel(x, mem, positions, g_mix_pre, g_mix_post, w_in, b_f, lam_q1, lam_k1, lam_q2, lam_k2, g_diff_sub, g_sb_out, g_fox_out, w_out, g_x_pre, g_x_post, g_mem, w_cq, w_ckv, w_co, g_ffn_pre, g_ffn_post, w_gate, w_up, w_down):
    batch, seq, d = x.shape
    depth = w_in.shape[0]
    m = batch * seq
    d_ff = w_gate.shape[2]
    assert w_in.shape[2] == _IN_OFFSETS["fc"][1] and d == V_ROWS

    bm_row = _pick(m, 512, 256, 128)
    bm_mm = _pick(m, 1024, 512, 256, 128)
    bm_big = _pick(m, 2048, 1024, 512, 256, 128)
    lag_diff, lag_sb, lag_fox = 2, 2, 3
    bn_in = 2 * A_W
    bn_ff = _pick(d_ff, 512, 256, 128)
    bk_ff = _pick(d_ff, 2816, 1408, 1024, 512, 256, 128)
    tq = _pick(seq, 256, 128)
    bm_x = _pick(seq, 512, 256, 128)
    bm_mem = _pick(mem.shape[0] * mem.shape[1], 512, 256, 128)

    w_qk = _columns(w_in, QK_ORDER).astype(BF16)
    w_qk_rope = _columns(w_in, QK_ROPE_ORDER).astype(BF16)
    w_vt = jnp.swapaxes(_columns(w_in, V_ORDER), 1, 2).astype(BF16)
    gate_lane_head = jnp.arange(2 * GATE_LANES) % GATE_LANES // GATE_PARTS
    wf_rep = jnp.pad(_columns(w_in, ("fc",))[:, :, gate_lane_head],
                     ((0, 0), (0, 0), (0, LANES - 2 * GATE_LANES))).astype(BF16)
    bf_rep = jnp.pad(b_f[:, gate_lane_head], ((0, 0), (0, LANES - 2 * GATE_LANES))).reshape(depth, 1, LANES)
    w_out_b, w_cq_b, w_ckv_b, w_co_b = (w.astype(BF16) for w in (w_out, w_cq, w_ckv, w_co))
    w_gate_b, w_up_b, w_down_b = (w.astype(BF16) for w in (w_gate, w_up, w_down))
    lam_q = jnp.concatenate([lam_q1, lam_q2], axis=1).reshape(depth, 1, LANES)
    lam_k = jnp.concatenate([lam_k1, lam_k2], axis=1).reshape(depth, 1, LANES)

    def scales(*groups):
        return jnp.concatenate([jnp.full((w,), v, F32) for w, v in groups]).reshape(1, -1)

    colscale = scales((SB_W, HEAD_DIM ** -0.5), (SB_W, 1.0), (FOX_W, HEAD_DIM ** -0.5 * LOG2E), (FOX_W, 1.0))
    colscale_rope = scales((A_W, DIFF_QK_DIM ** -0.5 * LOG2E), (A_W, 1.0))

    x = x.reshape(m, d)
    mem2 = mem.reshape(-1, d)
    cos_t, sin_t = rope_tables(positions, bm=bm_row)
    h = prenorm(x, g_mix_pre[0], bm=bm_row)

    for l in range(depth):
        lam_init = 0.8 - 0.6 * math.exp(-0.3 * l)
        qk = in_proj(h, w_qk[l], colscale, None, bm=bm_big, bn=bn_in)
        qk_rope = in_proj(h, w_qk_rope[l], colscale_rope, (cos_t, sin_t), bm=bm_big, bn=bn_in)
        vt = in_proj_vt(h, w_vt[l], bm=bm_big, bn=bn_in, tk=tq)
        fk, fq = forget_features(h, wf_rep[l], bf_rep[l], batch=batch, blk=tq)
        oa = diff_attention(qk_rope, vt, lam_q[l], lam_k[l], g_diff_sub[l].reshape(1, LANES), lam_init, batch=batch, tq=tq,
                            lag=lag_diff)
        ob = stick_breaking_attention(qk, vt, batch=batch, tq=tq, lag=lag_sb)
        oc = forgetting_attention(qk, vt, fk, fq, batch=batch, tq=tq, lag=lag_fox)
        x, h = mix_out(oa, ob, oc, x, w_out_b[l], g_sb_out[l], g_fox_out[l], g_mix_post[l], g_x_pre[l], bm=bm_row)
        kv = norm_matmul(mem2, g_mem[l], w_ckv_b[l], bm=bm_mem)
        x, h = cross_attention(h, x, kv, w_cq_b[l], w_co_b[l], g_x_post[l], g_ffn_pre[l], batch=batch, bm=bm_x)
        a = ffn_up(h, w_gate_b[l], w_up_b[l], bm=bm_mm, bn=bn_ff)
        g_next = g_mix_pre[l + 1] if l + 1 < depth else None
        x, *rest = ffn_down(a, w_down_b[l], x, g_ffn_post[l], g_next, bm=bm_row, bk=bk_ff)
        h = rest[0] if rest else None
    return x.reshape(batch, seq, d)
```

```python
import functools
import math

import jax
import jax.numpy as jnp
from jax import lax
from jax.experimental import pallas as pl
from jax.experimental.pallas import tpu as pltpu

F32 = jnp.float32
BF16 = jnp.bfloat16

EPS = 1e-6
NEG_INF = -1e30
ROPE_THETA = 10000.0
LOG2E = math.log2(math.e)

LANES = 128
HEAD_DIM = 128
N_DIFF, N_SB, N_FOX = 4, 6, 6
DIFF_QK_DIM = HEAD_DIM // 2
A_W, SB_W, FOX_W = N_DIFF * HEAD_DIM, N_SB * HEAD_DIM, N_FOX * HEAD_DIM
V_ROWS = A_W + SB_W + FOX_W
N_CROSS_HEADS = 4
CROSS_W = N_CROSS_HEADS * HEAD_DIM
GATE_PARTS = 3
GATE_LANES = GATE_PARTS * N_FOX

VMEM_LIMIT = 56 * 1024 * 1024

_IN_OFFSETS = {}
_off = 0
for _name, _width in (("qa", A_W), ("ka", A_W), ("va", A_W), ("qb", SB_W), ("kb", SB_W), ("vb", SB_W),
                      ("qc", FOX_W), ("kc", FOX_W), ("vc", FOX_W), ("fc", N_FOX)):
    _IN_OFFSETS[_name] = (_off, _off + _width)
    _off += _width
QK_ORDER = ("qb", "kb", "qc", "kc")
QK_ROPE_ORDER = ("qa", "ka")
V_ORDER = ("vb", "vc", "va")
QB_COL, QC_COL, QA_COL = 0, 2, 0
VB_ROW, VC_ROW, VA_ROW = 0, 1, (2 * SB_W) // A_W


def _params(*semantics):
    return pltpu.CompilerParams(dimension_semantics=semantics, vmem_limit_bytes=VMEM_LIMIT)


def _rms(xf, g):
    return xf * lax.rsqrt(jnp.mean(xf * xf, axis=-1, keepdims=True) + EPS) * g


def _dot(a, b):
    return jnp.dot(a, b, preferred_element_type=F32)


def _dot_nt(a, b):
    return lax.dot_general(a, b, (((1,), (1,)), ((), ())), preferred_element_type=F32)


def _bf16_parts(x):
    p0 = x.astype(BF16)
    r1 = x - p0.astype(F32)
    p1 = r1.astype(BF16)
    p2 = (r1 - p1.astype(F32)).astype(BF16)
    return p0, p1, p2


def _prenorm_kernel(x_ref, g_ref, h_ref):
    h_ref[...] = _rms(x_ref[...], g_ref[...]).astype(BF16)


def prenorm(x, g, *, bm):
    m, d = x.shape
    return pl.pallas_call(
        _prenorm_kernel,
        grid=(m // bm,),
        in_specs=[pl.BlockSpec((bm, d), lambda i: (i, 0)), pl.BlockSpec((1, d), lambda i: (0, 0))],
        out_specs=pl.BlockSpec((bm, d), lambda i: (i, 0)),
        out_shape=jax.ShapeDtypeStruct((m, d), BF16),
        compiler_params=_params("parallel"),
        name="prenorm",
    )(x, g.reshape(1, d))


def _rope_table_kernel(pos_ref, invf_ref, sign_ref, cos_ref, sin_ref):
    ang = pos_ref[...].astype(F32) * invf_ref[...]
    cos_ref[...] = jnp.cos(ang)
    sin_ref[...] = jnp.sin(ang) * sign_ref[...]


def rope_tables(positions, *, bm):
    m = positions.size
    half = DIFF_QK_DIM // 2
    inv_freq = ROPE_THETA ** (-jnp.arange(half, dtype=F32) / half)
    invf = jnp.tile(inv_freq, LANES // half).reshape(1, LANES)
    sign = jnp.where((jnp.arange(LANES) % DIFF_QK_DIM) < half, -1.0, 1.0).astype(F32).reshape(1, LANES)
    row = pl.BlockSpec((bm, LANES), lambda i: (i, 0))
    const = pl.BlockSpec((1, LANES), lambda i: (0, 0))
    return pl.pallas_call(
        _rope_table_kernel,
        grid=(m // bm,),
        in_specs=[pl.BlockSpec((bm, 1), lambda i: (i, 0)), const, const],
        out_specs=[row, row],
        out_shape=[jax.ShapeDtypeStruct((m, LANES), F32)] * 2,
        compiler_params=_params("parallel"),
        name="rope_tables",
    )(positions.reshape(m, 1), invf, sign)


def _in_proj_kernel(h_ref, w_ref, cs_ref, o_ref):
    o_ref[...] = (_dot(h_ref[...], w_ref[...]) * cs_ref[...]).astype(BF16)


def _in_proj_rope_kernel(h_ref, w_ref, cs_ref, cos_ref, sin_ref, o_ref):
    acc = _dot(h_ref[...], w_ref[...]) * cs_ref[...]
    bm, bn = acc.shape
    c, s = cos_ref[...], sin_ref[...]
    first_half = (lax.broadcasted_iota(jnp.int32, (bm, LANES), 1) % DIFF_QK_DIM) < DIFF_QK_DIM // 2
    for t in range(bn // LANES):
        a = acc[:, t * LANES:(t + 1) * LANES]
        partner = jnp.where(first_half, pltpu.roll(a, LANES - DIFF_QK_DIM // 2, 1),
                            pltpu.roll(a, DIFF_QK_DIM // 2, 1))
        o_ref[:, t * LANES:(t + 1) * LANES] = (a * c + partner * s).astype(BF16)


def in_proj(h, w, colscale, rope_tables_or_none, *, bm, bn):
    m, d = h.shape
    n = w.shape[1]
    assert n % bn == 0 and m % bm == 0
    rope = rope_tables_or_none is not None
    row_tab = pl.BlockSpec((bm, LANES), lambda i, j: (i, 0))
    return pl.pallas_call(
        _in_proj_rope_kernel if rope else _in_proj_kernel,
        grid=(m // bm, n // bn),
        in_specs=[pl.BlockSpec((bm, d), lambda i, j: (i, 0)),
                  pl.BlockSpec((d, bn), lambda i, j: (0, j)),
                  pl.BlockSpec((1, bn), lambda i, j: (0, j))] + ([row_tab, row_tab] if rope else []),
        out_specs=pl.BlockSpec((bm, bn), lambda i, j: (i, j)),
        out_shape=jax.ShapeDtypeStruct((m, n), BF16),
        compiler_params=_params("parallel", "arbitrary"),
        name="in_proj_rope" if rope else "in_proj",
    )(h, w, colscale, *(rope_tables_or_none or ()))


def _in_proj_vt_kernel(h_ref, wt_ref, o_ref, *, tk):
    res = _dot_nt(wt_ref[...], h_ref[...]).astype(BF16)
    for c in range(o_ref.shape[0]):
        o_ref[c] = res[:, c * tk:(c + 1) * tk]


def in_proj_vt(h, wt, *, bm, bn, tk):
    m, d = h.shape
    n = wt.shape[0]
    return pl.pallas_call(
        functools.partial(_in_proj_vt_kernel, tk=tk),
        grid=(m // bm, n // bn),
        in_specs=[pl.BlockSpec((bm, d), lambda i, j: (i, 0)),
                  pl.BlockSpec((bn, d), lambda i, j: (j, 0))],
        out_specs=pl.BlockSpec((bm // tk, bn, tk), lambda i, j: (i, j, 0)),
        out_shape=jax.ShapeDtypeStruct((m // tk, n, tk), BF16),
        compiler_params=_params("parallel", "arbitrary"),
        name="in_proj_vt",
    )(h, wt)


def _forget_features_kernel(h_ref, wf_ref, bf_ref, part_ref, fk_ref, fq_ref, *, blk):
    s = h_ref.shape[0]
    lane = lax.broadcasted_iota(jnp.int32, (blk, LANES), 1)
    part = jnp.broadcast_to(part_ref[...], (blk, LANES))
    lower = (lax.broadcasted_iota(jnp.int32, (blk, blk), 0)
             >= lax.broadcasted_iota(jnp.int32, (blk, blk), 1)).astype(BF16)
    one = jnp.ones((blk, LANES), F32)
    zero = jnp.zeros((blk, LANES), F32)
    carry = jnp.zeros((1, LANES), F32)
    for t in range(s // blk):
        rows = slice(t * blk, (t + 1) * blk)
        fc = _dot(h_ref[rows, :], wf_ref[...]) + bf_ref[...]
        log_f = (jnp.minimum(fc, 0.0) - jnp.log1p(jnp.exp(-jnp.abs(fc)))) * LOG2E
        x0, x1, x2 = _bf16_parts(log_f)
        cum = _dot(lower, x0) + _dot(lower, x1) + _dot(lower, x2) + carry
        carry = cum[blk - 1:blk, :]
        c0, c1, c2 = (c.astype(F32) for c in _bf16_parts(cum))
        parts = jnp.where(part == 0, c0, jnp.where(part == 1, c1, c2))
        fk = jnp.where(lane < GATE_LANES, parts, jnp.where(lane < 2 * GATE_LANES, one, zero))
        fq = jnp.where(lane < GATE_LANES, -one, jnp.where(lane < 2 * GATE_LANES, parts, zero))
        fk_ref[rows, :] = fk.astype(BF16)
        fq_ref[rows, :] = fq.astype(BF16)


def forget_features(h, wf_rep, bf_rep, *, batch, blk):
    m, d = h.shape
    s = m // batch
    part = (jnp.arange(LANES, dtype=jnp.int32) % GATE_PARTS).reshape(1, LANES)
    rows = pl.BlockSpec((s, LANES), lambda b: (b, 0))
    vec = pl.BlockSpec((1, LANES), lambda b: (0, 0))
    return pl.pallas_call(
        functools.partial(_forget_features_kernel, blk=blk),
        grid=(batch,),
        in_specs=[pl.BlockSpec((s, d), lambda b: (b, 0)),
                  pl.BlockSpec((d, LANES), lambda b: (0, 0)), vec, vec],
        out_specs=[rows, rows],
        out_shape=[jax.ShapeDtypeStruct((m, LANES), BF16)] * 2,
        compiler_params=_params("parallel"),
        name="forget_features",
    )(h, wf_rep, bf_rep, part)


def _rows(blk, t):
    return pl.ds(pl.multiple_of(blk * t, t), t)


def _row_block(ref, blk, t, h):
    return ref[_rows(blk, t), h * HEAD_DIM:(h + 1) * HEAD_DIM]


def _over_query_blocks(block_fn, nq):
    def q_block(i, carry):
        block_fn(i)
        return carry

    lax.fori_loop(0, nq, q_block, 0)


def _value_block_t(ref, kb, h):
    return ref[kb, h * HEAD_DIM:(h + 1) * HEAD_DIM, :]


def _run_pipeline(i, n_heads, stages, lag, prepare, finish):
    n = len(stages)
    lead = [(n - 1 - k) * lag for k in range(n)]
    assert lead[0] <= n_heads
    unprepared = set(range(n_heads))

    def positions(cur, nxt, cur_masked, first, last, finishing=False):
        for t in range(first, last):
            for k in range(n):
                idx = t + lead[k]
                if 0 <= idx < n_heads:
                    if k == 0 and idx in unprepared:
                        unprepared.discard(idx)
                        prepare(idx)
                    stages[k](cur, idx, cur_masked)
                    if finishing and k == n - 1:
                        finish(idx)
                elif idx >= n_heads and nxt is not None:
                    stages[k](nxt, idx - n_heads, False)

    positions(i, None, True, -lead[0], 0)
    positions(i, jnp.maximum(i - 1, 0), True, 0, n_heads)

    def body(j, carry):
        cur = i - j
        positions(cur, cur - 1, False, 0, n_heads)
        return carry

    lax.fori_loop(1, i, body, 0)

    @pl.when(i > 0)
    def _():
        positions(0, None, False, 0, n_heads, finishing=True)

    @pl.when(i == 0)
    def _():
        for h in range(n_heads):
            finish(h)


def _softmax_stages(scores, values_t, keep, m_ref, l_ref, acc_ref, s_buf, mn_buf, al_buf, ps_buf, pv_buf):
    def stage_scores(kb, h, masked):
        s_buf[h] = scores(kb, h)

    def stage_probabilities(kb, h, masked):
        s = s_buf[h]
        if masked:
            s = jnp.where(keep, s, NEG_INF)
        m_prev = m_ref[h]
        m_new = jnp.maximum(m_prev, jnp.max(s, axis=0, keepdims=True))
        p = jnp.exp2(s - m_new)
        mn_buf[h] = m_new
        al_buf[h] = jnp.exp2(m_prev - m_new)
        ps_buf[h] = jnp.sum(p, axis=0, keepdims=True)
        pv_buf[h] = _dot(values_t(kb, h), p.astype(BF16))

    def stage_commit(kb, h, masked):
        alpha = al_buf[h]
        m_ref[h] = mn_buf[h]
        l_ref[h] = alpha * l_ref[h] + ps_buf[h]
        acc_ref[h] = alpha * acc_ref[h] + pv_buf[h]

    return stage_scores, stage_probabilities, stage_commit


def _softmax_scratch(n_heads, tk, nq):
    stat = pltpu.VMEM((n_heads, 1, nq), F32)
    wide = pltpu.VMEM((n_heads, HEAD_DIM, nq), F32)
    return [stat, stat, wide,
            pltpu.VMEM((n_heads, tk, nq), F32), stat, stat, stat, wide]


def _init_softmax_stats(m_ref, l_ref, acc_ref):
    m_ref[...] = jnp.full(m_ref.shape, NEG_INF, F32)
    l_ref[...] = jnp.zeros(l_ref.shape, F32)
    acc_ref[...] = jnp.zeros(acc_ref.shape, F32)


def _group_specs(s, tq, width, q_col, v_row):
    nq = s // tq
    q_spec = pl.BlockSpec((s, width), lambda b: (b, q_col))
    k_spec = pl.BlockSpec((s, width), lambda b: (b, q_col + 1))
    vt_spec = pl.BlockSpec((nq, width, tq), lambda b: (b, v_row, 0))
    o_spec = pl.BlockSpec((s, width), lambda b: (b, 0))
    return q_spec, k_spec, vt_spec, o_spec


def _diff_attn_kernel(*refs, nq, **static):
    _over_query_blocks(lambda i: _diff_attn_block(i, *refs, **static), nq)


def _diff_attn_block(i, lq_ref, lk_ref, g_ref, q_ref, k_ref, vt_ref, o_ref, qq_ref, m_ref, l_ref, acc_ref,
                     *bufs, tq, lam_init, lag):
    low = lax.broadcasted_iota(jnp.int32, (HEAD_DIM, tq), 0) < DIFF_QK_DIM

    def prepare(h):
        q_t = _row_block(q_ref, i, tq, h).astype(F32).T
        qq_ref[h, :, 0:tq] = jnp.where(low, q_t, 0.0).astype(BF16)
        qq_ref[h, :, tq:2 * tq] = jnp.where(low, 0.0, q_t).astype(BF16)

    _init_softmax_stats(m_ref, l_ref, acc_ref)

    key = lax.broadcasted_iota(jnp.int32, (tq, 2 * tq), 0)
    qry = lax.broadcasted_iota(jnp.int32, (tq, 2 * tq), 1)
    keep = jnp.where(qry >= tq, qry - tq, qry) >= key

    def scores(kb, h):
        return _dot(_row_block(k_ref, kb, tq, h), qq_ref[h])

    stages = _softmax_stages(scores, lambda kb, h: _value_block_t(vt_ref, kb, h), keep,
                             m_ref, l_ref, acc_ref, *bufs)
    def finish(h):
        prod = lq_ref[...] * lk_ref[...]
        first = lax.broadcasted_iota(jnp.int32, prod.shape, 1) < DIFF_QK_DIM
        e1 = jnp.exp(jnp.sum(jnp.where(first, prod, 0.0), axis=1, keepdims=True))
        e2 = jnp.exp(jnp.sum(jnp.where(first, 0.0, prod), axis=1, keepdims=True))
        lam = (e1 - e2) + lam_init
        o_t = (acc_ref[h, :, 0:tq] / l_ref[h, :, 0:tq]
               - lam * (acc_ref[h, :, tq:2 * tq] / l_ref[h, :, tq:2 * tq]))
        o_ref[_rows(i, tq), h * HEAD_DIM:(h + 1) * HEAD_DIM] = (_rms(o_t.T, g_ref[...])
                                                                * (1.0 - lam_init)).astype(BF16)

    _run_pipeline(i, N_DIFF, stages, lag, prepare, finish)


def diff_attention(qk, vt, lam_q, lam_k, g_sub, lam_init, *, batch, tq, lag):
    m = qk.shape[0]
    s = m // batch
    vec = pl.BlockSpec((1, LANES), lambda b: (0, 0))
    q_spec, k_spec, vt_spec, o_spec = _group_specs(s, tq, A_W, QA_COL, VA_ROW)
    return pl.pallas_call(
        functools.partial(_diff_attn_kernel, nq=s // tq, tq=tq, lam_init=lam_init, lag=lag),
        grid=(batch,),
        in_specs=[vec, vec, vec, q_spec, k_spec, vt_spec],
        out_specs=o_spec,
        out_shape=jax.ShapeDtypeStruct((m, A_W), BF16),
        scratch_shapes=[pltpu.VMEM((N_DIFF, HEAD_DIM, 2 * tq), BF16)] + _softmax_scratch(N_DIFF, tq, 2 * tq),
        compiler_params=_params("parallel"),
        name="diff_attention",
    )(lam_q, lam_k, g_sub, qk, qk, vt)


def _sb_attn_kernel(*refs, nq, **static):
    _over_query_blocks(lambda i: _sb_attn_block(i, *refs, **static), nq)


def _sb_attn_block(i, q_ref, k_ref, vt_ref, o_ref, qn_ref, r_ref, acc_ref,
                   zn_buf, after_buf, cs_buf, pv_buf, *, tq, lag):
    def prepare(h):
        qn_ref[h] = (-_row_block(q_ref, i, tq, h).astype(F32)).T.astype(BF16)

    r_ref[...] = jnp.zeros(r_ref.shape, F32)
    acc_ref[...] = jnp.zeros(acc_ref.shape, F32)
    ss = lax.broadcasted_iota(jnp.int32, (tq, 2 * tq), 0)
    jj = lax.broadcasted_iota(jnp.int32, (tq, 2 * tq), 1)
    tri = (jnp.where(jj >= tq, jj - tq, jj) >= ss).astype(BF16)

    strict = lax.broadcasted_iota(jnp.int32, (tq, tq), 0) < lax.broadcasted_iota(jnp.int32, (tq, tq), 1)

    def scores(kb, h, masked):
        zn_buf[h] = _dot(_row_block(k_ref, kb, tq, h), qn_ref[h])

    def suffix_sums(kb, h, masked):
        zn = zn_buf[h]
        sp = jnp.log(1.0 + jnp.exp2(jnp.abs(zn) * -LOG2E))
        log_remain = jnp.minimum(zn, 0.0) - sp
        if masked:
            log_remain = jnp.where(strict, log_remain, 0.0)
        hi = log_remain.astype(BF16)
        lo = (log_remain - hi.astype(F32)).astype(BF16)
        after_buf[h] = _dot(tri, jnp.concatenate([hi, lo], axis=0))
        cs_buf[h] = jnp.sum(log_remain, axis=0, keepdims=True)

    def weights(kb, h, masked):
        w = jnp.exp((after_buf[h] + r_ref[h]) - zn_buf[h])
        if masked:
            w = jnp.where(strict, w, 0.0)
        pv_buf[h] = _dot(_value_block_t(vt_ref, kb, h), w.astype(BF16))
        r_ref[h] += cs_buf[h]

    def accumulate(kb, h, masked):
        acc_ref[h] += pv_buf[h]

    def finish(h):
        o_ref[_rows(i, tq), h * HEAD_DIM:(h + 1) * HEAD_DIM] = acc_ref[h].T.astype(BF16)

    _run_pipeline(i, N_SB, (scores, suffix_sums, weights, accumulate), lag, prepare, finish)


def stick_breaking_attention(qk, vt, *, batch, tq, lag):
    m = qk.shape[0]
    s = m // batch
    q_spec, k_spec, vt_spec, o_spec = _group_specs(s, tq, SB_W, QB_COL, VB_ROW)
    stat = pltpu.VMEM((N_SB, 1, tq), F32)
    wide = pltpu.VMEM((N_SB, HEAD_DIM, tq), F32)
    tile = pltpu.VMEM((N_SB, tq, tq), F32)
    return pl.pallas_call(
        functools.partial(_sb_attn_kernel, nq=s // tq, tq=tq, lag=lag),
        grid=(batch,),
        in_specs=[q_spec, k_spec, vt_spec],
        out_specs=o_spec,
        out_shape=jax.ShapeDtypeStruct((m, SB_W), BF16),
        scratch_shapes=[pltpu.VMEM((N_SB, HEAD_DIM, tq), BF16), stat, wide,
                        tile, tile, stat, wide],
        compiler_params=_params("parallel"),
        name="stick_breaking_attention",
    )(qk, qk, vt)


def _fox_attn_kernel(*refs, nq, **static):
    _over_query_blocks(lambda i: _fox_attn_block(i, *refs, **static), nq)


def _fox_attn_block(i, fq_ref, fk_ref, q_ref, k_ref, vt_ref, o_ref, qa_ref, m_ref, l_ref, acc_ref, *bufs, tq, lag):
    feat = lax.broadcasted_iota(jnp.int32, (LANES, tq), 0)
    fq_t = fq_ref[_rows(i, tq), :].astype(F32).T

    def prepare(h):
        lo = GATE_PARTS * h
        mine = (((feat >= lo) & (feat < lo + GATE_PARTS))
                | ((feat >= GATE_LANES + lo) & (feat < GATE_LANES + lo + GATE_PARTS)))
        qa_ref[h, 0:HEAD_DIM, :] = _row_block(q_ref, i, tq, h).astype(F32).T.astype(BF16)
        qa_ref[h, HEAD_DIM:2 * HEAD_DIM, :] = jnp.where(mine, fq_t, 0.0).astype(BF16)

    _init_softmax_stats(m_ref, l_ref, acc_ref)

    keep = lax.broadcasted_iota(jnp.int32, (tq, tq), 1) >= lax.broadcasted_iota(jnp.int32, (tq, tq), 0)

    def scores(kb, h):
        fk = fk_ref[pl.ds(pl.multiple_of(kb * tq, tq), tq), :]
        return _dot(jnp.concatenate([_row_block(k_ref, kb, tq, h), fk], axis=1), qa_ref[h])

    stages = _softmax_stages(scores, lambda kb, h: _value_block_t(vt_ref, kb, h), keep,
                             m_ref, l_ref, acc_ref, *bufs)
    def finish(h):
        o_ref[_rows(i, tq), h * HEAD_DIM:(h + 1) * HEAD_DIM] = (acc_ref[h] / l_ref[h]).T.astype(BF16)

    _run_pipeline(i, N_FOX, stages, lag, prepare, finish)


def forgetting_attention(qk, vt, fk, fq, *, batch, tq, lag):
    m = qk.shape[0]
    s = m // batch
    nq = s // tq
    q_spec, k_spec, vt_spec, o_spec = _group_specs(s, tq, FOX_W, QC_COL, VC_ROW)
    return pl.pallas_call(
        functools.partial(_fox_attn_kernel, nq=nq, tq=tq, lag=lag),
        grid=(batch,),
        in_specs=[pl.BlockSpec((s, LANES), lambda b: (b, 0)),
                  pl.BlockSpec((s, LANES), lambda b: (b, 0)),
                  q_spec, k_spec, vt_spec],
        out_specs=o_spec,
        out_shape=jax.ShapeDtypeStruct((m, FOX_W), BF16),
        scratch_shapes=[pltpu.VMEM((N_FOX, 2 * HEAD_DIM, tq), BF16)] + _softmax_scratch(N_FOX, tq, tq),
        compiler_params=_params("parallel"),
        name="forgetting_attention",
    )(fq, fk, qk, qk, vt)


def _residual_tail(y, x, g_post, g_next, x_out_ref, h_out_ref):
    x_new = x + _rms(y, g_post)
    x_out_ref[...] = x_new
    if h_out_ref is not None:
        h_out_ref[...] = _rms(x_new, g_next).astype(BF16)


def _mix_out_kernel(oa_ref, ob_ref, oc_ref, x_ref, w_ref, gb_ref, gc_ref, gp_ref, gn_ref, xo_ref, ho_ref):
    ob = _rms(ob_ref[...].astype(F32), gb_ref[...]).astype(BF16)
    oc = _rms(oc_ref[...].astype(F32), gc_ref[...]).astype(BF16)
    y = (_dot(oa_ref[...], w_ref[0:A_W, :]) + _dot(ob, w_ref[A_W:A_W + SB_W, :])
         + _dot(oc, w_ref[A_W + SB_W:, :]))
    _residual_tail(y, x_ref[...], gp_ref[...], gn_ref[...], xo_ref, ho_ref)


def mix_out(oa, ob, oc, x, w, g_sb, g_fox, g_post, g_next, *, bm):
    m, d = x.shape
    rows = lambda width: pl.BlockSpec((bm, width), lambda i: (i, 0))
    const = lambda width: pl.BlockSpec((1, width), lambda i: (0, 0))
    return pl.pallas_call(
        _mix_out_kernel,
        grid=(m // bm,),
        in_specs=[rows(A_W), rows(SB_W), rows(FOX_W), rows(d),
                  pl.BlockSpec(w.shape, lambda i: (0, 0)),
                  const(SB_W), const(FOX_W), const(d), const(d)],
        out_specs=[rows(d), rows(d)],
        out_shape=[jax.ShapeDtypeStruct((m, d), F32), jax.ShapeDtypeStruct((m, d), BF16)],
        compiler_params=_params("parallel"),
        name="mix_out",
    )(oa, ob, oc, x, w, g_sb.reshape(1, -1), g_fox.reshape(1, -1), g_post.reshape(1, -1), g_next.reshape(1, -1))


def _norm_matmul_kernel(x_ref, g_ref, w_ref, o_ref):
    o_ref[...] = _dot(_rms(x_ref[...], g_ref[...]).astype(BF16), w_ref[...]).astype(BF16)


def norm_matmul(x, g, w, *, bm):
    m, d = x.shape
    n = w.shape[1]
    return pl.pallas_call(
        _norm_matmul_kernel,
        grid=(m // bm,),
        in_specs=[pl.BlockSpec((bm, d), lambda i: (i, 0)), pl.BlockSpec((1, d), lambda i: (0, 0)),
                  pl.BlockSpec((d, n), lambda i: (0, 0))],
        out_specs=pl.BlockSpec((bm, n), lambda i: (i, 0)),
        out_shape=jax.ShapeDtypeStruct((m, n), BF16),
        compiler_params=_params("parallel"),
        name="memory_kv",
    )(x, g.reshape(1, d), w)


def _cross_attn_kernel(h_ref, x_ref, kv_ref, wq_ref, wo_ref, gp_ref, gn_ref, xo_ref, ho_ref):
    q = (_dot(h_ref[...], wq_ref[...]) * HEAD_DIM ** -0.5).astype(BF16)
    heads = []
    for hd in range(N_CROSS_HEADS):
        k = kv_ref[:, hd * HEAD_DIM:(hd + 1) * HEAD_DIM]
        v = kv_ref[:, CROSS_W + hd * HEAD_DIM:CROSS_W + (hd + 1) * HEAD_DIM]
        s = _dot_nt(q[:, hd * HEAD_DIM:(hd + 1) * HEAD_DIM], k)
        p = jnp.exp(s - jnp.max(s, axis=1, keepdims=True))
        o = _dot(p.astype(BF16), v) / jnp.sum(p, axis=1, keepdims=True)
        heads.append(o.astype(BF16))
    y = _dot(jnp.concatenate(heads, axis=1), wo_ref[...])
    _residual_tail(y, x_ref[...], gp_ref[...], gn_ref[...], xo_ref, ho_ref)


def cross_attention(h, x, kv, wq, wo, g_post, g_next, *, batch, bm):
    m, d = x.shape
    s = m // batch
    mem_len = kv.shape[0] // batch
    nb = s // bm
    rows = lambda width: pl.BlockSpec((bm, width), lambda b, i: (b * nb + i, 0))
    const = lambda shape: pl.BlockSpec(shape, lambda b, i: (0, 0))
    return pl.pallas_call(
        _cross_attn_kernel,
        grid=(batch, nb),
        in_specs=[rows(d), rows(d),
                  pl.BlockSpec((mem_len, 2 * CROSS_W), lambda b, i: (b, 0)),
                  const(wq.shape), const(wo.shape), const((1, d)), const((1, d))],
        out_specs=[rows(d), rows(d)],
        out_shape=[jax.ShapeDtypeStruct((m, d), F32), jax.ShapeDtypeStruct((m, d), BF16)],
        compiler_params=_params("parallel", "parallel"),
        name="cross_attention",
    )(h, x, kv, wq, wo, g_post.reshape(1, d), g_next.reshape(1, d))


def _ffn_up_kernel(h_ref, wg_ref, wu_ref, o_ref):
    h = h_ref[...]
    gate = _dot(h, wg_ref[...].astype(BF16))
    up = _dot(h, wu_ref[...].astype(BF16))
    o_ref[...] = (gate * jax.nn.sigmoid(gate) * up).astype(BF16)


def ffn_up(h, wg_all, wu_all, layer, *, bm, bn):
    m, d = h.shape
    n = wg_all.shape[2]
    wspec = pl.BlockSpec((None, d, bn), lambda i, j: (layer, 0, j))
    return pl.pallas_call(
        _ffn_up_kernel,
        grid=(m // bm, n // bn),
        in_specs=[pl.BlockSpec((bm, d), lambda i, j: (i, 0)), wspec, wspec],
        out_specs=pl.BlockSpec((bm, bn), lambda i, j: (i, j)),
        out_shape=jax.ShapeDtypeStruct((m, n), BF16),
        compiler_params=_params("parallel", "arbitrary"),
        name="ffn_up",
    )(h, wg_all, wu_all)


def _ffn_down_kernel(*refs, emit_h):
    if emit_h:
        a_ref, w_ref, x_ref, gp_ref, gn_ref, xo_ref, ho_ref = refs
    else:
        a_ref, w_ref, x_ref, gp_ref, xo_ref = refs
        gn_ref = ho_ref = None
    kk = pl.program_id(1)
    last = pl.num_programs(1) - 1
    part = _dot(a_ref[...], w_ref[...])

    @pl.when(kk == 0)
    def _():
        xo_ref[...] = part

    @pl.when((kk > 0) & (kk < last))
    def _():
        xo_ref[...] += part

    @pl.when(kk == last)
    def _():
        g_next = gn_ref[...] if emit_h else None
        _residual_tail(xo_ref[...] + part, x_ref[...], gp_ref[...], g_next, xo_ref, ho_ref)


def ffn_down(a, w, x, g_post, g_next, *, bm, bk):
    m, d = x.shape
    kdim = a.shape[1]
    emit_h = g_next is not None
    assert kdim // bk >= 2
    rows = pl.BlockSpec((bm, d), lambda i, k: (i, 0))
    const = pl.BlockSpec((1, d), lambda i, k: (0, 0))
    gains = [g_post.reshape(1, d)] + ([g_next.reshape(1, d)] if emit_h else [])
    out_shape = [jax.ShapeDtypeStruct((m, d), F32)] + ([jax.ShapeDtypeStruct((m, d), BF16)] if emit_h else [])
    return pl.pallas_call(
        functools.partial(_ffn_down_kernel, emit_h=emit_h),
        grid=(m // bm, kdim // bk),
        in_specs=[pl.BlockSpec((bm, bk), lambda i, k: (i, k)),
                  pl.BlockSpec((bk, d), lambda i, k: (k, 0)),
                  rows] + [const] * len(gains),
        out_specs=[rows] * len(out_shape),
        out_shape=out_shape,
        compiler_params=_params("parallel", "arbitrary"),
        name="ffn_down",
    )(a, w, x, *gains)


def _pick(n, *candidates):
    for c in candidates:
        if n % c == 0:
            return c
    return n


def _columns(w, names):
    return jnp.concatenate([w[..., _IN_OFFSETS[n][0]:_IN_OFFSETS[n][1]] for n in names], axis=-1)


def kernel(x, mem, positions, g_mix_pre, g_mix_post, w_in, b_f, lam_q1, lam_k1, lam_q2, lam_k2, g_diff_sub, g_sb_out, g_fox_out, w_out, g_x_pre, g_x_post, g_mem, w_cq, w_ckv, w_co, g_ffn_pre, g_ffn_post, w_gate, w_up, w_down):
    batch, seq, d = x.shape
    depth = w_in.shape[0]
    m = batch * seq
    d_ff = w_gate.shape[2]
    assert w_in.shape[2] == _IN_OFFSETS["fc"][1] and d == V_ROWS

    bm_row = _pick(m, 512, 256, 128)
    bm_mm = _pick(m, 1024, 512, 256, 128)
    bm_big = _pick(m, 2048, 1024, 512, 256, 128)
    lag_diff, lag_sb, lag_fox = 2, 2, 3
    bn_in = 2 * A_W
    bn_ff = _pick(d_ff, 512, 256, 128)
    bk_ff = _pick(d_ff, 2816, 1408, 1024, 512, 256, 128)
    tq = _pick(seq, 256, 128)
    bm_x = _pick(seq, 512, 256, 128)
    bm_mem = _pick(mem.shape[0] * mem.shape[1], 512, 256, 128)

    w_qk = _columns(w_in, QK_ORDER).astype(BF16)
    w_qk_rope = _columns(w_in, QK_ROPE_ORDER).astype(BF16)
    w_vt = jnp.swapaxes(_columns(w_in, V_ORDER), 1, 2).astype(BF16)
    gate_lane_head = jnp.arange(2 * GATE_LANES) % GATE_LANES // GATE_PARTS
    wf_rep = jnp.pad(_columns(w_in, ("fc",))[:, :, gate_lane_head],
                     ((0, 0), (0, 0), (0, LANES - 2 * GATE_LANES))).astype(BF16)
    bf_rep = jnp.pad(b_f[:, gate_lane_head], ((0, 0), (0, LANES - 2 * GATE_LANES))).reshape(depth, 1, LANES)
    w_out_b, w_cq_b, w_ckv_b, w_co_b = (w.astype(BF16) for w in (w_out, w_cq, w_ckv, w_co))
    w_down_b = w_down.astype(BF16)
    lam_q = jnp.concatenate([lam_q1, lam_q2], axis=1).reshape(depth, 1, LANES)
    lam_k = jnp.concatenate([lam_k1, lam_k2], axis=1).reshape(depth, 1, LANES)

    def scales(*groups):
        return jnp.concatenate([jnp.full((w,), v, F32) for w, v in groups]).reshape(1, -1)

    colscale = scales((SB_W, HEAD_DIM ** -0.5), (SB_W, 1.0), (FOX_W, HEAD_DIM ** -0.5 * LOG2E), (FOX_W, 1.0))
    colscale_rope = scales((A_W, DIFF_QK_DIM ** -0.5 * LOG2E), (A_W, 1.0))

    x = x.reshape(m, d)
    mem2 = mem.reshape(-1, d)
    cos_t, sin_t = rope_tables(positions, bm=bm_row)
    h = prenorm(x, g_mix_pre[0], bm=bm_row)

    for l in range(depth):
        lam_init = 0.8 - 0.6 * math.exp(-0.3 * l)
        qk = in_proj(h, w_qk[l], colscale, None, bm=bm_big, bn=bn_in)
        qk_rope = in_proj(h, w_qk_rope[l], colscale_rope, (cos_t, sin_t), bm=bm_big, bn=bn_in)
        vt = in_proj_vt(h, w_vt[l], bm=bm_big, bn=bn_in, tk=tq)
        fk, fq = forget_features(h, wf_rep[l], bf_rep[l], batch=batch, blk=tq)
        oa = diff_attention(qk_rope, vt, lam_q[l], lam_k[l], g_diff_sub[l].reshape(1, LANES), lam_init, batch=batch, tq=tq,
                            lag=lag_diff)
        ob = stick_breaking_attention(qk, vt, batch=batch, tq=tq, lag=lag_sb)
        oc = forgetting_attention(qk, vt, fk, fq, batch=batch, tq=tq, lag=lag_fox)
        x, h = mix_out(oa, ob, oc, x, w_out_b[l], g_sb_out[l], g_fox_out[l], g_mix_post[l], g_x_pre[l], bm=bm_row)
        kv = norm_matmul(mem2, g_mem[l], w_ckv_b[l], bm=bm_mem)
        x, h = cross_attention(h, x, kv, w_cq_b[l], w_co_b[l], g_x_post[l], g_ffn_pre[l], batch=batch, bm=bm_x)
        a = ffn_up(h, w_gate, w_up, l, bm=bm_mm, bn=bn_ff)
        g_next = g_mix_pre[l + 1] if l + 1 < depth else None
        x, *rest = ffn_down(a, w_down_b[l], x, g_ffn_post[l], g_next, bm=bm_row, bk=bk_ff)
        h = rest[0] if rest else None
    return x.reshape(batch, seq, d)
```

```python
import functools
import math

import jax
import jax.numpy as jnp
from jax import lax
from jax.experimental import pallas as pl
from jax.experimental.pallas import tpu as pltpu

F32 = jnp.float32
BF16 = jnp.bfloat16

EPS = 1e-6
NEG_INF = -1e30
ROPE_THETA = 10000.0
LOG2E = math.log2(math.e)

LANES = 128
HEAD_DIM = 128
N_DIFF, N_SB, N_FOX = 4, 6, 6
DIFF_QK_DIM = HEAD_DIM // 2
A_W, SB_W, FOX_W = N_DIFF * HEAD_DIM, N_SB * HEAD_DIM, N_FOX * HEAD_DIM
V_ROWS = A_W + SB_W + FOX_W
N_CROSS_HEADS = 4
CROSS_W = N_CROSS_HEADS * HEAD_DIM
GATE_PARTS = 3
GATE_LANES = GATE_PARTS * N_FOX

VMEM_LIMIT = 56 * 1024 * 1024

_IN_OFFSETS = {}
_off = 0
for _name, _width in (("qa", A_W), ("ka", A_W), ("va", A_W), ("qb", SB_W), ("kb", SB_W), ("vb", SB_W),
                      ("qc", FOX_W), ("kc", FOX_W), ("vc", FOX_W), ("fc", N_FOX)):
    _IN_OFFSETS[_name] = (_off, _off + _width)
    _off += _width
QK_ORDER = ("qb", "kb", "qc", "kc")
QK_ROPE_ORDER = ("qa", "ka")
V_ORDER = ("vb", "vc", "va")
QB_COL, QC_COL, QA_COL = 0, 2, 0
VB_ROW, VC_ROW, VA_ROW = 0, 1, (2 * SB_W) // A_W


def _params(*semantics):
    return pltpu.CompilerParams(dimension_semantics=semantics, vmem_limit_bytes=VMEM_LIMIT)


def _rms(xf, g):
    return xf * lax.rsqrt(jnp.mean(xf * xf, axis=-1, keepdims=True) + EPS) * g


def _dot(a, b):
    return jnp.dot(a, b, preferred_element_type=F32)


def _dot_nt(a, b):
    return lax.dot_general(a, b, (((1,), (1,)), ((), ())), preferred_element_type=F32)


def _bf16_parts(x):
    p0 = x.astype(BF16)
    r1 = x - p0.astype(F32)
    p1 = r1.astype(BF16)
    p2 = (r1 - p1.astype(F32)).astype(BF16)
    return p0, p1, p2


def _prenorm_kernel(x_ref, g_ref, h_ref):
    h_ref[...] = _rms(x_ref[...], g_ref[...]).astype(BF16)


def prenorm(x, g, *, bm):
    m, d = x.shape
    return pl.pallas_call(
        _prenorm_kernel,
        grid=(m // bm,),
        in_specs=[pl.BlockSpec((bm, d), lambda i: (i, 0)), pl.BlockSpec((1, d), lambda i: (0, 0))],
        out_specs=pl.BlockSpec((bm, d), lambda i: (i, 0)),
        out_shape=jax.ShapeDtypeStruct((m, d), BF16),
        compiler_params=_params("parallel"),
        name="prenorm",
    )(x, g.reshape(1, d))


def _rope_table_kernel(pos_ref, invf_ref, sign_ref, cos_ref, sin_ref):
    ang = pos_ref[...].astype(F32) * invf_ref[...]
    cos_ref[...] = jnp.cos(ang)
    sin_ref[...] = jnp.sin(ang) * sign_ref[...]


def rope_tables(positions, *, bm):
    m = positions.size
    half = DIFF_QK_DIM // 2
    inv_freq = ROPE_THETA ** (-jnp.arange(half, dtype=F32) / half)
    invf = jnp.tile(inv_freq, LANES // half).reshape(1, LANES)
    sign = jnp.where((jnp.arange(LANES) % DIFF_QK_DIM) < half, -1.0, 1.0).astype(F32).reshape(1, LANES)
    row = pl.BlockSpec((bm, LANES), lambda i: (i, 0))
    const = pl.BlockSpec((1, LANES), lambda i: (0, 0))
    return pl.pallas_call(
        _rope_table_kernel,
        grid=(m // bm,),
        in_specs=[pl.BlockSpec((bm, 1), lambda i: (i, 0)), const, const],
        out_specs=[row, row],
        out_shape=[jax.ShapeDtypeStruct((m, LANES), F32)] * 2,
        compiler_params=_params("parallel"),
        name="rope_tables",
    )(positions.reshape(m, 1), invf, sign)


def _in_proj_kernel(h_ref, w_ref, cs_ref, o_ref):
    o_ref[...] = (_dot(h_ref[...], w_ref[...]) * cs_ref[...]).astype(BF16)


def _in_proj_rope_kernel(h_ref, w_ref, cs_ref, cos_ref, sin_ref, o_ref):
    acc = _dot(h_ref[...], w_ref[...]) * cs_ref[...]
    bm, bn = acc.shape
    c, s = cos_ref[...], sin_ref[...]
    first_half = (lax.broadcasted_iota(jnp.int32, (bm, LANES), 1) % DIFF_QK_DIM) < DIFF_QK_DIM // 2
    for t in range(bn // LANES):
        a = acc[:, t * LANES:(t + 1) * LANES]
        partner = jnp.where(first_half, pltpu.roll(a, LANES - DIFF_QK_DIM // 2, 1),
                            pltpu.roll(a, DIFF_QK_DIM // 2, 1))
        o_ref[:, t * LANES:(t + 1) * LANES] = (a * c + partner * s).astype(BF16)


def in_proj(h, w, colscale, rope_tables_or_none, *, bm, bn):
    m, d = h.shape
    n = w.shape[1]
    assert n % bn == 0 and m % bm == 0
    rope = rope_tables_or_none is not None
    row_tab = pl.BlockSpec((bm, LANES), lambda i, j: (i, 0))
    return pl.pallas_call(
        _in_proj_rope_kernel if rope else _in_proj_kernel,
        grid=(m // bm, n // bn),
        in_specs=[pl.BlockSpec((bm, d), lambda i, j: (i, 0)),
                  pl.BlockSpec((d, bn), lambda i, j: (0, j)),
                  pl.BlockSpec((1, bn), lambda i, j: (0, j))] + ([row_tab, row_tab] if rope else []),
        out_specs=pl.BlockSpec((bm, bn), lambda i, j: (i, j)),
        out_shape=jax.ShapeDtypeStruct((m, n), BF16),
        compiler_params=_params("parallel", "arbitrary"),
        name="in_proj_rope" if rope else "in_proj",
    )(h, w, colscale, *(rope_tables_or_none or ()))


def _in_proj_vt_kernel(h_ref, wt_ref, o_ref, *, tk):
    res = _dot_nt(wt_ref[...], h_ref[...]).astype(BF16)
    for c in range(o_ref.shape[0]):
        o_ref[c] = res[:, c * tk:(c + 1) * tk]


def in_proj_vt(h, wt, *, bm, bn, tk):
    m, d = h.shape
    n = wt.shape[0]
    return pl.pallas_call(
        functools.partial(_in_proj_vt_kernel, tk=tk),
        grid=(m // bm, n // bn),
        in_specs=[pl.BlockSpec((bm, d), lambda i, j: (i, 0)),
                  pl.BlockSpec((bn, d), lambda i, j: (j, 0))],
        out_specs=pl.BlockSpec((bm // tk, bn, tk), lambda i, j: (i, j, 0)),
        out_shape=jax.ShapeDtypeStruct((m // tk, n, tk), BF16),
        compiler_params=_params("parallel", "arbitrary"),
        name="in_proj_vt",
    )(h, wt)


def _forget_features_kernel(h_ref, wf_ref, bf_ref, part_ref, fk_ref, fq_ref, *, blk):
    s = h_ref.shape[0]
    lane = lax.broadcasted_iota(jnp.int32, (blk, LANES), 1)
    part = jnp.broadcast_to(part_ref[...], (blk, LANES))
    lower = (lax.broadcasted_iota(jnp.int32, (blk, blk), 0)
             >= lax.broadcasted_iota(jnp.int32, (blk, blk), 1)).astype(BF16)
    one = jnp.ones((blk, LANES), F32)
    zero = jnp.zeros((blk, LANES), F32)
    carry = jnp.zeros((1, LANES), F32)
    for t in range(s // blk):
        rows = slice(t * blk, (t + 1) * blk)
        fc = _dot(h_ref[rows, :], wf_ref[...]) + bf_ref[...]
        log_f = (jnp.minimum(fc, 0.0) - jnp.log1p(jnp.exp(-jnp.abs(fc)))) * LOG2E
        x0, x1, x2 = _bf16_parts(log_f)
        cum = _dot(lower, x0) + _dot(lower, x1) + _dot(lower, x2) + carry
        carry = cum[blk - 1:blk, :]
        c0, c1, c2 = (c.astype(F32) for c in _bf16_parts(cum))
        parts = jnp.where(part == 0, c0, jnp.where(part == 1, c1, c2))
        fk = jnp.where(lane < GATE_LANES, parts, jnp.where(lane < 2 * GATE_LANES, one, zero))
        fq = jnp.where(lane < GATE_LANES, -one, jnp.where(lane < 2 * GATE_LANES, parts, zero))
        fk_ref[rows, :] = fk.astype(BF16)
        fq_ref[rows, :] = fq.astype(BF16)


def forget_features(h, wf_rep, bf_rep, *, batch, blk):
    m, d = h.shape
    s = m // batch
    part = (jnp.arange(LANES, dtype=jnp.int32) % GATE_PARTS).reshape(1, LANES)
    rows = pl.BlockSpec((s, LANES), lambda b: (b, 0))
    vec = pl.BlockSpec((1, LANES), lambda b: (0, 0))
    return pl.pallas_call(
        functools.partial(_forget_features_kernel, blk=blk),
        grid=(batch,),
        in_specs=[pl.BlockSpec((s, d), lambda b: (b, 0)),
                  pl.BlockSpec((d, LANES), lambda b: (0, 0)), vec, vec],
        out_specs=[rows, rows],
        out_shape=[jax.ShapeDtypeStruct((m, LANES), BF16)] * 2,
        compiler_params=_params("parallel"),
        name="forget_features",
    )(h, wf_rep, bf_rep, part)


def _rows(blk, t):
    return pl.ds(pl.multiple_of(blk * t, t), t)


def _row_block(ref, blk, t, h):
    return ref[_rows(blk, t), h * HEAD_DIM:(h + 1) * HEAD_DIM]


def _over_query_blocks(block_fn, nq):
    def q_block(i, carry):
        block_fn(i)
        return carry

    lax.fori_loop(0, nq, q_block, 0)


def _value_block_t(ref, kb, h):
    return ref[kb, h * HEAD_DIM:(h + 1) * HEAD_DIM, :]


def _run_pipeline(i, n_heads, stages, lag, prepare, finish):
    n = len(stages)
    lead = [(n - 1 - k) * lag for k in range(n)]
    assert lead[0] <= n_heads
    unprepared = set(range(n_heads))

    def positions(cur, nxt, cur_masked, first, last, finishing=False):
        for t in range(first, last):
            for k in range(n):
                idx = t + lead[k]
                if 0 <= idx < n_heads:
                    if k == 0 and idx in unprepared:
                        unprepared.discard(idx)
                        prepare(idx)
                    stages[k](cur, idx, cur_masked)
                    if finishing and k == n - 1:
                        finish(idx)
                elif idx >= n_heads and nxt is not None:
                    stages[k](nxt, idx - n_heads, False)

    positions(i, None, True, -lead[0], 0)
    positions(i, jnp.maximum(i - 1, 0), True, 0, n_heads)

    def body(j, carry):
        cur = i - j
        positions(cur, cur - 1, False, 0, n_heads)
        return carry

    lax.fori_loop(1, i, body, 0)

    @pl.when(i > 0)
    def _():
        positions(0, None, False, 0, n_heads, finishing=True)

    @pl.when(i == 0)
    def _():
        for h in range(n_heads):
            finish(h)


def _softmax_stages(scores, values_t, keep, m_ref, l_ref, acc_ref, s_buf, mn_buf, al_buf, ps_buf, pv_buf):
    def stage_scores(kb, h, masked):
        s_buf[h] = scores(kb, h)

    def stage_probabilities(kb, h, masked):
        s = s_buf[h]
        if masked:
            s = jnp.where(keep, s, NEG_INF)
        m_prev = m_ref[h]
        m_new = jnp.maximum(m_prev, jnp.max(s, axis=0, keepdims=True))
        p = jnp.exp2(s - m_new)
        mn_buf[h] = m_new
        al_buf[h] = jnp.exp2(m_prev - m_new)
        ps_buf[h] = jnp.sum(p, axis=0, keepdims=True)
        pv_buf[h] = _dot(values_t(kb, h), p.astype(BF16))

    def stage_commit(kb, h, masked):
        alpha = al_buf[h]
        m_ref[h] = mn_buf[h]
        l_ref[h] = alpha * l_ref[h] + ps_buf[h]
        acc_ref[h] = alpha * acc_ref[h] + pv_buf[h]

    return stage_scores, stage_probabilities, stage_commit


def _softmax_scratch(n_heads, tk, nq):
    stat = pltpu.VMEM((n_heads, 1, nq), F32)
    wide = pltpu.VMEM((n_heads, HEAD_DIM, nq), F32)
    return [stat, stat, wide,
            pltpu.VMEM((n_heads, tk, nq), F32), stat, stat, stat, wide]


def _init_softmax_stats(m_ref, l_ref, acc_ref):
    m_ref[...] = jnp.full(m_ref.shape, NEG_INF, F32)
    l_ref[...] = jnp.zeros(l_ref.shape, F32)
    acc_ref[...] = jnp.zeros(acc_ref.shape, F32)


def _group_specs(s, tq, width, q_col, v_row):
    nq = s // tq
    q_spec = pl.BlockSpec((s, width), lambda b: (b, q_col))
    k_spec = pl.BlockSpec((s, width), lambda b: (b, q_col + 1))
    vt_spec = pl.BlockSpec((nq, width, tq), lambda b: (b, v_row, 0))
    o_spec = pl.BlockSpec((s, width), lambda b: (b, 0))
    return q_spec, k_spec, vt_spec, o_spec


def _diff_attn_kernel(*refs, nq, **static):
    _over_query_blocks(lambda i: _diff_attn_block(i, *refs, **static), nq)


def _diff_attn_block(i, lq_ref, lk_ref, g_ref, q_ref, k_ref, vt_ref, o_ref, qq_ref, m_ref, l_ref, acc_ref,
                     *bufs, tq, lam_init, lag):
    low = lax.broadcasted_iota(jnp.int32, (HEAD_DIM, tq), 0) < DIFF_QK_DIM

    def prepare(h):
        q_t = _row_block(q_ref, i, tq, h).astype(F32).T
        qq_ref[h, :, 0:tq] = jnp.where(low, q_t, 0.0).astype(BF16)
        qq_ref[h, :, tq:2 * tq] = jnp.where(low, 0.0, q_t).astype(BF16)

    _init_softmax_stats(m_ref, l_ref, acc_ref)

    key = lax.broadcasted_iota(jnp.int32, (tq, 2 * tq), 0)
    qry = lax.broadcasted_iota(jnp.int32, (tq, 2 * tq), 1)
    keep = jnp.where(qry >= tq, qry - tq, qry) >= key

    def scores(kb, h):
        return _dot(_row_block(k_ref, kb, tq, h), qq_ref[h])

    stages = _softmax_stages(scores, lambda kb, h: _value_block_t(vt_ref, kb, h), keep,
                             m_ref, l_ref, acc_ref, *bufs)
    def finish(h):
        prod = lq_ref[...] * lk_ref[...]
        first = lax.broadcasted_iota(jnp.int32, prod.shape, 1) < DIFF_QK_DIM
        e1 = jnp.exp(jnp.sum(jnp.where(first, prod, 0.0), axis=1, keepdims=True))
        e2 = jnp.exp(jnp.sum(jnp.where(first, 0.0, prod), axis=1, keepdims=True))
        lam = (e1 - e2) + lam_init
        o_t = (acc_ref[h, :, 0:tq] / l_ref[h, :, 0:tq]
               - lam * (acc_ref[h, :, tq:2 * tq] / l_ref[h, :, tq:2 * tq]))
        o_ref[_rows(i, tq), h * HEAD_DIM:(h + 1) * HEAD_DIM] = (_rms(o_t.T, g_ref[...])
                                                                * (1.0 - lam_init)).astype(BF16)

    _run_pipeline(i, N_DIFF, stages, lag, prepare, finish)


def diff_attention(qk, vt, lam_q, lam_k, g_sub, lam_init, *, batch, tq, lag):
    m = qk.shape[0]
    s = m // batch
    vec = pl.BlockSpec((1, LANES), lambda b: (0, 0))
    q_spec, k_spec, vt_spec, o_spec = _group_specs(s, tq, A_W, QA_COL, VA_ROW)
    return pl.pallas_call(
        functools.partial(_diff_attn_kernel, nq=s // tq, tq=tq, lam_init=lam_init, lag=lag),
        grid=(batch,),
        in_specs=[vec, vec, vec, q_spec, k_spec, vt_spec],
        out_specs=o_spec,
        out_shape=jax.ShapeDtypeStruct((m, A_W), BF16),
        scratch_shapes=[pltpu.VMEM((N_DIFF, HEAD_DIM, 2 * tq), BF16)] + _softmax_scratch(N_DIFF, tq, 2 * tq),
        compiler_params=_params("parallel"),
        name="diff_attention",
    )(lam_q, lam_k, g_sub, qk, qk, vt)


def _sb_attn_kernel(*refs, nq, **static):
    _over_query_blocks(lambda i: _sb_attn_block(i, *refs, **static), nq)


def _sb_attn_block(i, q_ref, k_ref, vt_ref, o_ref, qn_ref, r_ref, acc_ref,
                   zn_buf, after_buf, cs_buf, pv_buf, *, tq, lag):
    def prepare(h):
        qn_ref[h] = (-_row_block(q_ref, i, tq, h).astype(F32)).T.astype(BF16)

    r_ref[...] = jnp.zeros(r_ref.shape, F32)
    acc_ref[...] = jnp.zeros(acc_ref.shape, F32)
    ss = lax.broadcasted_iota(jnp.int32, (tq, tq), 0)
    jj = lax.broadcasted_iota(jnp.int32, (tq, tq), 1)
    tri = (jj >= ss).astype(BF16)

    strict = lax.broadcasted_iota(jnp.int32, (tq, tq), 0) < lax.broadcasted_iota(jnp.int32, (tq, tq), 1)

    def scores(kb, h, masked):
        zn_buf[h] = _dot(_row_block(k_ref, kb, tq, h), qn_ref[h])

    def suffix_sums(kb, h, masked):
        zn = zn_buf[h]
        sp = jnp.log(1.0 + jnp.exp2(jnp.abs(zn) * -LOG2E))
        log_remain = jnp.minimum(zn, 0.0) - sp
        if masked:
            log_remain = jnp.where(strict, log_remain, 0.0)
        after_buf[h] = _dot(tri, log_remain.astype(BF16))
        cs_buf[h] = jnp.sum(log_remain, axis=0, keepdims=True)

    def weights(kb, h, masked):
        w = jnp.exp((after_buf[h] + r_ref[h]) - zn_buf[h])
        if masked:
            w = jnp.where(strict, w, 0.0)
        pv_buf[h] = _dot(_value_block_t(vt_ref, kb, h), w.astype(BF16))
        r_ref[h] += cs_buf[h]

    def accumulate(kb, h, masked):
        acc_ref[h] += pv_buf[h]

    def finish(h):
        o_ref[_rows(i, tq), h * HEAD_DIM:(h + 1) * HEAD_DIM] = acc_ref[h].T.astype(BF16)

    _run_pipeline(i, N_SB, (scores, suffix_sums, weights, accumulate), lag, prepare, finish)


def stick_breaking_attention(qk, vt, *, batch, tq, lag):
    m = qk.shape[0]
    s = m // batch
    q_spec, k_spec, vt_spec, o_spec = _group_specs(s, tq, SB_W, QB_COL, VB_ROW)
    stat = pltpu.VMEM((N_SB, 1, tq), F32)
    wide = pltpu.VMEM((N_SB, HEAD_DIM, tq), F32)
    tile = pltpu.VMEM((N_SB, tq, tq), F32)
    return pl.pallas_call(
        functools.partial(_sb_attn_kernel, nq=s // tq, tq=tq, lag=lag),
        grid=(batch,),
        in_specs=[q_spec, k_spec, vt_spec],
        out_specs=o_spec,
        out_shape=jax.ShapeDtypeStruct((m, SB_W), BF16),
        scratch_shapes=[pltpu.VMEM((N_SB, HEAD_DIM, tq), BF16), stat, wide,
                        tile, tile, stat, wide],
        compiler_params=_params("parallel"),
        name="stick_breaking_attention",
    )(qk, qk, vt)


def _fox_attn_kernel(*refs, nq, **static):
    _over_query_blocks(lambda i: _fox_attn_block(i, *refs, **static), nq)


def _fox_attn_block(i, fq_ref, fk_ref, q_ref, k_ref, vt_ref, o_ref, qa_ref, m_ref, l_ref, acc_ref, *bufs, tq, lag):
    feat = lax.broadcasted_iota(jnp.int32, (LANES, tq), 0)
    fq_t = fq_ref[_rows(i, tq), :].astype(F32).T

    def prepare(h):
        lo = GATE_PARTS * h
        mine = (((feat >= lo) & (feat < lo + GATE_PARTS))
                | ((feat >= GATE_LANES + lo) & (feat < GATE_LANES + lo + GATE_PARTS)))
        qa_ref[h, 0:HEAD_DIM, :] = _row_block(q_ref, i, tq, h).astype(F32).T.astype(BF16)
        qa_ref[h, HEAD_DIM:2 * HEAD_DIM, :] = jnp.where(mine, fq_t, 0.0).astype(BF16)

    _init_softmax_stats(m_ref, l_ref, acc_ref)

    keep = lax.broadcasted_iota(jnp.int32, (tq, tq), 1) >= lax.broadcasted_iota(jnp.int32, (tq, tq), 0)

    def scores(kb, h):
        fk = fk_ref[pl.ds(pl.multiple_of(kb * tq, tq), tq), :]
        return _dot(jnp.concatenate([_row_block(k_ref, kb, tq, h), fk], axis=1), qa_ref[h])

    stages = _softmax_stages(scores, lambda kb, h: _value_block_t(vt_ref, kb, h), keep,
                             m_ref, l_ref, acc_ref, *bufs)
    def finish(h):
        o_ref[_rows(i, tq), h * HEAD_DIM:(h + 1) * HEAD_DIM] = (acc_ref[h] / l_ref[h]).T.astype(BF16)

    _run_pipeline(i, N_FOX, stages, lag, prepare, finish)


def forgetting_attention(qk, vt, fk, fq, *, batch, tq, lag):
    m = qk.shape[0]
    s = m // batch
    nq = s // tq
    q_spec, k_spec, vt_spec, o_spec = _group_specs(s, tq, FOX_W, QC_COL, VC_ROW)
    return pl.pallas_call(
        functools.partial(_fox_attn_kernel, nq=nq, tq=tq, lag=lag),
        grid=(batch,),
        in_specs=[pl.BlockSpec((s, LANES), lambda b: (b, 0)),
                  pl.BlockSpec((s, LANES), lambda b: (b, 0)),
                  q_spec, k_spec, vt_spec],
        out_specs=o_spec,
        out_shape=jax.ShapeDtypeStruct((m, FOX_W), BF16),
        scratch_shapes=[pltpu.VMEM((N_FOX, 2 * HEAD_DIM, tq), BF16)] + _softmax_scratch(N_FOX, tq, tq),
        compiler_params=_params("parallel"),
        name="forgetting_attention",
    )(fq, fk, qk, qk, vt)


def _residual_tail(y, x, g_post, g_next, x_out_ref, h_out_ref):
    x_new = x + _rms(y, g_post)
    x_out_ref[...] = x_new
    if h_out_ref is not None:
        h_out_ref[...] = _rms(x_new, g_next).astype(BF16)


def _mix_out_kernel(oa_ref, ob_ref, oc_ref, x_ref, w_ref, gb_ref, gc_ref, gp_ref, gn_ref, xo_ref, ho_ref):
    ob = _rms(ob_ref[...].astype(F32), gb_ref[...]).astype(BF16)
    oc = _rms(oc_ref[...].astype(F32), gc_ref[...]).astype(BF16)
    y = (_dot(oa_ref[...], w_ref[0:A_W, :]) + _dot(ob, w_ref[A_W:A_W + SB_W, :])
         + _dot(oc, w_ref[A_W + SB_W:, :]))
    _residual_tail(y, x_ref[...], gp_ref[...], gn_ref[...], xo_ref, ho_ref)


def mix_out(oa, ob, oc, x, w, g_sb, g_fox, g_post, g_next, *, bm):
    m, d = x.shape
    rows = lambda width: pl.BlockSpec((bm, width), lambda i: (i, 0))
    const = lambda width: pl.BlockSpec((1, width), lambda i: (0, 0))
    return pl.pallas_call(
        _mix_out_kernel,
        grid=(m // bm,),
        in_specs=[rows(A_W), rows(SB_W), rows(FOX_W), rows(d),
                  pl.BlockSpec(w.shape, lambda i: (0, 0)),
                  const(SB_W), const(FOX_W), const(d), const(d)],
        out_specs=[rows(d), rows(d)],
        out_shape=[jax.ShapeDtypeStruct((m, d), F32), jax.ShapeDtypeStruct((m, d), BF16)],
        compiler_params=_params("parallel"),
        name="mix_out",
    )(oa, ob, oc, x, w, g_sb.reshape(1, -1), g_fox.reshape(1, -1), g_post.reshape(1, -1), g_next.reshape(1, -1))


def _norm_matmul_kernel(x_ref, g_ref, w_ref, o_ref):
    o_ref[...] = _dot(_rms(x_ref[...], g_ref[...]).astype(BF16), w_ref[...]).astype(BF16)


def norm_matmul(x, g, w, *, bm):
    m, d = x.shape
    n = w.shape[1]
    return pl.pallas_call(
        _norm_matmul_kernel,
        grid=(m // bm,),
        in_specs=[pl.BlockSpec((bm, d), lambda i: (i, 0)), pl.BlockSpec((1, d), lambda i: (0, 0)),
                  pl.BlockSpec((d, n), lambda i: (0, 0))],
        out_specs=pl.BlockSpec((bm, n), lambda i: (i, 0)),
        out_shape=jax.ShapeDtypeStruct((m, n), BF16),
        compiler_params=_params("parallel"),
        name="memory_kv",
    )(x, g.reshape(1, d), w)


def _cross_attn_kernel(h_ref, x_ref, kv_ref, wq_ref, wo_ref, gp_ref, gn_ref, xo_ref, ho_ref):
    q = (_dot(h_ref[...], wq_ref[...]) * HEAD_DIM ** -0.5).astype(BF16)
    heads = []
    for hd in range(N_CROSS_HEADS):
        k = kv_ref[:, hd * HEAD_DIM:(hd + 1) * HEAD_DIM]
        v = kv_ref[:, CROSS_W + hd * HEAD_DIM:CROSS_W + (hd + 1) * HEAD_DIM]
        s = _dot_nt(q[:, hd * HEAD_DIM:(hd + 1) * HEAD_DIM], k)
        p = jnp.exp(s - jnp.max(s, axis=1, keepdims=True))
        o = _dot(p.astype(BF16), v) / jnp.sum(p, axis=1, keepdims=True)
        heads.append(o.astype(BF16))
    y = _dot(jnp.concatenate(heads, axis=1), wo_ref[...])
    _residual_tail(y, x_ref[...], gp_ref[...], gn_ref[...], xo_ref, ho_ref)


def cross_attention(h, x, kv, wq, wo, g_post, g_next, *, batch, bm):
    m, d = x.shape
    s = m // batch
    mem_len = kv.shape[0] // batch
    nb = s // bm
    rows = lambda width: pl.BlockSpec((bm, width), lambda b, i: (b * nb + i, 0))
    const = lambda shape: pl.BlockSpec(shape, lambda b, i: (0, 0))
    return pl.pallas_call(
        _cross_attn_kernel,
        grid=(batch, nb),
        in_specs=[rows(d), rows(d),
                  pl.BlockSpec((mem_len, 2 * CROSS_W), lambda b, i: (b, 0)),
                  const(wq.shape), const(wo.shape), const((1, d)), const((1, d))],
        out_specs=[rows(d), rows(d)],
        out_shape=[jax.ShapeDtypeStruct((m, d), F32), jax.ShapeDtypeStruct((m, d), BF16)],
        compiler_params=_params("parallel", "parallel"),
        name="cross_attention",
    )(h, x, kv, wq, wo, g_post.reshape(1, d), g_next.reshape(1, d))


def _ffn_up_kernel(h_ref, wg_ref, wu_ref, o_ref):
    h = h_ref[...]
    gate = _dot(h, wg_ref[...])
    up = _dot(h, wu_ref[...])
    o_ref[...] = (gate * jax.nn.sigmoid(gate) * up).astype(BF16)


def ffn_up(h, wg, wu, *, bm, bn):
    m, d = h.shape
    n = wg.shape[1]
    wspec = pl.BlockSpec((d, bn), lambda i, j: (0, j))
    return pl.pallas_call(
        _ffn_up_kernel,
        grid=(m // bm, n // bn),
        in_specs=[pl.BlockSpec((bm, d), lambda i, j: (i, 0)), wspec, wspec],
        out_specs=pl.BlockSpec((bm, bn), lambda i, j: (i, j)),
        out_shape=jax.ShapeDtypeStruct((m, n), BF16),
        compiler_params=_params("parallel", "arbitrary"),
        name="ffn_up",
    )(h, wg, wu)


def _ffn_down_kernel(*refs, emit_h):
    if emit_h:
        a_ref, w_ref, x_ref, gp_ref, gn_ref, xo_ref, ho_ref = refs
    else:
        a_ref, w_ref, x_ref, gp_ref, xo_ref = refs
        gn_ref = ho_ref = None
    kk = pl.program_id(1)
    last = pl.num_programs(1) - 1
    part = _dot(a_ref[...], w_ref[...])

    @pl.when(kk == 0)
    def _():
        xo_ref[...] = part

    @pl.when((kk > 0) & (kk < last))
    def _():
        xo_ref[...] += part

    @pl.when(kk == last)
    def _():
        g_next = gn_ref[...] if emit_h else None
        _residual_tail(xo_ref[...] + part, x_ref[...], gp_ref[...], g_next, xo_ref, ho_ref)


def ffn_down(a, w, x, g_post, g_next, *, bm, bk):
    m, d = x.shape
    kdim = a.shape[1]
    emit_h = g_next is not None
    assert kdim // bk >= 2
    rows = pl.BlockSpec((bm, d), lambda i, k: (i, 0))
    const = pl.BlockSpec((1, d), lambda i, k: (0, 0))
    gains = [g_post.reshape(1, d)] + ([g_next.reshape(1, d)] if emit_h else [])
    out_shape = [jax.ShapeDtypeStruct((m, d), F32)] + ([jax.ShapeDtypeStruct((m, d), BF16)] if emit_h else [])
    return pl.pallas_call(
        functools.partial(_ffn_down_kernel, emit_h=emit_h),
        grid=(m // bm, kdim // bk),
        in_specs=[pl.BlockSpec((bm, bk), lambda i, k: (i, k)),
                  pl.BlockSpec((bk, d), lambda i, k: (k, 0)),
                  rows] + [const] * len(gains),
        out_specs=[rows] * len(out_shape),
        out_shape=out_shape,
        compiler_params=_params("parallel", "arbitrary"),
        name="ffn_down",
    )(a, w, x, *gains)


def _pick(n, *candidates):
    for c in candidates:
        if n % c == 0:
            return c
    return n


def _columns(w, names):
    return jnp.concatenate([w[..., _IN_OFFSETS[n][0]:_IN_OFFSETS[n][1]] for n in names], axis=-1)


def kernel(x, mem, positions, g_mix_pre, g_mix_post, w_in, b_f, lam_q1, lam_k1, lam_q2, lam_k2, g_diff_sub, g_sb_out, g_fox_out, w_out, g_x_pre, g_x_post, g_mem, w_cq, w_ckv, w_co, g_ffn_pre, g_ffn_post, w_gate, w_up, w_down):
    batch, seq, d = x.shape
    depth = w_in.shape[0]
    m = batch * seq
    d_ff = w_gate.shape[2]
    assert w_in.shape[2] == _IN_OFFSETS["fc"][1] and d == V_ROWS

    bm_row = _pick(m, 512, 256, 128)
    bm_mm = _pick(m, 1024, 512, 256, 128)
    bm_big = _pick(m, 2048, 1024, 512, 256, 128)
    lag_diff, lag_sb, lag_fox = 2, 2, 3
    bn_in = 2 * A_W
    bn_ff = _pick(d_ff, 512, 256, 128)
    bk_ff = _pick(d_ff, 2816, 1408, 1024, 512, 256, 128)
    tq = _pick(seq, 256, 128)
    bm_x = _pick(seq, 512, 256, 128)
    bm_mem = _pick(mem.shape[0] * mem.shape[1], 512, 256, 128)

    w_qk = _columns(w_in, QK_ORDER).astype(BF16)
    w_qk_rope = _columns(w_in, QK_ROPE_ORDER).astype(BF16)
    w_vt = jnp.swapaxes(_columns(w_in, V_ORDER), 1, 2).astype(BF16)
    gate_lane_head = jnp.arange(2 * GATE_LANES) % GATE_LANES // GATE_PARTS
    wf_rep = jnp.pad(_columns(w_in, ("fc",))[:, :, gate_lane_head],
                     ((0, 0), (0, 0), (0, LANES - 2 * GATE_LANES))).astype(BF16)
    bf_rep = jnp.pad(b_f[:, gate_lane_head], ((0, 0), (0, LANES - 2 * GATE_LANES))).reshape(depth, 1, LANES)
    w_out_b, w_cq_b, w_ckv_b, w_co_b = (w.astype(BF16) for w in (w_out, w_cq, w_ckv, w_co))
    w_gate_b, w_up_b, w_down_b = (w.astype(BF16) for w in (w_gate, w_up, w_down))
    lam_q = jnp.concatenate([lam_q1, lam_q2], axis=1).reshape(depth, 1, LANES)
    lam_k = jnp.concatenate([lam_k1, lam_k2], axis=1).reshape(depth, 1, LANES)

    def scales(*groups):
        return jnp.concatenate([jnp.full((w,), v, F32) for w, v in groups]).reshape(1, -1)

    colscale = scales((SB_W, HEAD_DIM ** -0.5), (SB_W, 1.0), (FOX_W, HEAD_DIM ** -0.5 * LOG2E), (FOX_W, 1.0))
    colscale_rope = scales((A_W, DIFF_QK_DIM ** -0.5 * LOG2E), (A_W, 1.0))

    x = x.reshape(m, d)
    mem2 = mem.reshape(-1, d)
    cos_t, sin_t = rope_tables(positions, bm=bm_row)
    h = prenorm(x, g_mix_pre[0], bm=bm_row)

    for l in range(depth):
        lam_init = 0.8 - 0.6 * math.exp(-0.3 * l)
        qk = in_proj(h, w_qk[l], colscale, None, bm=bm_big, bn=bn_in)
        qk_rope = in_proj(h, w_qk_rope[l], colscale_rope, (cos_t, sin_t), bm=bm_big, bn=bn_in)
        vt = in_proj_vt(h, w_vt[l], bm=bm_big, bn=bn_in, tk=tq)
        fk, fq = forget_features(h, wf_rep[l], bf_rep[l], batch=batch, blk=tq)
        oa = diff_attention(qk_rope, vt, lam_q[l], lam_k[l], g_diff_sub[l].reshape(1, LANES), lam_init, batch=batch, tq=tq,
                            lag=lag_diff)
        ob = stick_breaking_attention(qk, vt, batch=batch, tq=tq, lag=lag_sb)
        oc = forgetting_attention(qk, vt, fk, fq, batch=batch, tq=tq, lag=lag_fox)
        x, h = mix_out(oa, ob, oc, x, w_out_b[l], g_sb_out[l], g_fox_out[l], g_mix_post[l], g_x_pre[l], bm=bm_row)
        kv = norm_matmul(mem2, g_mem[l], w_ckv_b[l], bm=bm_mem)
        x, h = cross_attention(h, x, kv, w_cq_b[l], w_co_b[l], g_x_post[l], g_ffn_pre[l], batch=batch, bm=bm_x)
        a = ffn_up(h, w_gate_b[l], w_up_b[l], bm=bm_mm, bn=bn_ff)
        g_next = g_mix_pre[l + 1] if l + 1 < depth else None
        x, *rest = ffn_down(a, w_down_b[l], x, g_ffn_post[l], g_next, bm=bm_row, bk=bk_ff)
        h = rest[0] if rest else None
    return x.reshape(batch, seq, d)
```
